```python
import jax, jax.numpy as jnp
from jax import lax
import numpy as np

D_MODEL = 1024
BATCH = 2
SEQ = 8192
DEPTH = 2
DEC_BATCH = 32
DEC_SEQ = 16
PAST_LEN = 2048

CHUNK = 64
N_HEADS = 8
HEAD_DIM = 64
ROT_DIM = HEAD_DIM // 4
ROPE_THETA = 500000.0
N_IDX_HEADS = 8
IDX_DIM = 64
TOPK_MAX = 256
Q_BLOCK = 128
D_CONV = 512
CONV_W = 31
D_FF = 4 * D_MODEL
EPS = 1e-6
LN_EPS = 1e-5
ATTN_SCALE = HEAD_DIM ** -0.5
IDX_SCALE = (N_IDX_HEADS ** -0.5) * (IDX_DIM ** -0.5)
SPLITS = (D_CONV, D_CONV, N_HEADS * HEAD_DIM, N_HEADS * HEAD_DIM, N_HEADS * HEAD_DIM,
          N_IDX_HEADS * IDX_DIM, IDX_DIM, N_IDX_HEADS, D_MODEL, D_MODEL)
D_IN = 2 * D_CONV + 3 * N_HEADS * HEAD_DIM + N_IDX_HEADS * IDX_DIM + IDX_DIM + N_IDX_HEADS + 2 * D_MODEL

kernel_name = "chunk_causal_conformer_dsa_hybrid_step"


def rms_norm(x, g):
    xf = x.astype(jnp.float32)
    y = xf * lax.rsqrt(jnp.mean(xf * xf, axis=-1, keepdims=True) + EPS)
    return (y * g.astype(jnp.float32)).astype(x.dtype)


def layer_norm(x, g, b):
    xf = x.astype(jnp.float32)
    mu = jnp.mean(xf, axis=-1, keepdims=True)
    var = jnp.mean(jnp.square(xf - mu), axis=-1, keepdims=True)
    y = (xf - mu) * lax.rsqrt(var + LN_EPS)
    return (y * g.astype(jnp.float32) + b.astype(jnp.float32)).astype(x.dtype)


def partial_rope(x, pos):
    half = ROT_DIM // 2
    inv = ROPE_THETA ** (-jnp.arange(half, dtype=jnp.float32) / half)
    ang = pos.astype(jnp.float32)[:, None] * inv[None, :]
    cos = jnp.cos(ang)[:, None, :]
    sin = jnp.sin(ang)[:, None, :]
    xf = x.astype(jnp.float32)
    x1 = xf[..., :half]
    x2 = xf[..., half:ROT_DIM]
    out = jnp.concatenate([x1 * cos - x2 * sin, x2 * cos + x1 * sin, xf[..., ROT_DIM:]], axis=-1)
    return out.astype(x.dtype)


def sparse_attend(q, qi, iw, q_pos, k, v, ki, k_pos, n_sel):
    s_idx = jnp.einsum('bqhd,bld->bqhl', qi, ki)
    score = jnp.einsum('bqh,bqhl->bql', iw.astype(jnp.float32),
                       jax.nn.relu(s_idx).astype(jnp.float32)) * IDX_SCALE
    admiss = (k_pos[None, :] // CHUNK) <= (q_pos[:, None] // CHUNK)
    score = jnp.where(admiss[None], score, -jnp.inf)
    vals, idx = lax.top_k(score, n_sel)
    valid = jnp.isfinite(vals)
    gather = jax.vmap(lambda tb, ib: tb[ib])
    kg = gather(k, idx)
    vg = gather(v, idx)
    s = jnp.einsum('bqhd,bqkhd->bqhk', q, kg).astype(jnp.float32) * ATTN_SCALE
    s = jnp.where(valid[:, :, None, :], s, -jnp.inf)
    p = jax.nn.softmax(s, axis=-1)
    return jnp.einsum('bqhk,bqkhd->bqhd', p.astype(vg.dtype), vg)


def dsa_prompt(q, qi, iw, k, v, ki, pos, n_sel):
    B, T = q.shape[:2]
    nb = T // Q_BLOCK

    def blk(a):
        return jnp.moveaxis(a.reshape((B, nb, Q_BLOCK) + a.shape[2:]), 1, 0)

    def body(args):
        qb, qib, iwb, pb = args
        return sparse_attend(qb, qib, iwb, pb, k, v, ki, pos, n_sel)

    o = lax.map(body, (blk(q), blk(qi), blk(iw), pos.reshape(nb, Q_BLOCK)))
    return jnp.moveaxis(o, 0, 1).reshape(q.shape)


def trunk_layer(x, pos, conv_prev, k_prev, v_prev, ki_prev,
                norm_mix, w_in, conv_w, conv_b, conv_ln_g, conv_ln_b, w_conv_out,
                q_norm, k_norm, w_attn_out, w_out, norm_ffn, w_ff1, w_ff2):
    B, T, _ = x.shape
    h = rms_norm(x, norm_mix)
    z = h @ w_in
    parts = []
    off = 0
    for n in SPLITS:
        parts.append(z[..., off:off + n])
        off += n
    glu_a, glu_b, q, k, v, qi, ki, iw, g_conv, g_attn = parts

    glu = glu_a * jax.nn.sigmoid(glu_b)
    conv_in = jnp.concatenate([conv_prev.astype(glu.dtype), glu], axis=1)
    new_conv = conv_in[:, -(CONV_W - 1):]
    dw = lax.conv_general_dilated(conv_in, conv_w[:, None, :].astype(conv_in.dtype), (1,), 'VALID',
                                  dimension_numbers=('NWC', 'WIO', 'NWC'),
                                  feature_group_count=D_CONV) + conv_b
    conv_out = jax.nn.silu(layer_norm(dw, conv_ln_g, conv_ln_b)) @ w_conv_out

    q = partial_rope(rms_norm(q.reshape(B, T, N_HEADS, HEAD_DIM), q_norm), pos)
    k = partial_rope(rms_norm(k.reshape(B, T, N_HEADS, HEAD_DIM), k_norm), pos)
    v = v.reshape(B, T, N_HEADS, HEAD_DIM)
    qi = partial_rope(qi.reshape(B, T, N_IDX_HEADS, IDX_DIM), pos)
    ki = partial_rope(ki[:, :, None, :], pos)[:, :, 0, :]
    if k_prev is None:
        o = dsa_prompt(q, qi, iw, k, v, ki, pos, min(TOPK_MAX, T // 4))
    else:
        L = k_prev.shape[1] + T
        k_all = jnp.concatenate([k_prev.astype(k.dtype), k], axis=1)
        v_all = jnp.concatenate([v_prev.astype(v.dtype), v], axis=1)
        ki_all = jnp.concatenate([ki_prev.astype(ki.dtype), ki], axis=1)
        o = sparse_attend(q, qi, iw, pos, k_all, v_all, ki_all,
                          jnp.arange(L, dtype=jnp.int32), min(TOPK_MAX, L // 4))
    attn_out = o.reshape(B, T, N_HEADS * HEAD_DIM) @ w_attn_out

    merged = jax.nn.sigmoid(g_conv) * conv_out + jax.nn.sigmoid(g_attn) * attn_out
    x = x + merged @ w_out
    x = x + jnp.square(jax.nn.relu(rms_norm(x, norm_ffn) @ w_ff1)) @ w_ff2
    return x, k, v, ki, new_conv


def setup_inputs(seed: int = 0) -> dict:
    key = jax.random.key(seed)
    ks = jax.random.split(key, 24)
    f32 = jnp.float32
    nrm = lambda k, s, sc: jax.random.normal(k, s, f32) * sc
    return {
        "x_prompt": nrm(ks[0], (BATCH, SEQ, D_MODEL), 1.0),
        "x_sample": nrm(ks[1], (DEC_BATCH, DEC_SEQ, D_MODEL), 1.0),
        "cache_k": nrm(ks[2], (DEPTH, DEC_BATCH, PAST_LEN, N_HEADS, HEAD_DIM), 1.0),
        "cache_v": nrm(ks[3], (DEPTH, DEC_BATCH, PAST_LEN, N_HEADS, HEAD_DIM), 1.0),
        "cache_kidx": nrm(ks[4], (DEPTH, DEC_BATCH, PAST_LEN, IDX_DIM), 1.0),
        "state_conv": nrm(ks[5], (DEPTH, DEC_BATCH, CONV_W - 1, D_CONV), 0.5),
        "norm_mix": 1.0 + nrm(ks[6], (DEPTH, D_MODEL), 0.01),
        "w_in": nrm(ks[7], (DEPTH, D_MODEL, D_IN), D_MODEL ** -0.5),
        "conv_w": nrm(ks[8], (DEPTH, CONV_W, D_CONV), CONV_W ** -0.5),
        "conv_b": nrm(ks[9], (DEPTH, D_CONV), 0.01),
        "conv_ln_g": 1.0 + nrm(ks[10], (DEPTH, D_CONV), 0.01),
        "conv_ln_b": nrm(ks[11], (DEPTH, D_CONV), 0.01),
        "w_conv_out": nrm(ks[12], (DEPTH, D_CONV, D_MODEL), D_CONV ** -0.5),
        "q_norm": 1.0 + nrm(ks[13], (DEPTH, HEAD_DIM), 0.01),
        "k_norm": 1.0 + nrm(ks[14], (DEPTH, HEAD_DIM), 0.01),
        "w_attn_out": nrm(ks[15], (DEPTH, N_HEADS * HEAD_DIM, D_MODEL), (N_HEADS * HEAD_DIM) ** -0.5),
        "w_out": nrm(ks[16], (DEPTH, D_MODEL, D_MODEL), D_MODEL ** -0.5),
        "norm_ffn": 1.0 + nrm(ks[17], (DEPTH, D_MODEL), 0.01),
        "w_ff1": nrm(ks[18], (DEPTH, D_MODEL, D_FF), D_MODEL ** -0.5),
        "w_ff2": nrm(ks[19], (DEPTH, D_FF, D_MODEL), 0.5 * D_FF ** -0.5),
    }


def reference(x_prompt, x_sample, cache_k, cache_v, cache_kidx, state_conv,
              norm_mix, w_in, conv_w, conv_b, conv_ln_g, conv_ln_b, w_conv_out,
              q_norm, k_norm, w_attn_out, w_out, norm_ffn, w_ff1, w_ff2):
    B, T = x_prompt.shape[:2]
    Td = x_sample.shape[1]
    P = cache_k.shape[2]
    pos_p = jnp.arange(T, dtype=jnp.int32)
    pos_s = P + jnp.arange(Td, dtype=jnp.int32)
    hp, hs = x_prompt, x_sample
    kp_l, vp_l, kip_l, cp_l = [], [], [], []
    ks_l, vs_l, kis_l, cs_l = [], [], [], []
    for l in range(DEPTH):
        w = (norm_mix[l], w_in[l], conv_w[l], conv_b[l], conv_ln_g[l], conv_ln_b[l],
             w_conv_out[l], q_norm[l], k_norm[l], w_attn_out[l], w_out[l],
             norm_ffn[l], w_ff1[l], w_ff2[l])
        zero_pad = jnp.zeros((B, CONV_W - 1, D_CONV), hp.dtype)
        hp, kp, vp, kip, cp = trunk_layer(hp, pos_p, zero_pad, None, None, None, *w)
        hs, ks_, vs, kis, cs = trunk_layer(hs, pos_s, state_conv[l], cache_k[l], cache_v[l],
                                           cache_kidx[l], *w)
        kp_l.append(kp); vp_l.append(vp); kip_l.append(kip); cp_l.append(cp)
        ks_l.append(ks_); vs_l.append(vs); kis_l.append(kis); cs_l.append(cs)
    return (hp, hs,
            jnp.stack(kp_l), jnp.stack(vp_l), jnp.stack(kip_l), jnp.stack(cp_l),
            jnp.stack(ks_l), jnp.stack(vs_l), jnp.stack(kis_l), jnp.stack(cs_l))
```

```python
import functools

import jax
import jax.numpy as jnp
import numpy as np
from jax import lax
from jax.experimental import pallas as pl
from jax.experimental.pallas import tpu as pltpu

F32 = jnp.float32
BF16 = jnp.bfloat16
I32 = jnp.int32

D_MODEL = 1024
CHUNK = 64
CHUNK_SHIFT = 6
N_HEADS = 8
HEAD_DIM = 64
ROT_DIM = HEAD_DIM // 4
ROT_HALF = ROT_DIM // 2
ROPE_THETA = 500000.0
N_IDX_HEADS = 8
IDX_DIM = 64
TOPK_MAX = 256
D_CONV = 512
CONV_W = 31
D_FF = 4 * D_MODEL
EPS = 1e-6
LN_EPS = 1e-5
ATTN_SCALE = HEAD_DIM ** -0.5
IDX_SCALE = (N_IDX_HEADS ** -0.5) * (IDX_DIM ** -0.5)
D_ATT = N_HEADS * HEAD_DIM

LANES = 128
SUBLANES = 8
VMEM_LIMIT_BYTES = 56 * 1024 * 1024

COL_KIW = 6 * 512
COL_GC = COL_KIW + LANES
COL_GA = COL_GC + D_MODEL
D_IN_PACKED = COL_GA + D_MODEL
D_IN_HEAD = 2 * D_CONV + 3 * D_ATT + N_IDX_HEADS * IDX_DIM + IDX_DIM + N_IDX_HEADS

HALO = 32
INF = float("inf")
F32_MAX = float(np.finfo(np.float32).max)
V_SLAB = HEAD_DIM + 16
M_FLOOR = -1e30
MASK_BIAS = -2e30
SEARCH_FAST_ITERS = 16
SEARCH_MAX_ITERS = 4096


def _cparams(sem):
    return pltpu.CompilerParams(dimension_semantics=sem, vmem_limit_bytes=VMEM_LIMIT_BYTES)


def _resident(block_shape, index_map):
    return pl.BlockSpec(block_shape, index_map, pipeline_mode=pl.Buffered(1))


def _rope(x, c, sa, sb):
    return x * c + pltpu.roll(x, LANES - ROT_HALF, 1) * sa + pltpu.roll(x, ROT_HALF, 1) * sb


def _head_rms(x, gain):
    lane = lax.broadcasted_iota(I32, x.shape, 1)
    lo = lane < HEAD_DIM
    x2 = x * x
    s_lo = jnp.sum(jnp.where(lo, x2, 0.0), axis=-1, keepdims=True)
    s_hi = jnp.sum(jnp.where(lo, 0.0, x2), axis=-1, keepdims=True)
    ms = jnp.where(lo, s_lo, s_hi) * (1.0 / HEAD_DIM)
    return x * lax.rsqrt(ms + EPS) * gain


def _inproj_body(x_ref, g_ref, w_ref, cos_ref, sa_ref, sb_ref, qg_ref, kg_ref,
                 glu_ref, q_ref, k_ref, kb_ref, v_ref, vb_ref, qi_ref, kiw_ref, gc_ref, ga_ref):
    x = x_ref[...]
    ms = jnp.mean(x * x, axis=-1, keepdims=True)
    h = (x * lax.rsqrt(ms + EPS) * g_ref[...]).astype(BF16)

    def proj(c0, n):
        return jnp.dot(h, w_ref[:, c0:c0 + n], preferred_element_type=F32)

    glu_ref[...] = proj(0, D_CONV) * jax.nn.sigmoid(proj(D_CONV, D_CONV))

    c, sa, sb = cos_ref[...], sa_ref[...], sb_ref[...]
    zq = proj(2 * D_CONV, D_ATT)
    zk = proj(2 * D_CONV + D_ATT, D_ATT)
    zqi = proj(2 * D_CONV + 3 * D_ATT, N_IDX_HEADS * IDX_DIM)
    for g in range(D_ATT // LANES):
        sl = slice(g * LANES, (g + 1) * LANES)
        qh = _rope(_head_rms(zq[:, sl], qg_ref[...]), c, sa, sb)
        q_ref[:, sl] = (qh * ATTN_SCALE).astype(BF16)
        kh = _rope(_head_rms(zk[:, sl], kg_ref[...]), c, sa, sb)
        k_ref[:, sl] = kh
        kb_ref[:, sl] = kh.astype(BF16)
        qi_ref[:, sl] = _rope(zqi[:, sl], c, sa, sb).astype(BF16)

    zv = proj(2 * D_CONV + 2 * D_ATT, D_ATT)
    v_ref[...] = zv
    vb_ref[...] = zv.astype(BF16)

    zkiw = proj(COL_KIW, LANES)
    lane = lax.broadcasted_iota(I32, zkiw.shape, 1)
    is_ki = lane < IDX_DIM
    roped = _rope(zkiw, jnp.where(is_ki, c, 1.0), jnp.where(is_ki, sa, 0.0), jnp.where(is_ki, sb, 0.0))
    kiw_ref[...] = jnp.where(is_ki, roped, zkiw * IDX_SCALE)

    gc_ref[...] = jax.nn.sigmoid(proj(COL_GC, D_MODEL))
    ga_ref[...] = jax.nn.sigmoid(proj(COL_GA, D_MODEL))


def _inproj(x, g, w, cos, sa, sb, qg, kg, tm, n_tab_blocks):
    n = x.shape[0]
    row = lambda width: pl.BlockSpec((tm, width), lambda i: (i, 0))
    tab = pl.BlockSpec((tm, LANES), lambda i: (i % n_tab_blocks, 0))
    vec = lambda width: pl.BlockSpec((1, width), lambda i: (0, 0))
    out_shapes = [
        jax.ShapeDtypeStruct((n, D_CONV), F32),
        jax.ShapeDtypeStruct((n, D_ATT), BF16),
        jax.ShapeDtypeStruct((n, D_ATT), F32),
        jax.ShapeDtypeStruct((n, D_ATT), BF16),
        jax.ShapeDtypeStruct((n, D_ATT), F32),
        jax.ShapeDtypeStruct((n, D_ATT), BF16),
        jax.ShapeDtypeStruct((n, D_ATT), BF16),
        jax.ShapeDtypeStruct((n, LANES), F32),
        jax.ShapeDtypeStruct((n, D_MODEL), F32),
        jax.ShapeDtypeStruct((n, D_MODEL), F32),
    ]
    out_specs = [row(s.shape[1]) for s in out_shapes]
    return pl.pallas_call(
        _inproj_body,
        grid=(n // tm,),
        in_specs=[row(D_MODEL), vec(D_MODEL), _resident((D_MODEL, D_IN_PACKED), lambda i: (0, 0)),
                  tab, tab, tab, vec(LANES), vec(LANES)],
        out_specs=out_specs,
        out_shape=out_shapes,
        compiler_params=_cparams(("parallel",)),
        name="inproj",
    )(x, g, w, cos, sa, sb, qg, kg)


def _conv_body(cur_ref, halo_ref, st_ref, cw_ref, cb_ref, lg_ref, lb_ref, wco_ref, gc_ref,
               out_ref, win_ref, act_ref, *, tm, rows):
    i = pl.program_id(1)
    win_ref[0:HALO, :] = jnp.where(i == 0, st_ref[0], halo_ref[0])
    win_ref[HALO:HALO + tm, :] = cur_ref[0]
    off = HALO - (CONV_W - 1)
    for r in range(tm // rows):
        acc = jnp.zeros((rows, D_CONV), F32) + cb_ref[...]
        for j in range(CONV_W):
            acc = acc + win_ref[r * rows + off + j:r * rows + off + j + rows, :] * cw_ref[j:j + 1, :]
        mu = jnp.mean(acc, axis=-1, keepdims=True)
        d = acc - mu
        var = jnp.mean(d * d, axis=-1, keepdims=True)
        y = d * lax.rsqrt(var + LN_EPS) * lg_ref[...] + lb_ref[...]
        act_ref[r * rows:(r + 1) * rows, :] = (y * jax.nn.sigmoid(y)).astype(BF16)
    out_ref[0] = gc_ref[0] * jnp.dot(act_ref[...], wco_ref[...], preferred_element_type=F32)


def _conv_branch(glu, halo_src, state, cw, cb, lg, lb, wco, gc, tm):
    b, t, _ = glu.shape
    rows = min(tm, 32)
    hb = tm // HALO
    vec = pl.BlockSpec((1, D_CONV), lambda bi, i: (0, 0))
    return pl.pallas_call(
        functools.partial(_conv_body, tm=tm, rows=rows),
        grid=(b, t // tm),
        in_specs=[
            pl.BlockSpec((1, tm, D_CONV), lambda bi, i: (bi, i, 0)),
            pl.BlockSpec((1, HALO, D_CONV), lambda bi, i: (bi, jnp.maximum(i * hb - 1, 0), 0)),
            pl.BlockSpec((1, HALO, D_CONV), lambda bi, i: (bi, 0, 0)),
            pl.BlockSpec((HALO, D_CONV), lambda bi, i: (0, 0)),
            vec, vec, vec,
            _resident((D_CONV, D_MODEL), lambda bi, i: (0, 0)),
            pl.BlockSpec((1, tm, D_MODEL), lambda bi, i: (bi, i, 0)),
        ],
        out_specs=pl.BlockSpec((1, tm, D_MODEL), lambda bi, i: (bi, i, 0)),
        out_shape=jax.ShapeDtypeStruct((b, t, D_MODEL), F32),
        scratch_shapes=[pltpu.VMEM((HALO + tm, D_CONV), F32), pltpu.VMEM((tm, D_CONV), BF16)],
        compiler_params=_cparams(("parallel", "arbitrary")),
        name="conv_branch",
    )(glu, halo_src, state, cw, cb, lg, lb, wco, gc)


def _dsa_body(qT_ref, qiT_ref, iwT_ref, k_ref, vT_ref, ki_ref, oT_ref,
              sc_ref, qpad_ref, bias_ref, s_ref, acc_ref, m_ref, bm_ref,
              *, qb, lb, sub, causal_blocks, q_base, nq_valid, l_valid, n_sel):
    j = pl.program_id(1)
    n_kb = (j + 1) if causal_blocks else (sc_ref.shape[0] // lb)
    q0 = q_base + j * qb
    lane = lax.broadcasted_iota(I32, (1, qb), 1)
    qchunk = (q0 + lane) >> CHUNK_SHIFT
    kf = float(n_sel)
    groups = lb // SUBLANES

    def rows(kb):
        return pl.ds(pl.multiple_of(kb * lb, lb), lb)

    def fold(x, op):
        return op(x.reshape(x.shape[0] // SUBLANES, SUBLANES, qb), axis=0)

    def fin(x8, op):
        return op(x8, axis=0, keepdims=True)

    def idx_block(kb, carry):
        mx8, mn8, n8 = carry
        r0 = pl.multiple_of(kb * lb, lb)
        for s in range(lb // sub):
            kis = ki_ref[0, pl.ds(r0 + s * sub, sub), :]
            score = jnp.zeros((sub, qb), F32)
            for h in range(N_IDX_HEADS):
                sh = jnp.dot(kis, qiT_ref[0, h * IDX_DIM:(h + 1) * IDX_DIM, :], preferred_element_type=F32)
                score = score + jnp.maximum(sh, 0.0) * iwT_ref[0, h:h + 1, :]
            kpos = r0 + s * sub + lax.broadcasted_iota(I32, (sub, qb), 0)
            adm = jnp.where(kpos < l_valid, kpos >> CHUNK_SHIFT, qchunk + 1) <= qchunk
            sc_ref[pl.ds(r0 + s * sub, sub), :] = jnp.where(adm, score, -INF)
            mx8 = jnp.maximum(mx8, fold(jnp.where(adm, score, -INF), jnp.max))
            mn8 = jnp.minimum(mn8, fold(jnp.where(adm, score, INF), jnp.min))
            n8 = n8 + fold(jnp.where(adm, 1.0, 0.0), jnp.sum)
        return mx8, mn8, n8

    full8 = lambda v: jnp.full((SUBLANES, qb), v, F32)
    mx8, mn8, n8 = lax.fori_loop(0, n_kb, idx_block, (full8(-INF), full8(INF), full8(0.0)))
    mx, mn, n_adm = fin(mx8, jnp.max), fin(mn8, jnp.min), fin(n8, jnp.sum)

    def scan(p, lo, up, snap):
        def blk(kb, c):
            x = sc_ref[rows(kb), :]
            out = [c[0] + fold(jnp.where(x >= p, 1.0, 0.0), jnp.sum)]
            if snap:
                out.append(jnp.minimum(c[1], fold(jnp.where(x >= lo, x, INF), jnp.min)))
                out.append(jnp.maximum(c[2], fold(jnp.where(x < up, x, -INF), jnp.max)))
            return tuple(out)
        init = (full8(0.0), full8(INF), full8(-INF)) if snap else (full8(0.0),)
        res = lax.fori_loop(0, n_kb, blk, init)
        if snap:
            return fin(res[0], jnp.sum), fin(res[1], jnp.min), fin(res[2], jnp.max)
        return fin(res[0], jnp.sum)

    def pivot(lo, c_lo, up, c_up, bisect):
        xu = jnp.where(up == INF, mx, up)
        fl = jnp.log(c_lo)
        fu = jnp.log(jnp.maximum(c_up, 0.5))
        frac = (fl - float(np.log(n_sel + 0.5))) / (fl - fu)
        p = jnp.where(bisect, 0.5 * lo + 0.5 * xu, lo + (xu - lo) * frac)
        p = jnp.minimum(p, xu)
        stuck = jnp.logical_not(p > lo)
        return jnp.where(stuck, xu, p), stuck

    def update(p, c, lo, c_lo, up, c_up, live):
        ge = c >= kf
        to_lo = jnp.logical_and(live, ge)
        to_up = jnp.logical_and(live, jnp.logical_not(ge))
        return (jnp.where(to_lo, p, lo), jnp.where(to_lo, c, c_lo),
                jnp.where(to_up, p, up), jnp.where(to_up, c, c_up))

    def any_set(flag):
        return jnp.max(jnp.where(flag, 1, 0))

    def count_pass(st):
        it, _, _, lo, c_lo, up, c_up, done = st
        live = done == 0
        p, stuck = pivot(lo, c_lo, up, c_up, it % 3 == 2)
        c = scan(p, lo, up, False)
        lo, c_lo, up, c_up = update(p, c, lo, c_lo, up, c_up, live)
        done = jnp.where(c_lo == kf, 1, done)
        return (it + 1, any_set(done == 0), any_set(jnp.logical_and(stuck, live)), lo, c_lo, up, c_up, done)

    def snap_pass(st):
        it, _, lo, c_lo, up, c_up, done, tie = st
        live = done == 0
        p, _ = pivot(lo, c_lo, up, c_up, it % 2 == 1)
        c, a, b = scan(p, lo, up, True)
        tied = jnp.logical_and(live, a == b)
        lo2, c_lo2, up2, c_up2 = update(p, c, lo, c_lo, up, c_up, jnp.logical_and(live, jnp.logical_not(tied)))
        lo2 = jnp.where(live, jnp.maximum(lo2, a), lo2)
        tie = jnp.where(tied, 1, tie)
        done = jnp.where(jnp.logical_or(tied, c_lo2 == kf), 1, done)
        return (it + 1, any_set(done == 0), lo2, c_lo2, up2, c_up2, done, tie)

    few = n_adm <= kf
    done0 = jnp.where(jnp.logical_or(few, lane >= nq_valid), 1, 0)
    lo0 = jnp.where(few, -INF, mn)
    row = lambda v: jnp.full((1, qb), v, F32)
    st = (jnp.int32(0), any_set(done0 == 0), jnp.int32(0), lo0, n_adm, row(INF), row(0.0), done0)
    st = lax.while_loop(lambda s: (s[0] < SEARCH_FAST_ITERS) & (s[1] > 0) & (s[2] == 0), count_pass, st)
    it1, active1, _, lo1, c_lo1, up1, c_up1, done1 = st
    st = (it1, active1, lo1, c_lo1, up1, c_up1, done1, jnp.zeros((1, qb), I32))
    st = lax.while_loop(lambda s: (s[0] < SEARCH_MAX_ITERS) & (s[1] > 0), snap_pass, st)
    thr, tie = st[2], st[7]
    has_ties = any_set(jnp.logical_and(tie == 1, lane < nq_valid)) > 0
    thr_valid = jnp.maximum(thr, -F32_MAX)

    qpad_ref[...] = jnp.zeros(qpad_ref.shape, BF16)
    for h in range(N_HEADS):
        r = (h % 2) * HEAD_DIM
        qpad_ref[h, r:r + HEAD_DIM, :] = qT_ref[0, h * HEAD_DIM:(h + 1) * HEAD_DIM, :]

    def attention(with_ties):
        m_ref[...] = jnp.full(m_ref.shape, M_FLOOR, F32)
        acc_ref[...] = jnp.zeros(acc_ref.shape, F32)
        if with_ties:
            def gt_blk(kb, c):
                return c + fold(jnp.where(sc_ref[rows(kb), :] > thr, 1.0, 0.0), jnp.sum)
            need = kf - fin(lax.fori_loop(0, n_kb, gt_blk, full8(0.0)), jnp.sum)
            ri = lax.broadcasted_iota(I32, (lb, lb), 0)
            ci = lax.broadcasted_iota(I32, (lb, lb), 1)
            tri = jnp.where(ci <= ri, 1.0, 0.0).astype(BF16)

        def att_block(kb, seen):
            x = sc_ref[rows(kb), :]
            if with_ties:
                eq = x == thr
                eqf = jnp.where(eq, 1.0, 0.0)
                pref = jnp.dot(tri, eqf.astype(BF16), preferred_element_type=F32) + seen
                seen = seen + jnp.sum(eqf, axis=0, keepdims=True)
                take = jnp.where(eq, jnp.where(pref <= need, 1.0, 0.0), jnp.where(x > thr, 1.0, 0.0))
                bias_ref[...] = jnp.where(jnp.where(x > -INF, take, 0.0) > 0.5, 0.0, MASK_BIAS)
            else:
                bias_ref[...] = jnp.where(x >= thr_valid, 0.0, MASK_BIAS)
            for h in range(N_HEADS):
                p2 = (h // 2) * 2 * HEAD_DIM
                s = jnp.dot(k_ref[0, rows(kb), p2:p2 + 2 * HEAD_DIM], qpad_ref[h],
                            preferred_element_type=F32) + bias_ref[...]
                s_ref[h] = s
                bm_ref[h:h + 1, :] = jnp.max(s, axis=0, keepdims=True)
            m_old = m_ref[...]
            m_new = jnp.maximum(m_old, bm_ref[...])
            alpha = jnp.exp(m_old - m_new)
            m_ref[...] = m_new
            for h in range(N_HEADS):
                p = jnp.exp(s_ref[h] - m_new[h:h + 1, :]).astype(BF16)
                vs = slice(h * V_SLAB, (h + 1) * V_SLAB)
                pv = jnp.dot(vT_ref[0, vs, rows(kb)], p, preferred_element_type=F32)
                acc_ref[vs, :] = acc_ref[vs, :] * alpha[h:h + 1, :] + pv
            return seen

        lax.fori_loop(0, n_kb, att_block, jnp.zeros((1, qb), F32))
        for h in range(N_HEADS):
            num = acc_ref[h * V_SLAB:h * V_SLAB + HEAD_DIM, :]
            den = acc_ref[h * V_SLAB + HEAD_DIM:h * V_SLAB + HEAD_DIM + 1, :]
            oT_ref[0, h * HEAD_DIM:(h + 1) * HEAD_DIM, :] = num / den

    @pl.when(has_ties)
    def _():
        attention(True)

    @pl.when(jnp.logical_not(has_ties))
    def _():
        attention(False)


def _dsa(qT, qiT, iwT, k, vT, ki, *, qb, lb, causal_blocks, q_base, nq_valid, l_valid, n_sel):
    b, _, tq = qT.shape
    lp = k.shape[1]
    body = functools.partial(_dsa_body, qb=qb, lb=lb, sub=64, causal_blocks=causal_blocks,
                             q_base=q_base, nq_valid=nq_valid, l_valid=l_valid, n_sel=n_sel)
    return pl.pallas_call(
        body,
        grid=(b, tq // qb),
        in_specs=[
            pl.BlockSpec((1, D_ATT, qb), lambda bi, j: (bi, 0, j)),
            pl.BlockSpec((1, N_IDX_HEADS * IDX_DIM, qb), lambda bi, j: (bi, 0, j)),
            pl.BlockSpec((1, N_IDX_HEADS, qb), lambda bi, j: (bi, 0, j)),
            _resident((1, lp, D_ATT), lambda bi, j: (bi, 0, 0)),
            _resident((1, N_HEADS * V_SLAB, lp), lambda bi, j: (bi, 0, 0)),
            _resident((1, lp, IDX_DIM), lambda bi, j: (bi, 0, 0)),
        ],
        out_specs=pl.BlockSpec((1, D_ATT, qb), lambda bi, j: (bi, 0, j)),
        out_shape=jax.ShapeDtypeStruct((b, D_ATT, tq), F32),
        scratch_shapes=[
            pltpu.VMEM((lp, qb), F32),
            pltpu.VMEM((N_HEADS, 2 * HEAD_DIM, qb), BF16),
            pltpu.VMEM((lb, qb), F32),
            pltpu.VMEM((N_HEADS, lb, qb), F32),
            pltpu.VMEM((N_HEADS * V_SLAB, qb), F32),
            pltpu.VMEM((N_HEADS, qb), F32),
            pltpu.VMEM((N_HEADS, qb), F32),
        ],
        compiler_params=_cparams(("parallel", "arbitrary")),
        name="dsa",
    )(qT, qiT, iwT, k, vT, ki)


def _values_with_ones(vb3):
    b, l, _ = vb3.shape
    vt = jnp.transpose(vb3.reshape(b, l, N_HEADS, HEAD_DIM), (0, 2, 3, 1))
    ones = jnp.ones((b, N_HEADS, V_SLAB - HEAD_DIM, l), vb3.dtype)
    return jnp.concatenate([vt, ones], axis=2).reshape(b, N_HEADS * V_SLAB, l)


def _tail_body(x_ref, o_ref, mc_ref, ga_ref, wao_ref, wout_ref, g_ref, w1_ref, w2_ref, y_ref, *, ff_chunk):
    attn = jnp.dot(o_ref[...].astype(BF16), wao_ref[...], preferred_element_type=F32)
    merged = mc_ref[...] + ga_ref[...] * attn
    x1 = x_ref[...] + jnp.dot(merged.astype(BF16), wout_ref[...], preferred_element_type=F32)
    ms = jnp.mean(x1 * x1, axis=-1, keepdims=True)
    h = (x1 * lax.rsqrt(ms + EPS) * g_ref[...]).astype(BF16)
    y = x1
    for c in range(D_FF // ff_chunk):
        cs = slice(c * ff_chunk, (c + 1) * ff_chunk)
        u = jnp.maximum(jnp.dot(h, w1_ref[:, cs], preferred_element_type=F32), 0.0)
        y = y + jnp.dot((u * u).astype(BF16), w2_ref[cs, :], preferred_element_type=F32)
    y_ref[...] = y


def _tail(x, o, mc, ga, wao, wout, g, w1, w2, tm):
    n = x.shape[0]
    row = lambda width: pl.BlockSpec((tm, width), lambda i: (i, 0))
    full = lambda a: _resident(a.shape, lambda i: (0, 0))
    return pl.pallas_call(
        functools.partial(_tail_body, ff_chunk=1024),
        grid=(n // tm,),
        in_specs=[row(D_MODEL), row(D_ATT), row(D_MODEL), row(D_MODEL),
                  full(wao), full(wout), pl.BlockSpec((1, D_MODEL), lambda i: (0, 0)), full(w1), full(w2)],
        out_specs=row(D_MODEL),
        out_shape=jax.ShapeDtypeStruct((n, D_MODEL), F32),
        compiler_params=_cparams(("parallel",)),
        name="tail",
    )(x, o, mc, ga, wao, wout, g, w1, w2)


def _rope_tables(pos):
    inv = ROPE_THETA ** (-jnp.arange(ROT_HALF, dtype=F32) / ROT_HALF)
    ang = pos.astype(F32)[:, None] * inv[None, :]
    cos, sin = jnp.cos(ang), jnp.sin(ang)
    r = np.arange(LANES) % HEAD_DIM
    jj = r % ROT_HALF
    first = jnp.asarray(r < ROT_HALF)[None, :]
    second = jnp.asarray((r >= ROT_HALF) & (r < ROT_DIM))[None, :]
    c = jnp.where(first | second, cos[:, jj], 1.0)
    sa = jnp.where(first, -sin[:, jj], 0.0)
    sb = jnp.where(second, sin[:, jj], 0.0)
    return c, sa, sb


def _pack_w_in(w):
    pad = jnp.zeros((D_MODEL, LANES - (D_IN_HEAD - COL_KIW)), w.dtype)
    return jnp.concatenate([w[:, :D_IN_HEAD], pad, w[:, D_IN_HEAD:]], axis=1).astype(BF16)


def _tile2(v):
    return jnp.concatenate([v, v])[None, :].astype(F32)


def _layer(x, tabs, n_tab_blocks, conv_state, caches, w, *, tm, conv_tm, dsa_cfg):
    b, t, _ = x.shape
    n = b * t
    (norm_mix, w_in_p, conv_w, conv_b, ln_g, ln_b, w_conv_out, q_norm, k_norm,
     w_attn_out, w_out, norm_ffn, w_ff1, w_ff2) = w
    xf = x.reshape(n, D_MODEL)
    glu, q, k, kb, v, vb, qi, kiw, gc, ga = _inproj(
        xf, norm_mix[None, :], w_in_p, *tabs, _tile2(q_norm), _tile2(k_norm), tm, n_tab_blocks)

    glu3 = glu.reshape(b, t, D_CONV)
    state_p = jnp.pad(conv_state, ((0, 0), (HALO - (CONV_W - 1), 0), (0, 0)))
    halo_src = glu3 if t >= HALO else state_p
    cw_p = jnp.pad(conv_w, ((0, HALO - CONV_W), (0, 0)))
    mc = _conv_branch(glu3, halo_src, state_p, cw_p, conv_b[None, :], ln_g[None, :], ln_b[None, :],
                      w_conv_out, gc.reshape(b, t, D_MODEL), conv_tm)
    new_conv = jnp.concatenate([conv_state, glu3], axis=1)[:, -(CONV_W - 1):]

    ki = kiw[:, :IDX_DIM]
    qT = jnp.swapaxes(q.reshape(b, t, D_ATT), 1, 2)
    qiT = jnp.swapaxes(qi.reshape(b, t, D_ATT), 1, 2)
    iwT = jnp.swapaxes(kiw[:, IDX_DIM:IDX_DIM + N_IDX_HEADS].reshape(b, t, N_IDX_HEADS), 1, 2)
    kb3 = kb.reshape(b, t, D_ATT)
    vb3 = vb.reshape(b, t, D_ATT)
    kib3 = ki.astype(BF16).reshape(b, t, IDX_DIM)
    if caches is None:
        oT = _dsa(qT, qiT, iwT, kb3, _values_with_ones(vb3), kib3, **dsa_cfg)
    else:
        k_prev, v_prev, ki_prev = caches
        p = k_prev.shape[1]
        qb, lb = dsa_cfg["qb"], dsa_cfg["lb"]
        lp = -(-(p + t) // lb) * lb
        padq = lambda a: jnp.pad(a, ((0, 0), (0, 0), (0, qb - t)))
        padk = lambda a: jnp.pad(a, ((0, 0), (0, lp - p - t), (0, 0)))
        k_all = padk(jnp.concatenate([k_prev.reshape(b, p, D_ATT).astype(BF16), kb3], axis=1))
        v_all = padk(jnp.concatenate([v_prev.reshape(b, p, D_ATT).astype(BF16), vb3], axis=1))
        ki_all = padk(jnp.concatenate([ki_prev.astype(BF16), kib3], axis=1))
        oT = _dsa(padq(qT), padq(qiT), padq(iwT), k_all, _values_with_ones(v_all), ki_all, **dsa_cfg)[:, :, :t]
    o = jnp.swapaxes(oT, 1, 2).reshape(n, D_ATT)

    y = _tail(xf, o, mc.reshape(n, D_MODEL), ga, w_attn_out, w_out, norm_ffn[None, :], w_ff1, w_ff2, tm)
    return (y.reshape(b, t, D_MODEL), k.reshape(b, t, N_HEADS, HEAD_DIM), v.reshape(b, t, N_HEADS, HEAD_DIM),
            ki.reshape(b, t, IDX_DIM), new_conv)


def kernel(x_prompt, x_sample, cache_k, cache_v, cache_kidx, state_conv, norm_mix, w_in, conv_w, conv_b,
           conv_ln_g, conv_ln_b, w_conv_out, q_norm, k_norm, w_attn_out, w_out, norm_ffn, w_ff1, w_ff2):
    bp, tp, _ = x_prompt.shape
    bs, ts, _ = x_sample.shape
    depth = norm_mix.shape[0]
    past = cache_k.shape[2]
    tm = 256
    tabs_p = _rope_tables(jnp.arange(tp, dtype=I32))
    tabs_s = tuple(jnp.tile(a, (bs, 1)) for a in _rope_tables(past + jnp.arange(ts, dtype=I32)))
    cfg_p = dict(qb=256, lb=256, causal_blocks=True, q_base=0, nq_valid=256, l_valid=tp,
                 n_sel=min(TOPK_MAX, tp // 4))
    cfg_s = dict(qb=LANES, lb=256, causal_blocks=False, q_base=past, nq_valid=ts, l_valid=past + ts,
                 n_sel=min(TOPK_MAX, (past + ts) // 4))
    hp, hs = x_prompt, x_sample
    outs_p, outs_s = [], []
    for l in range(depth):
        w = (norm_mix[l], _pack_w_in(w_in[l]), conv_w[l], conv_b[l], conv_ln_g[l], conv_ln_b[l],
             w_conv_out[l].astype(BF16), q_norm[l], k_norm[l], w_attn_out[l].astype(BF16),
             w_out[l].astype(BF16), norm_ffn[l], w_ff1[l].astype(BF16), w_ff2[l].astype(BF16))
        zero_state = jnp.zeros((bp, CONV_W - 1, D_CONV), F32)
        hp, *rest_p = _layer(hp, tabs_p, tp // tm, zero_state, None, w, tm=tm, conv_tm=tm, dsa_cfg=cfg_p)
        hs, *rest_s = _layer(hs, tabs_s, (bs * ts) // tm, state_conv[l],
                             (cache_k[l], cache_v[l], cache_kidx[l]), w, tm=tm, conv_tm=ts, dsa_cfg=cfg_s)
        outs_p.append(rest_p)
        outs_s.append(rest_s)
    stack = lambda outs, i: jnp.stack([o[i] for o in outs])
    return (hp, hs,
            stack(outs_p, 0), stack(outs_p, 1), stack(outs_p, 2), stack(outs_p, 3),
            stack(outs_s, 0), stack(outs_s, 1), stack(outs_s, 2), stack(outs_s, 3))
```

```python
import functools

import jax
import jax.numpy as jnp
import numpy as np
from jax import lax
from jax.experimental import pallas as pl
from jax.experimental.pallas import tpu as pltpu

F32 = jnp.float32
BF16 = jnp.bfloat16
I32 = jnp.int32

D_MODEL = 1024
CHUNK = 64
CHUNK_SHIFT = 6
N_HEADS = 8
HEAD_DIM = 64
ROT_DIM = HEAD_DIM // 4
ROT_HALF = ROT_DIM // 2
ROPE_THETA = 500000.0
N_IDX_HEADS = 8
IDX_DIM = 64
TOPK_MAX = 256
D_CONV = 512
CONV_W = 31
D_FF = 4 * D_MODEL
EPS = 1e-6
LN_EPS = 1e-5
ATTN_SCALE = HEAD_DIM ** -0.5
IDX_SCALE = (N_IDX_HEADS ** -0.5) * (IDX_DIM ** -0.5)
D_ATT = N_HEADS * HEAD_DIM

LANES = 128
SUBLANES = 8
VMEM_LIMIT_BYTES = 56 * 1024 * 1024

COL_KIW = 6 * 512
COL_GC = COL_KIW + LANES
COL_GA = COL_GC + D_MODEL
D_IN_PACKED = COL_GA + D_MODEL
D_IN_HEAD = 2 * D_CONV + 3 * D_ATT + N_IDX_HEADS * IDX_DIM + IDX_DIM + N_IDX_HEADS

HALO = 32
INF = float("inf")
F32_MAX = float(np.finfo(np.float32).max)
V_SLAB = HEAD_DIM + 16
M_FLOOR = -1e30
MASK_BIAS = -2e30
SEARCH_FAST_ITERS = 48
SEARCH_MAX_ITERS = 4096


def _cparams(sem):
    return pltpu.CompilerParams(dimension_semantics=sem, vmem_limit_bytes=VMEM_LIMIT_BYTES)


def _resident(block_shape, index_map):
    return pl.BlockSpec(block_shape, index_map, pipeline_mode=pl.Buffered(1))


def _rope(x, c, sa, sb):
    return x * c + pltpu.roll(x, LANES - ROT_HALF, 1) * sa + pltpu.roll(x, ROT_HALF, 1) * sb


def _head_rms(x, gain):
    lane = lax.broadcasted_iota(I32, x.shape, 1)
    lo = lane < HEAD_DIM
    x2 = x * x
    s_lo = jnp.sum(jnp.where(lo, x2, 0.0), axis=-1, keepdims=True)
    s_hi = jnp.sum(jnp.where(lo, 0.0, x2), axis=-1, keepdims=True)
    ms = jnp.where(lo, s_lo, s_hi) * (1.0 / HEAD_DIM)
    return x * lax.rsqrt(ms + EPS) * gain


def _inproj_body(x_ref, g_ref, w_ref, cos_ref, sa_ref, sb_ref, qg_ref, kg_ref,
                 glu_ref, q_ref, k_ref, kb_ref, v_ref, vb_ref, qi_ref, kiw_ref, gc_ref, ga_ref):
    x = x_ref[...]
    ms = jnp.mean(x * x, axis=-1, keepdims=True)
    h = (x * lax.rsqrt(ms + EPS) * g_ref[...]).astype(BF16)

    def proj(c0, n):
        return jnp.dot(h, w_ref[:, c0:c0 + n], preferred_element_type=F32)

    glu_ref[...] = proj(0, D_CONV) * jax.nn.sigmoid(proj(D_CONV, D_CONV))

    c, sa, sb = cos_ref[...], sa_ref[...], sb_ref[...]
    zq = proj(2 * D_CONV, D_ATT)
    zk = proj(2 * D_CONV + D_ATT, D_ATT)
    zqi = proj(2 * D_CONV + 3 * D_ATT, N_IDX_HEADS * IDX_DIM)
    for g in range(D_ATT // LANES):
        sl = slice(g * LANES, (g + 1) * LANES)
        qh = _rope(_head_rms(zq[:, sl], qg_ref[...]), c, sa, sb)
        q_ref[:, sl] = (qh * ATTN_SCALE).astype(BF16)
        kh = _rope(_head_rms(zk[:, sl], kg_ref[...]), c, sa, sb)
        k_ref[:, sl] = kh
        kb_ref[:, sl] = kh.astype(BF16)
        qi_ref[:, sl] = _rope(zqi[:, sl], c, sa, sb).astype(BF16)

    zv = proj(2 * D_CONV + 2 * D_ATT, D_ATT)
    v_ref[...] = zv
    vb_ref[...] = zv.astype(BF16)

    zkiw = proj(COL_KIW, LANES)
    lane = lax.broadcasted_iota(I32, zkiw.shape, 1)
    is_ki = lane < IDX_DIM
    roped = _rope(zkiw, jnp.where(is_ki, c, 1.0), jnp.where(is_ki, sa, 0.0), jnp.where(is_ki, sb, 0.0))
    kiw_ref[...] = jnp.where(is_ki, roped, zkiw * IDX_SCALE)

    gc_ref[...] = jax.nn.sigmoid(proj(COL_GC, D_MODEL))
    ga_ref[...] = jax.nn.sigmoid(proj(COL_GA, D_MODEL))


def _inproj(x, g, w, cos, sa, sb, qg, kg, tm, n_tab_blocks):
    n = x.shape[0]
    row = lambda width: pl.BlockSpec((tm, width), lambda i: (i, 0))
    tab = pl.BlockSpec((tm, LANES), lambda i: (i % n_tab_blocks, 0))
    vec = lambda width: pl.BlockSpec((1, width), lambda i: (0, 0))
    out_shapes = [
        jax.ShapeDtypeStruct((n, D_CONV), F32),
        jax.ShapeDtypeStruct((n, D_ATT), BF16),
        jax.ShapeDtypeStruct((n, D_ATT), F32),
        jax.ShapeDtypeStruct((n, D_ATT), BF16),
        jax.ShapeDtypeStruct((n, D_ATT), F32),
        jax.ShapeDtypeStruct((n, D_ATT), BF16),
        jax.ShapeDtypeStruct((n, D_ATT), BF16),
        jax.ShapeDtypeStruct((n, LANES), F32),
        jax.ShapeDtypeStruct((n, D_MODEL), F32),
        jax.ShapeDtypeStruct((n, D_MODEL), F32),
    ]
    out_specs = [row(s.shape[1]) for s in out_shapes]
    return pl.pallas_call(
        _inproj_body,
        grid=(n // tm,),
        in_specs=[row(D_MODEL), vec(D_MODEL), _resident((D_MODEL, D_IN_PACKED), lambda i: (0, 0)),
                  tab, tab, tab, vec(LANES), vec(LANES)],
        out_specs=out_specs,
        out_shape=out_shapes,
        compiler_params=_cparams(("parallel",)),
        name="inproj",
    )(x, g, w, cos, sa, sb, qg, kg)


def _conv_body(cur_ref, halo_ref, st_ref, cw_ref, cb_ref, lg_ref, lb_ref, wco_ref, gc_ref,
               out_ref, win_ref, act_ref, *, tm, rows):
    i = pl.program_id(1)
    win_ref[0:HALO, :] = jnp.where(i == 0, st_ref[0], halo_ref[0])
    win_ref[HALO:HALO + tm, :] = cur_ref[0]
    off = HALO - (CONV_W - 1)
    for r in range(tm // rows):
        acc = jnp.zeros((rows, D_CONV), F32) + cb_ref[...]
        for j in range(CONV_W):
            acc = acc + win_ref[r * rows + off + j:r * rows + off + j + rows, :] * cw_ref[j:j + 1, :]
        mu = jnp.mean(acc, axis=-1, keepdims=True)
        d = acc - mu
        var = jnp.mean(d * d, axis=-1, keepdims=True)
        y = d * lax.rsqrt(var + LN_EPS) * lg_ref[...] + lb_ref[...]
        act_ref[r * rows:(r + 1) * rows, :] = (y * jax.nn.sigmoid(y)).astype(BF16)
    out_ref[0] = gc_ref[0] * jnp.dot(act_ref[...], wco_ref[...], preferred_element_type=F32)


def _conv_branch(glu, halo_src, state, cw, cb, lg, lb, wco, gc, tm):
    b, t, _ = glu.shape
    rows = min(tm, 32)
    hb = tm // HALO
    vec = pl.BlockSpec((1, D_CONV), lambda bi, i: (0, 0))
    return pl.pallas_call(
        functools.partial(_conv_body, tm=tm, rows=rows),
        grid=(b, t // tm),
        in_specs=[
            pl.BlockSpec((1, tm, D_CONV), lambda bi, i: (bi, i, 0)),
            pl.BlockSpec((1, HALO, D_CONV), lambda bi, i: (bi, jnp.maximum(i * hb - 1, 0), 0)),
            pl.BlockSpec((1, HALO, D_CONV), lambda bi, i: (bi, 0, 0)),
            pl.BlockSpec((HALO, D_CONV), lambda bi, i: (0, 0)),
            vec, vec, vec,
            _resident((D_CONV, D_MODEL), lambda bi, i: (0, 0)),
            pl.BlockSpec((1, tm, D_MODEL), lambda bi, i: (bi, i, 0)),
        ],
        out_specs=pl.BlockSpec((1, tm, D_MODEL), lambda bi, i: (bi, i, 0)),
        out_shape=jax.ShapeDtypeStruct((b, t, D_MODEL), F32),
        scratch_shapes=[pltpu.VMEM((HALO + tm, D_CONV), F32), pltpu.VMEM((tm, D_CONV), BF16)],
        compiler_params=_cparams(("parallel", "arbitrary")),
        name="conv_branch",
    )(glu, halo_src, state, cw, cb, lg, lb, wco, gc)


def _fold(x, op):
    return op(x.reshape(x.shape[0] // SUBLANES, SUBLANES, x.shape[1]), axis=0)


def _fin(x8, op):
    return op(x8, axis=0, keepdims=True)


def _any_set(flag):
    return jnp.max(jnp.where(flag, 1, 0))


def _select_threshold(sc_ref, n_kb, lb, qb, n_sel, mx, mn, n_adm, lane_ok):
    kf = float(n_sel)
    log_target = float(np.log(n_sel + 0.5))
    full8 = lambda v: jnp.full((SUBLANES, qb), v, F32)
    row = lambda v: jnp.full((1, qb), v, F32)

    def rows(kb):
        return pl.ds(pl.multiple_of(kb * lb, lb), lb)

    def count2(p):
        def blk(kb, c):
            x = sc_ref[rows(kb), :]
            return (c[0] + _fold(jnp.where(x >= p, 1.0, 0.0), jnp.sum),
                    c[1] + _fold(jnp.where(x > p, 1.0, 0.0), jnp.sum))
        ge8, gt8 = lax.fori_loop(0, n_kb, blk, (full8(0.0), full8(0.0)))
        return _fin(ge8, jnp.sum), _fin(gt8, jnp.sum)

    def scan(p, lo, up, snap):
        def blk(kb, c):
            x = sc_ref[rows(kb), :]
            out = [c[0] + _fold(jnp.where(x >= p, 1.0, 0.0), jnp.sum)]
            if snap:
                out.append(jnp.minimum(c[1], _fold(jnp.where(x >= lo, x, INF), jnp.min)))
                out.append(jnp.maximum(c[2], _fold(jnp.where(x < up, x, -INF), jnp.max)))
            return tuple(out)
        init = (full8(0.0), full8(INF), full8(-INF)) if snap else (full8(0.0),)
        res = lax.fori_loop(0, n_kb, blk, init)
        if snap:
            return _fin(res[0], jnp.sum), _fin(res[1], jnp.min), _fin(res[2], jnp.max)
        return _fin(res[0], jnp.sum)

    def pivot(lo, c_lo, up, c_up, wl, wu, bisect):
        xu = jnp.where(up == INF, mx, up)
        gl = (jnp.log(c_lo) - log_target) * wl
        gu = (log_target - jnp.log(jnp.maximum(c_up, 0.5))) * wu
        p = jnp.where(bisect, 0.5 * lo + 0.5 * xu, lo + (xu - lo) * (gl / (gl + gu)))
        p = jnp.minimum(p, xu)
        stuck = jnp.logical_not(p > lo)
        return jnp.where(stuck, xu, p), stuck

    def update(p, c, lo, c_lo, up, c_up, live):
        ge = c >= kf
        to_lo = jnp.logical_and(live, ge)
        to_up = jnp.logical_and(live, jnp.logical_not(ge))
        return (jnp.where(to_lo, p, lo), jnp.where(to_lo, c, c_lo),
                jnp.where(to_up, p, up), jnp.where(to_up, c, c_up), to_lo, to_up)

    def count_pass(st):
        it, _, _, lo, c_lo, up, c_up, wl, wu, side, done = st
        live = done == 0
        p, stuck = pivot(lo, c_lo, up, c_up, wl, wu, it % 8 == 7)
        c = scan(p, lo, up, False)
        lo, c_lo, up, c_up, to_lo, to_up = update(p, c, lo, c_lo, up, c_up, live)
        wl = jnp.where(jnp.logical_and(to_up, side < 0.0), wl * 0.5, jnp.where(to_lo, 1.0, wl))
        wu = jnp.where(jnp.logical_and(to_lo, side > 0.0), wu * 0.5, jnp.where(to_up, 1.0, wu))
        side = jnp.where(to_lo, 1.0, jnp.where(to_up, -1.0, side))
        done = jnp.where(c_lo == kf, 1, done)
        return (it + 1, _any_set(done == 0), _any_set(jnp.logical_and(stuck, live)),
                lo, c_lo, up, c_up, wl, wu, side, done)

    def snap_pass(st):
        it, _, lo, c_lo, up, c_up, done, tie = st
        live = done == 0
        p, _ = pivot(lo, c_lo, up, c_up, row(1.0), row(1.0), it % 2 == 1)
        c, a, b = scan(p, lo, up, True)
        tied = jnp.logical_and(live, a == b)
        lo2, c_lo2, up2, c_up2, _, _ = update(p, c, lo, c_lo, up, c_up,
                                              jnp.logical_and(live, jnp.logical_not(tied)))
        lo2 = jnp.where(live, jnp.maximum(lo2, a), lo2)
        tie = jnp.where(tied, 1, tie)
        done = jnp.where(jnp.logical_or(tied, c_lo2 == kf), 1, done)
        return (it + 1, _any_set(done == 0), lo2, c_lo2, up2, c_up2, done, tie)

    few = n_adm <= kf
    live0 = jnp.logical_and(jnp.logical_not(few), lane_ok)
    ge0, gt0 = count2(row(0.0))
    tie0 = jnp.logical_and(live0, jnp.logical_and(gt0 < kf, ge0 >= kf))
    above = ge0 >= kf
    lo0 = jnp.where(few, -INF, jnp.where(tie0, 0.0, jnp.where(above, jnp.maximum(mn, 0.0), mn)))
    c_lo0 = jnp.where(jnp.logical_and(above, mn < 0.0), ge0, n_adm)
    up0 = jnp.where(above, INF, 0.0)
    c_up0 = jnp.where(above, 0.0, ge0)
    done0 = jnp.where(jnp.logical_and(live0, jnp.logical_not(tie0)), 0, 1)
    done0 = jnp.where(c_lo0 == kf, 1, done0)
    tie_init = jnp.where(tie0, 1, 0)
    st = (jnp.int32(0), _any_set(done0 == 0), jnp.int32(0), lo0, c_lo0, up0, c_up0,
          row(1.0), row(1.0), row(0.0), done0)
    st = lax.while_loop(lambda s: (s[0] < SEARCH_FAST_ITERS) & (s[1] > 0) & (s[2] == 0), count_pass, st)
    it1, active1, _, lo1, c_lo1, up1, c_up1, _, _, _, done1 = st
    st = (it1, active1, lo1, c_lo1, up1, c_up1, done1, tie_init)
    st = lax.while_loop(lambda s: (s[0] < SEARCH_MAX_ITERS) & (s[1] > 0), snap_pass, st)
    return st[2], st[7]


def _selection_bias(x, thr, thr_valid, need, seen, tri, with_ties):
    if not with_ties:
        return jnp.where(x >= thr_valid, 0.0, MASK_BIAS), seen
    eq = x == thr
    eqf = jnp.where(eq, 1.0, 0.0)
    pref = jnp.dot(tri, eqf.astype(BF16), preferred_element_type=F32) + seen
    take = jnp.where(eq, jnp.where(pref <= need, 1.0, 0.0), jnp.where(x > thr, 1.0, 0.0))
    bias = jnp.where(jnp.where(x > -INF, take, 0.0) > 0.5, 0.0, MASK_BIAS)
    return bias, seen + jnp.sum(eqf, axis=0, keepdims=True)


def _tie_setup(sc_ref, n_kb, lb, qb, n_sel, thr):
    def gt_blk(kb, c):
        x = sc_ref[pl.ds(pl.multiple_of(kb * lb, lb), lb), :]
        return c + _fold(jnp.where(x > thr, 1.0, 0.0), jnp.sum)
    gt = _fin(lax.fori_loop(0, n_kb, gt_blk, jnp.zeros((SUBLANES, qb), F32)), jnp.sum)
    ri = lax.broadcasted_iota(I32, (lb, lb), 0)
    ci = lax.broadcasted_iota(I32, (lb, lb), 1)
    return float(n_sel) - gt, jnp.where(ci <= ri, 1.0, 0.0).astype(BF16)


def _dsa_body(qT_ref, qiT_ref, iwT_ref, k_ref, vT_ref, ki_ref, oT_ref,
              sc_ref, qpad_ref, bias_ref, s_ref, acc_ref, m_ref, bm_ref,
              *, qb, lb, sub, causal_blocks, q_base, nq_valid, l_valid, n_sel):
    j = pl.program_id(1)
    n_kb = (j + 1) if causal_blocks else (sc_ref.shape[0] // lb)
    q0 = q_base + j * qb
    lane = lax.broadcasted_iota(I32, (1, qb), 1)
    qchunk = (q0 + lane) >> CHUNK_SHIFT
    fold, fin = _fold, _fin

    def rows(kb):
        return pl.ds(pl.multiple_of(kb * lb, lb), lb)

    def idx_block(kb, carry):
        mx8, mn8, n8 = carry
        r0 = pl.multiple_of(kb * lb, lb)
        for s in range(lb // sub):
            kis = ki_ref[0, pl.ds(r0 + s * sub, sub), :]
            score = jnp.zeros((sub, qb), F32)
            for h in range(N_IDX_HEADS):
                sh = jnp.dot(kis, qiT_ref[0, h * IDX_DIM:(h + 1) * IDX_DIM, :], preferred_element_type=F32)
                score = score + jnp.maximum(sh, 0.0) * iwT_ref[0, h:h + 1, :]
            kpos = r0 + s * sub + lax.broadcasted_iota(I32, (sub, qb), 0)
            adm = jnp.where(kpos < l_valid, kpos >> CHUNK_SHIFT, qchunk + 1) <= qchunk
            sc_ref[pl.ds(r0 + s * sub, sub), :] = jnp.where(adm, score, -INF)
            mx8 = jnp.maximum(mx8, fold(jnp.where(adm, score, -INF), jnp.max))
            mn8 = jnp.minimum(mn8, fold(jnp.where(adm, score, INF), jnp.min))
            n8 = n8 + fold(jnp.where(adm, 1.0, 0.0), jnp.sum)
        return mx8, mn8, n8

    full8 = lambda v: jnp.full((SUBLANES, qb), v, F32)
    mx8, mn8, n8 = lax.fori_loop(0, n_kb, idx_block, (full8(-INF), full8(INF), full8(0.0)))
    mx, mn, n_adm = fin(mx8, jnp.max), fin(mn8, jnp.min), fin(n8, jnp.sum)

    thr, tie = _select_threshold(sc_ref, n_kb, lb, qb, n_sel, mx, mn, n_adm, lane < nq_valid)
    has_ties = _any_set(jnp.logical_and(tie == 1, lane < nq_valid)) > 0
    thr_valid = jnp.maximum(thr, -F32_MAX)

    qpad_ref[...] = jnp.zeros(qpad_ref.shape, BF16)
    for h in range(N_HEADS):
        r = (h % 2) * HEAD_DIM
        qpad_ref[h, r:r + HEAD_DIM, :] = qT_ref[0, h * HEAD_DIM:(h + 1) * HEAD_DIM, :]

    def attention(with_ties):
        m_ref[...] = jnp.full(m_ref.shape, M_FLOOR, F32)
        acc_ref[...] = jnp.zeros(acc_ref.shape, F32)
        need, tri = _tie_setup(sc_ref, n_kb, lb, qb, n_sel, thr) if with_ties else (None, None)

        def att_block(kb, seen):
            bias_ref[...], seen = _selection_bias(sc_ref[rows(kb), :], thr, thr_valid, need, seen, tri, with_ties)
            for h in range(N_HEADS):
                p2 = (h // 2) * 2 * HEAD_DIM
                s = jnp.dot(k_ref[0, rows(kb), p2:p2 + 2 * HEAD_DIM], qpad_ref[h],
                            preferred_element_type=F32) + bias_ref[...]
                s_ref[h] = s
                bm_ref[h:h + 1, :] = jnp.max(s, axis=0, keepdims=True)
            m_old = m_ref[...]
            m_new = jnp.maximum(m_old, bm_ref[...])
            alpha = jnp.exp(m_old - m_new)
            m_ref[...] = m_new
            for h in range(N_HEADS):
                p = jnp.exp(s_ref[h] - m_new[h:h + 1, :]).astype(BF16)
                vs = slice(h * V_SLAB, (h + 1) * V_SLAB)
                pv = jnp.dot(vT_ref[0, vs, rows(kb)], p, preferred_element_type=F32)
                acc_ref[vs, :] = acc_ref[vs, :] * alpha[h:h + 1, :] + pv
            return seen

        lax.fori_loop(0, n_kb, att_block, jnp.zeros((1, qb), F32))
        for h in range(N_HEADS):
            num = acc_ref[h * V_SLAB:h * V_SLAB + HEAD_DIM, :]
            den = acc_ref[h * V_SLAB + HEAD_DIM:h * V_SLAB + HEAD_DIM + 1, :]
            oT_ref[0, h * HEAD_DIM:(h + 1) * HEAD_DIM, :] = num / den

    @pl.when(has_ties)
    def _():
        attention(True)

    @pl.when(jnp.logical_not(has_ties))
    def _():
        attention(False)


def _dsa(qT, qiT, iwT, k, vT, ki, *, qb, lb, causal_blocks, q_base, nq_valid, l_valid, n_sel):
    b, _, tq = qT.shape
    lp = k.shape[1]
    body = functools.partial(_dsa_body, qb=qb, lb=lb, sub=64, causal_blocks=causal_blocks,
                             q_base=q_base, nq_valid=nq_valid, l_valid=l_valid, n_sel=n_sel)
    return pl.pallas_call(
        body,
        grid=(b, tq // qb),
        in_specs=[
            pl.BlockSpec((1, D_ATT, qb), lambda bi, j: (bi, 0, j)),
            pl.BlockSpec((1, N_IDX_HEADS * IDX_DIM, qb), lambda bi, j: (bi, 0, j)),
            pl.BlockSpec((1, N_IDX_HEADS, qb), lambda bi, j: (bi, 0, j)),
            _resident((1, lp, D_ATT), lambda bi, j: (bi, 0, 0)),
            _resident((1, N_HEADS * V_SLAB, lp), lambda bi, j: (bi, 0, 0)),
            _resident((1, lp, IDX_DIM), lambda bi, j: (bi, 0, 0)),
        ],
        out_specs=pl.BlockSpec((1, D_ATT, qb), lambda bi, j: (bi, 0, j)),
        out_shape=jax.ShapeDtypeStruct((b, D_ATT, tq), F32),
        scratch_shapes=[
            pltpu.VMEM((lp, qb), F32),
            pltpu.VMEM((N_HEADS, 2 * HEAD_DIM, qb), BF16),
            pltpu.VMEM((lb, qb), F32),
            pltpu.VMEM((N_HEADS, lb, qb), F32),
            pltpu.VMEM((N_HEADS * V_SLAB, qb), F32),
            pltpu.VMEM((N_HEADS, qb), F32),
            pltpu.VMEM((N_HEADS, qb), F32),
        ],
        compiler_params=_cparams(("parallel", "arbitrary")),
        name="dsa",
    )(qT, qiT, iwT, k, vT, ki)


def _values_with_ones(vb3):
    b, l, _ = vb3.shape
    vt = jnp.transpose(vb3.reshape(b, l, N_HEADS, HEAD_DIM), (0, 2, 3, 1))
    ones = jnp.ones((b, N_HEADS, V_SLAB - HEAD_DIM, l), vb3.dtype)
    return jnp.concatenate([vt, ones], axis=2).reshape(b, N_HEADS * V_SLAB, l)


def _dsa_sample_body(wq_ref, wqi_ref, iw_ref, kc_ref, vc_ref, kic_ref, kn_ref, vn_ref, kin_ref, o_ref,
                     sc_ref, s_ref, kx_ref, vx_ref, kix_ref, acc_ref, den_ref,
                     *, lb, n_cache_kb, past, t_new, n_sel):
    qb = LANES
    n_kb = n_cache_kb + 1
    l_valid = past + t_new
    lane = lax.broadcasted_iota(I32, (1, qb), 1)
    qchunk = (past + (lane & (t_new - 1))) >> CHUNK_SHIFT
    full8 = lambda v: jnp.full((SUBLANES, qb), v, F32)

    def rows(kb):
        return pl.ds(kb * lb, lb) if isinstance(kb, int) else pl.ds(pl.multiple_of(kb * lb, lb), lb)

    def over_blocks(fn, carry, cache_ref, new_ref):
        carry = lax.fori_loop(0, n_cache_kb, lambda kb, c: fn(kb, c, cache_ref[0, rows(kb), :].astype(BF16)), carry)
        return fn(n_cache_kb, carry, new_ref[...])

    for new_ref, stage_ref in ((kn_ref, kx_ref), (vn_ref, vx_ref), (kin_ref, kix_ref)):
        stage_ref[...] = jnp.zeros(stage_ref.shape, BF16)
        stage_ref[0:t_new, :] = new_ref[0].astype(BF16)

    def idx_block(kb, carry, kis):
        mx8, mn8, n8 = carry
        r = jnp.maximum(jnp.dot(kis, wqi_ref[0], preferred_element_type=F32), 0.0) * iw_ref[0]
        for shift in (t_new, 2 * t_new, 4 * t_new):
            r = r + pltpu.roll(r, shift, 1)
        kpos = kb * lb + lax.broadcasted_iota(I32, (lb, qb), 0)
        adm = jnp.where(kpos < l_valid, kpos >> CHUNK_SHIFT, qchunk + 1) <= qchunk
        sc_ref[rows(kb), :] = jnp.where(adm, r, -INF)
        return (jnp.maximum(mx8, _fold(jnp.where(adm, r, -INF), jnp.max)),
                jnp.minimum(mn8, _fold(jnp.where(adm, r, INF), jnp.min)),
                n8 + _fold(jnp.where(adm, 1.0, 0.0), jnp.sum))

    mx8, mn8, n8 = over_blocks(idx_block, (full8(-INF), full8(INF), full8(0.0)), kic_ref, kix_ref)
    mx, mn, n_adm = _fin(mx8, jnp.max), _fin(mn8, jnp.min), _fin(n8, jnp.sum)
    thr, tie = _select_threshold(sc_ref, n_kb, lb, qb, n_sel, mx, mn, n_adm, lane >= 0)
    has_ties = _any_set(tie == 1) > 0
    thr_valid = jnp.maximum(thr, -F32_MAX)

    def attention(with_ties):
        need, tri = _tie_setup(sc_ref, n_kb, lb, qb, n_sel, thr) if with_ties else (None, None)

        def score_block(kb, carry, kblk):
            m8, seen = carry
            bias, seen = _selection_bias(sc_ref[rows(kb), :], thr, thr_valid, need, seen, tri, with_ties)
            s = jnp.dot(kblk, wq_ref[0], preferred_element_type=F32) + bias
            s_ref[rows(kb), :] = s
            return jnp.maximum(m8, _fold(s, jnp.max)), seen

        m8, _ = over_blocks(score_block, (full8(M_FLOOR), jnp.zeros((1, qb), F32)), kc_ref, kx_ref)
        m = _fin(m8, jnp.max)
        acc_ref[...] = jnp.zeros(acc_ref.shape, F32)
        den_ref[...] = jnp.zeros(den_ref.shape, F32)
        ones = jnp.ones((lb, LANES), BF16)

        def value_block(kb, carry, vblk):
            pT = jnp.exp(s_ref[rows(kb), :] - m).T.astype(BF16)
            acc_ref[...] += jnp.dot(pT, vblk, preferred_element_type=F32)
            den_ref[...] += jnp.dot(pT, ones, preferred_element_type=F32)
            return carry

        over_blocks(value_block, 0, vc_ref, vx_ref)
        for h in range(N_HEADS):
            rs = slice(h * t_new, (h + 1) * t_new)
            cs = slice(h * HEAD_DIM, (h + 1) * HEAD_DIM)
            o_ref[0, :, cs] = acc_ref[rs, cs] / den_ref[rs, 0:1]

    @pl.when(has_ties)
    def _():
        attention(True)

    @pl.when(jnp.logical_not(has_ties))
    def _():
        attention(False)


def _dsa_sample(q, qi, iw, k_cache, v_cache, ki_cache, k_new, v_new, ki_new, *, lb, n_sel):
    b, t, _ = q.shape
    past = k_cache.shape[1]
    assert N_HEADS * t == LANES and past % lb == 0 and t <= lb
    n_cache_kb = past // lb
    lp = past + lb
    eye = jnp.eye(N_HEADS, dtype=q.dtype)
    qh = jnp.transpose(q.reshape(b, t, N_HEADS, HEAD_DIM), (0, 2, 3, 1))
    wq = jnp.einsum("bhdq,gh->bgdhq", qh, eye).reshape(b, D_ATT, LANES)
    wqi = jnp.transpose(qi.reshape(b, t, N_IDX_HEADS, IDX_DIM), (0, 3, 2, 1)).reshape(b, IDX_DIM, LANES)
    iw_row = jnp.transpose(iw, (0, 2, 1)).reshape(b, 1, LANES)
    per_b = lambda shape: pl.BlockSpec((1,) + shape, lambda bi: (bi, 0, 0))
    body = functools.partial(_dsa_sample_body, lb=lb, n_cache_kb=n_cache_kb, past=past, t_new=t, n_sel=n_sel)
    return pl.pallas_call(
        body,
        grid=(b,),
        in_specs=[per_b((D_ATT, LANES)), per_b((IDX_DIM, LANES)), per_b((1, LANES)),
                  per_b((past, D_ATT)), per_b((past, D_ATT)), per_b((past, IDX_DIM)),
                  per_b((t, D_ATT)), per_b((t, D_ATT)), per_b((t, IDX_DIM))],
        out_specs=per_b((t, D_ATT)),
        out_shape=jax.ShapeDtypeStruct((b, t, D_ATT), F32),
        scratch_shapes=[
            pltpu.VMEM((lp, LANES), F32),
            pltpu.VMEM((lp, LANES), F32),
            pltpu.VMEM((lb, D_ATT), BF16),
            pltpu.VMEM((lb, D_ATT), BF16),
            pltpu.VMEM((lb, IDX_DIM), BF16),
            pltpu.VMEM((LANES, D_ATT), F32),
            pltpu.VMEM((LANES, LANES), F32),
        ],
        compiler_params=_cparams(("parallel",)),
        name="dsa_sample",
    )(wq, wqi, iw_row, k_cache, v_cache, ki_cache, k_new, v_new, ki_new)


def _tail_body(x_ref, o_ref, mc_ref, ga_ref, wao_ref, wout_ref, g_ref, w1_ref, w2_ref, y_ref, *, ff_chunk):
    attn = jnp.dot(o_ref[...].astype(BF16), wao_ref[...], preferred_element_type=F32)
    merged = mc_ref[...] + ga_ref[...] * attn
    x1 = x_ref[...] + jnp.dot(merged.astype(BF16), wout_ref[...], preferred_element_type=F32)
    ms = jnp.mean(x1 * x1, axis=-1, keepdims=True)
    h = (x1 * lax.rsqrt(ms + EPS) * g_ref[...]).astype(BF16)
    y = x1
    for c in range(D_FF // ff_chunk):
        cs = slice(c * ff_chunk, (c + 1) * ff_chunk)
        u = jnp.maximum(jnp.dot(h, w1_ref[:, cs], preferred_element_type=F32), 0.0)
        y = y + jnp.dot((u * u).astype(BF16), w2_ref[cs, :], preferred_element_type=F32)
    y_ref[...] = y


def _tail(x, o, mc, ga, wao, wout, g, w1, w2, tm):
    n = x.shape[0]
    row = lambda width: pl.BlockSpec((tm, width), lambda i: (i, 0))
    full = lambda a: _resident(a.shape, lambda i: (0, 0))
    return pl.pallas_call(
        functools.partial(_tail_body, ff_chunk=1024),
        grid=(n // tm,),
        in_specs=[row(D_MODEL), row(D_ATT), row(D_MODEL), row(D_MODEL),
                  full(wao), full(wout), pl.BlockSpec((1, D_MODEL), lambda i: (0, 0)), full(w1), full(w2)],
        out_specs=row(D_MODEL),
        out_shape=jax.ShapeDtypeStruct((n, D_MODEL), F32),
        compiler_params=_cparams(("parallel",)),
        name="tail",
    )(x, o, mc, ga, wao, wout, g, w1, w2)


def _rope_tables(pos):
    inv = ROPE_THETA ** (-jnp.arange(ROT_HALF, dtype=F32) / ROT_HALF)
    ang = pos.astype(F32)[:, None] * inv[None, :]
    cos, sin = jnp.cos(ang), jnp.sin(ang)
    r = np.arange(LANES) % HEAD_DIM
    jj = r % ROT_HALF
    first = jnp.asarray(r < ROT_HALF)[None, :]
    second = jnp.asarray((r >= ROT_HALF) & (r < ROT_DIM))[None, :]
    c = jnp.where(first | second, cos[:, jj], 1.0)
    sa = jnp.where(first, -sin[:, jj], 0.0)
    sb = jnp.where(second, sin[:, jj], 0.0)
    return c, sa, sb


def _pack_w_in(w):
    pad = jnp.zeros((D_MODEL, LANES - (D_IN_HEAD - COL_KIW)), w.dtype)
    return jnp.concatenate([w[:, :D_IN_HEAD], pad, w[:, D_IN_HEAD:]], axis=1).astype(BF16)


def _tile2(v):
    return jnp.concatenate([v, v])[None, :].astype(F32)


def _layer(x, tabs, n_tab_blocks, conv_state, caches, w, *, tm, conv_tm, dsa_cfg):
    b, t, _ = x.shape
    n = b * t
    (norm_mix, w_in_p, conv_w, conv_b, ln_g, ln_b, w_conv_out, q_norm, k_norm,
     w_attn_out, w_out, norm_ffn, w_ff1, w_ff2) = w
    xf = x.reshape(n, D_MODEL)
    glu, q, k, kb, v, vb, qi, kiw, gc, ga = _inproj(
        xf, norm_mix[None, :], w_in_p, *tabs, _tile2(q_norm), _tile2(k_norm), tm, n_tab_blocks)

    glu3 = glu.reshape(b, t, D_CONV)
    state_p = jnp.pad(conv_state, ((0, 0), (HALO - (CONV_W - 1), 0), (0, 0)))
    halo_src = glu3 if t >= HALO else state_p
    cw_p = jnp.pad(conv_w, ((0, HALO - CONV_W), (0, 0)))
    mc = _conv_branch(glu3, halo_src, state_p, cw_p, conv_b[None, :], ln_g[None, :], ln_b[None, :],
                      w_conv_out, gc.reshape(b, t, D_MODEL), conv_tm)
    new_conv = jnp.concatenate([conv_state, glu3], axis=1)[:, -(CONV_W - 1):]

    ki = kiw[:, :IDX_DIM]
    iw = kiw[:, IDX_DIM:IDX_DIM + N_IDX_HEADS].reshape(b, t, N_IDX_HEADS)
    if caches is None:
        qT = jnp.swapaxes(q.reshape(b, t, D_ATT), 1, 2)
        qiT = jnp.swapaxes(qi.reshape(b, t, D_ATT), 1, 2)
        oT = _dsa(qT, qiT, jnp.swapaxes(iw, 1, 2), kb.reshape(b, t, D_ATT),
                  _values_with_ones(vb.reshape(b, t, D_ATT)), ki.astype(BF16).reshape(b, t, IDX_DIM), **dsa_cfg)
        o = jnp.swapaxes(oT, 1, 2).reshape(n, D_ATT)
    else:
        k_prev, v_prev, ki_prev = caches
        p = k_prev.shape[1]
        o = _dsa_sample(q.reshape(b, t, D_ATT), qi.reshape(b, t, D_ATT), iw,
                        k_prev.reshape(b, p, D_ATT), v_prev.reshape(b, p, D_ATT), ki_prev,
                        k.reshape(b, t, D_ATT), v.reshape(b, t, D_ATT), ki.reshape(b, t, IDX_DIM),
                        **dsa_cfg).reshape(n, D_ATT)

    y = _tail(xf, o, mc.reshape(n, D_MODEL), ga, w_attn_out, w_out, norm_ffn[None, :], w_ff1, w_ff2, tm)
    return (y.reshape(b, t, D_MODEL), k.reshape(b, t, N_HEADS, HEAD_DIM), v.reshape(b, t, N_HEADS, HEAD_DIM),
            ki.reshape(b, t, IDX_DIM), new_conv)


def kernel(x_prompt, x_sample, cache_k, cache_v, cache_kidx, state_conv, norm_mix, w_in, conv_w, conv_b,
           conv_ln_g, conv_ln_b, w_conv_out, q_norm, k_norm, w_attn_out, w_out, norm_ffn, w_ff1, w_ff2):
    bp, tp, _ = x_prompt.shape
    bs, ts, _ = x_sample.shape
    depth = norm_mix.shape[0]
    past = cache_k.shape[2]
    tm = 256
    tabs_p = _rope_tables(jnp.arange(tp, dtype=I32))
    tabs_s = tuple(jnp.tile(a, (bs, 1)) for a in _rope_tables(past + jnp.arange(ts, dtype=I32)))
    cfg_p = dict(qb=256, lb=256, causal_blocks=True, q_base=0, nq_valid=256, l_valid=tp,
                 n_sel=min(TOPK_MAX, tp // 4))
    cfg_s = dict(lb=256, n_sel=min(TOPK_MAX, (past + ts) // 4))
    hp, hs = x_prompt, x_sample
    outs_p, outs_s = [], []
    for l in range(depth):
        w = (norm_mix[l], _pack_w_in(w_in[l]), conv_w[l], conv_b[l], conv_ln_g[l], conv_ln_b[l],
             w_conv_out[l].astype(BF16), q_norm[l], k_norm[l], w_attn_out[l].astype(BF16),
             w_out[l].astype(BF16), norm_ffn[l], w_ff1[l].astype(BF16), w_ff2[l].astype(BF16))
        zero_state = jnp.zeros((bp, CONV_W - 1, D_CONV), F32)
        hp, *rest_p = _layer(hp, tabs_p, tp // tm, zero_state, None, w, tm=tm, conv_tm=tm, dsa_cfg=cfg_p)
        hs, *rest_s = _layer(hs, tabs_s, (bs * ts) // tm, state_conv[l],
                             (cache_k[l], cache_v[l], cache_kidx[l]), w, tm=tm, conv_tm=ts, dsa_cfg=cfg_s)
        outs_p.append(rest_p)
        outs_s.append(rest_s)
    stack = lambda outs, i: jnp.stack([o[i] for o in outs])
    return (hp, hs,
            stack(outs_p, 0), stack(outs_p, 1), stack(outs_p, 2), stack(outs_p, 3),
            stack(outs_s, 0), stack(outs_s, 1), stack(outs_s, 2), stack(outs_s, 3))
```

```python
import functools

import jax
import jax.numpy as jnp
import numpy as np
from jax import lax
from jax.experimental import pallas as pl
from jax.experimental.pallas import tpu as pltpu

F32 = jnp.float32
BF16 = jnp.bfloat16
I32 = jnp.int32

D_MODEL = 1024
CHUNK = 64
CHUNK_SHIFT = 6
N_HEADS = 8
HEAD_DIM = 64
ROT_DIM = HEAD_DIM // 4
ROT_HALF = ROT_DIM // 2
ROPE_THETA = 500000.0
N_IDX_HEADS = 8
IDX_DIM = 64
TOPK_MAX = 256
D_CONV = 512
CONV_W = 31
D_FF = 4 * D_MODEL
EPS = 1e-6
LN_EPS = 1e-5
ATTN_SCALE = HEAD_DIM ** -0.5
IDX_SCALE = (N_IDX_HEADS ** -0.5) * (IDX_DIM ** -0.5)
D_ATT = N_HEADS * HEAD_DIM

LANES = 128
SUBLANES = 8
ACC_ROWS = 4 * SUBLANES
VMEM_LIMIT_BYTES = 56 * 1024 * 1024

COL_KIW = 6 * 512
COL_GC = COL_KIW + LANES
COL_GA = COL_GC + D_MODEL
D_IN_PACKED = COL_GA + D_MODEL
D_IN_HEAD = 2 * D_CONV + 3 * D_ATT + N_IDX_HEADS * IDX_DIM + IDX_DIM + N_IDX_HEADS

HALO = 32
INF = float("inf")
F32_MAX = float(np.finfo(np.float32).max)
ONES_ROWS = 16
V_SLAB = HEAD_DIM + ONES_ROWS
M_FLOOR = -1e30
MASK_BIAS = -2e30
SEARCH_FAST_ITERS = 48
SEARCH_MAX_ITERS = 4096


def _cparams(sem):
    return pltpu.CompilerParams(dimension_semantics=sem, vmem_limit_bytes=VMEM_LIMIT_BYTES)


def _resident(block_shape, index_map):
    return pl.BlockSpec(block_shape, index_map, pipeline_mode=pl.Buffered(1))


def _rope(x, c, sa, sb):
    return x * c + pltpu.roll(x, LANES - ROT_HALF, 1) * sa + pltpu.roll(x, ROT_HALF, 1) * sb


def _head_rms(x, gain):
    lane = lax.broadcasted_iota(I32, x.shape, 1)
    lo = lane < HEAD_DIM
    x2 = x * x
    s_lo = jnp.sum(jnp.where(lo, x2, 0.0), axis=-1, keepdims=True)
    s_hi = jnp.sum(jnp.where(lo, 0.0, x2), axis=-1, keepdims=True)
    ms = jnp.where(lo, s_lo, s_hi) * (1.0 / HEAD_DIM)
    return x * lax.rsqrt(ms + EPS) * gain


def _inproj_body(x_ref, g_ref, w_ref, cos_ref, sa_ref, sb_ref, qg_ref, kg_ref,
                 glu_ref, q_ref, k_ref, v_ref, qi_ref, kiw_ref, gc_ref, ga_ref, *extra, transposed):
    x = x_ref[...]
    ms = jnp.mean(x * x, axis=-1, keepdims=True)
    h = (x * lax.rsqrt(ms + EPS) * g_ref[...]).astype(BF16)

    def proj(c0, n):
        return jnp.dot(h, w_ref[:, c0:c0 + n], preferred_element_type=F32)

    glu_ref[...] = proj(0, D_CONV) * jax.nn.sigmoid(proj(D_CONV, D_CONV))

    c, sa, sb = cos_ref[...], sa_ref[...], sb_ref[...]
    zq = proj(2 * D_CONV, D_ATT)
    zk = proj(2 * D_CONV + D_ATT, D_ATT)
    zqi = proj(2 * D_CONV + 3 * D_ATT, N_IDX_HEADS * IDX_DIM)
    zv = proj(2 * D_CONV + 2 * D_ATT, D_ATT)
    v_ref[...] = zv
    for g in range(D_ATT // LANES):
        sl = slice(g * LANES, (g + 1) * LANES)
        qh = _rope(_head_rms(zq[:, sl], qg_ref[...]), c, sa, sb) * ATTN_SCALE
        kh = _rope(_head_rms(zk[:, sl], kg_ref[...]), c, sa, sb)
        qih = _rope(zqi[:, sl], c, sa, sb)
        k_ref[:, sl] = kh
        if transposed:
            kb_ref, vt_ref, _ = extra
            q_ref[sl, :] = qh.T.astype(BF16)
            qi_ref[sl, :] = qih.T.astype(BF16)
            kb_ref[:, sl] = kh.astype(BF16)
            vt = zv[:, sl].T.astype(BF16)
            ones = jnp.ones((ONES_ROWS, vt.shape[1]), BF16)
            for half in range(2):
                r0 = (2 * g + half) * V_SLAB
                vt_ref[r0:r0 + HEAD_DIM, :] = vt[half * HEAD_DIM:(half + 1) * HEAD_DIM, :]
                vt_ref[r0 + HEAD_DIM:r0 + V_SLAB, :] = ones
        else:
            q_ref[:, sl] = qh.astype(BF16)
            qi_ref[:, sl] = qih.astype(BF16)

    zkiw = proj(COL_KIW, LANES)
    lane = lax.broadcasted_iota(I32, zkiw.shape, 1)
    is_ki = lane < IDX_DIM
    roped = _rope(zkiw, jnp.where(is_ki, c, 1.0), jnp.where(is_ki, sa, 0.0), jnp.where(is_ki, sb, 0.0))
    kiw = jnp.where(is_ki, roped, zkiw * IDX_SCALE)
    kiw_ref[...] = kiw
    if transposed:
        extra[2][...] = kiw.T[IDX_DIM:IDX_DIM + N_IDX_HEADS, :]

    gc_ref[...] = jax.nn.sigmoid(proj(COL_GC, D_MODEL))
    ga_ref[...] = jax.nn.sigmoid(proj(COL_GA, D_MODEL))


def _inproj(x, g, w, cos, sa, sb, qg, kg, tm, n_tab_blocks, transposed):
    n = x.shape[0]
    row = lambda width: pl.BlockSpec((tm, width), lambda i: (i, 0))
    col = lambda height: pl.BlockSpec((height, tm), lambda i: (0, i))
    tab = pl.BlockSpec((tm, LANES), lambda i: (i % n_tab_blocks, 0))
    vec = lambda width: pl.BlockSpec((1, width), lambda i: (0, 0))
    rows_of = lambda width, dt: (jax.ShapeDtypeStruct((n, width), dt), row(width))
    cols_of = lambda height, dt: (jax.ShapeDtypeStruct((height, n), dt), col(height))
    outs = [
        rows_of(D_CONV, F32),
        cols_of(D_ATT, BF16) if transposed else rows_of(D_ATT, BF16),
        rows_of(D_ATT, F32),
        rows_of(D_ATT, F32),
        cols_of(D_ATT, BF16) if transposed else rows_of(D_ATT, BF16),
        rows_of(LANES, F32),
        rows_of(D_MODEL, F32),
        rows_of(D_MODEL, F32),
    ]
    if transposed:
        outs += [rows_of(D_ATT, BF16), cols_of(N_HEADS * V_SLAB, BF16), cols_of(N_IDX_HEADS, F32)]
    return pl.pallas_call(
        functools.partial(_inproj_body, transposed=transposed),
        grid=(n // tm,),
        in_specs=[row(D_MODEL), vec(D_MODEL), _resident((D_MODEL, D_IN_PACKED), lambda i: (0, 0)),
                  tab, tab, tab, vec(LANES), vec(LANES)],
        out_specs=[o[1] for o in outs],
        out_shape=[o[0] for o in outs],
        compiler_params=_cparams(("parallel",)),
        name="inproj",
    )(x, g, w, cos, sa, sb, qg, kg)


def _conv_body(cur_ref, halo_ref, st_ref, cw_ref, cb_ref, lg_ref, lb_ref, wco_ref, gc_ref,
               out_ref, win_ref, act_ref, *, tm, rows):
    i = pl.program_id(1)
    win_ref[0:HALO, :] = jnp.where(i == 0, st_ref[0], halo_ref[0])
    win_ref[HALO:HALO + tm, :] = cur_ref[0]
    off = HALO - (CONV_W - 1)
    for r in range(tm // rows):
        acc = jnp.zeros((rows, D_CONV), F32) + cb_ref[...]
        for j in range(CONV_W):
            acc = acc + win_ref[r * rows + off + j:r * rows + off + j + rows, :] * cw_ref[j:j + 1, :]
        mu = jnp.mean(acc, axis=-1, keepdims=True)
        d = acc - mu
        var = jnp.mean(d * d, axis=-1, keepdims=True)
        y = d * lax.rsqrt(var + LN_EPS) * lg_ref[...] + lb_ref[...]
        act_ref[r * rows:(r + 1) * rows, :] = (y * jax.nn.sigmoid(y)).astype(BF16)
    out_ref[0] = gc_ref[0] * jnp.dot(act_ref[...], wco_ref[...], preferred_element_type=F32)


def _conv_branch(glu, halo_src, state, cw, cb, lg, lb, wco, gc, tm):
    b, t, _ = glu.shape
    rows = min(tm, 32)
    hb = tm // HALO
    vec = pl.BlockSpec((1, D_CONV), lambda bi, i: (0, 0))
    return pl.pallas_call(
        functools.partial(_conv_body, tm=tm, rows=rows),
        grid=(b, t // tm),
        in_specs=[
            pl.BlockSpec((1, tm, D_CONV), lambda bi, i: (bi, i, 0)),
            pl.BlockSpec((1, HALO, D_CONV), lambda bi, i: (bi, jnp.maximum(i * hb - 1, 0), 0)),
            pl.BlockSpec((1, HALO, D_CONV), lambda bi, i: (bi, 0, 0)),
            pl.BlockSpec((HALO, D_CONV), lambda bi, i: (0, 0)),
            vec, vec, vec,
            _resident((D_CONV, D_MODEL), lambda bi, i: (0, 0)),
            pl.BlockSpec((1, tm, D_MODEL), lambda bi, i: (bi, i, 0)),
        ],
        out_specs=pl.BlockSpec((1, tm, D_MODEL), lambda bi, i: (bi, i, 0)),
        out_shape=jax.ShapeDtypeStruct((b, t, D_MODEL), F32),
        scratch_shapes=[pltpu.VMEM((HALO + tm, D_CONV), F32), pltpu.VMEM((tm, D_CONV), BF16)],
        compiler_params=_cparams(("parallel", "arbitrary")),
        name="conv_branch",
    )(glu, halo_src, state, cw, cb, lg, lb, wco, gc)


def _fold(x, op):
    return op(x.reshape(x.shape[0] // ACC_ROWS, ACC_ROWS, x.shape[1]), axis=0)


def _fin(x8, op):
    return op(x8, axis=0, keepdims=True)


def _any_set(flag):
    return jnp.max(jnp.where(flag, 1, 0))


def _select_threshold(sc_ref, n_units, unit, qb, n_sel, mx, mn, n_adm, lane_ok):
    kf = float(n_sel)
    log_target = float(np.log(n_sel + 0.5))
    full8 = lambda v: jnp.full((ACC_ROWS, qb), v, F32)
    row = lambda v: jnp.full((1, qb), v, F32)

    def rows(u):
        return pl.ds(pl.multiple_of(u * unit, unit), unit)

    def count2(p):
        def blk(u, c):
            x = sc_ref[rows(u), :]
            return (c[0] + _fold(jnp.where(x >= p, 1.0, 0.0), jnp.sum),
                    c[1] + _fold(jnp.where(x > p, 1.0, 0.0), jnp.sum))
        ge8, gt8 = lax.fori_loop(0, n_units, blk, (full8(0.0), full8(0.0)))
        return _fin(ge8, jnp.sum), _fin(gt8, jnp.sum)

    def scan(p, lo, up, snap):
        def blk(u, c):
            x = sc_ref[rows(u), :]
            out = [c[0] + _fold(jnp.where(x >= p, 1.0, 0.0), jnp.sum)]
            if snap:
                out.append(jnp.minimum(c[1], _fold(jnp.where(x >= lo, x, INF), jnp.min)))
                out.append(jnp.maximum(c[2], _fold(jnp.where(x < up, x, -INF), jnp.max)))
            return tuple(out)
        init = (full8(0.0), full8(INF), full8(-INF)) if snap else (full8(0.0),)
        res = lax.fori_loop(0, n_units, blk, init)
        if snap:
            return _fin(res[0], jnp.sum), _fin(res[1], jnp.min), _fin(res[2], jnp.max)
        return _fin(res[0], jnp.sum)

    def pivot(lo, c_lo, up, c_up, wl, wu, bisect):
        xu = jnp.where(up == INF, mx, up)
        gl = (jnp.log(c_lo) - log_target) * wl
        gu = (log_target - jnp.log(jnp.maximum(c_up, 0.5))) * wu
        p = jnp.where(bisect, 0.5 * lo + 0.5 * xu, lo + (xu - lo) * (gl / (gl + gu)))
        p = jnp.minimum(p, xu)
        stuck = jnp.logical_not(p > lo)
        return jnp.where(stuck, xu, p), stuck

    def update(p, c, lo, c_lo, up, c_up, live):
        ge = c >= kf
        to_lo = jnp.logical_and(live, ge)
        to_up = jnp.logical_and(live, jnp.logical_not(ge))
        return (jnp.where(to_lo, p, lo), jnp.where(to_lo, c, c_lo),
                jnp.where(to_up, p, up), jnp.where(to_up, c, c_up), to_lo, to_up)

    def count_pass(st):
        it, _, _, lo, c_lo, up, c_up, wl, wu, side, done = st
        live = done == 0
        p, stuck = pivot(lo, c_lo, up, c_up, wl, wu, it % 8 == 7)
        c = scan(p, lo, up, False)
        lo, c_lo, up, c_up, to_lo, to_up = update(p, c, lo, c_lo, up, c_up, live)
        wl = jnp.where(jnp.logical_and(to_up, side < 0.0), wl * 0.5, jnp.where(to_lo, 1.0, wl))
        wu = jnp.where(jnp.logical_and(to_lo, side > 0.0), wu * 0.5, jnp.where(to_up, 1.0, wu))
        side = jnp.where(to_lo, 1.0, jnp.where(to_up, -1.0, side))
        done = jnp.where(c_lo == kf, 1, done)
        flags = jnp.sum(jnp.where(done == 0, 1, 0) + jnp.where(jnp.logical_and(stuck, live), 1 << 16, 0))
        return (it + 1, flags & 0xFFFF, flags >> 16, lo, c_lo, up, c_up, wl, wu, side, done)

    def snap_pass(st):
        it, _, lo, c_lo, up, c_up, done, tie = st
        live = done == 0
        p, _ = pivot(lo, c_lo, up, c_up, row(1.0), row(1.0), it % 2 == 1)
        c, a, b = scan(p, lo, up, True)
        tied = jnp.logical_and(live, a == b)
        lo2, c_lo2, up2, c_up2, _, _ = update(p, c, lo, c_lo, up, c_up,
                                              jnp.logical_and(live, jnp.logical_not(tied)))
        lo2 = jnp.where(live, jnp.maximum(lo2, a), lo2)
        tie = jnp.where(tied, 1, tie)
        done = jnp.where(jnp.logical_or(tied, c_lo2 == kf), 1, done)
        return (it + 1, _any_set(done == 0), lo2, c_lo2, up2, c_up2, done, tie)

    few = n_adm <= kf
    live0 = jnp.logical_and(jnp.logical_not(few), lane_ok)
    ge0, gt0 = count2(row(0.0))
    tie0 = jnp.logical_and(live0, jnp.logical_and(gt0 < kf, ge0 >= kf))
    above = ge0 >= kf
    lo0 = jnp.where(few, -INF, jnp.where(tie0, 0.0, jnp.where(above, jnp.maximum(mn, 0.0), mn)))
    c_lo0 = jnp.where(jnp.logical_and(above, mn < 0.0), ge0, n_adm)
    up0 = jnp.where(above, INF, 0.0)
    c_up0 = jnp.where(above, 0.0, ge0)
    done0 = jnp.where(jnp.logical_and(live0, jnp.logical_not(tie0)), 0, 1)
    done0 = jnp.where(c_lo0 == kf, 1, done0)
    tie_init = jnp.where(tie0, 1, 0)
    st = (jnp.int32(0), _any_set(done0 == 0), jnp.int32(0), lo0, c_lo0, up0, c_up0,
          row(1.0), row(1.0), row(0.0), done0)
    st = lax.while_loop(lambda s: (s[0] < SEARCH_FAST_ITERS) & (s[1] > 0) & (s[2] == 0), count_pass, st)
    it1, active1, _, lo1, c_lo1, up1, c_up1, _, _, _, done1 = st
    st = (it1, active1, lo1, c_lo1, up1, c_up1, done1, tie_init)
    st = lax.while_loop(lambda s: (s[0] < SEARCH_MAX_ITERS) & (s[1] > 0), snap_pass, st)
    return st[2], st[7]


def _selection_bias(x, thr, thr_valid, need, seen, tri, with_ties):
    if not with_ties:
        return jnp.where(x >= thr_valid, 0.0, MASK_BIAS), seen
    eq = x == thr
    eqf = jnp.where(eq, 1.0, 0.0)
    pref = jnp.dot(tri, eqf.astype(BF16), preferred_element_type=F32) + seen
    take = jnp.where(eq, jnp.where(pref <= need, 1.0, 0.0), jnp.where(x > thr, 1.0, 0.0))
    bias = jnp.where(jnp.where(x > -INF, take, 0.0) > 0.5, 0.0, MASK_BIAS)
    return bias, seen + jnp.sum(eqf, axis=0, keepdims=True)


def _tie_setup(sc_ref, n_units, unit, lb, qb, n_sel, thr):
    def gt_blk(u, c):
        x = sc_ref[pl.ds(pl.multiple_of(u * unit, unit), unit), :]
        return c + _fold(jnp.where(x > thr, 1.0, 0.0), jnp.sum)
    gt = _fin(lax.fori_loop(0, n_units, gt_blk, jnp.zeros((ACC_ROWS, qb), F32)), jnp.sum)
    ri = lax.broadcasted_iota(I32, (lb, lb), 0)
    ci = lax.broadcasted_iota(I32, (lb, lb), 1)
    return float(n_sel) - gt, jnp.where(ci <= ri, 1.0, 0.0).astype(BF16)


def _dsa_body(qT_ref, qiT_ref, iwT_ref, k_ref, vT_ref, ki_ref, oT_ref,
              sc_ref, qpad_ref, bias_ref, sa_ref, sb_ref, acc_ref, m_ref, alpha_ref, bm_ref, seen_ref,
              *, qb, lb, sub, n_sel):
    j = pl.program_id(1)
    n_kb = j + 1
    lane = lax.broadcasted_iota(I32, (1, qb), 1)
    qchunk = (j * qb + lane) >> CHUNK_SHIFT
    fold, fin = _fold, _fin

    def rows(kb):
        return pl.ds(pl.multiple_of(kb * lb, lb), lb)

    def idx_block(kb, carry):
        mx8, mn8, n8 = carry
        r0 = pl.multiple_of(kb * lb, lb)
        for s in range(lb // sub):
            kis = ki_ref[pl.ds(r0 + s * sub, sub), :]
            score = jnp.zeros((sub, qb), F32)
            for h in range(N_IDX_HEADS):
                sh = jnp.dot(kis, qiT_ref[h * IDX_DIM:(h + 1) * IDX_DIM, :], preferred_element_type=F32)
                score = score + jnp.maximum(sh, 0.0) * iwT_ref[h:h + 1, :]
            kpos = r0 + s * sub + lax.broadcasted_iota(I32, (sub, qb), 0)
            adm = (kpos >> CHUNK_SHIFT) <= qchunk
            sc_ref[pl.ds(r0 + s * sub, sub), :] = jnp.where(adm, score, -INF)
            mx8 = jnp.maximum(mx8, fold(jnp.where(adm, score, -INF), jnp.max))
            mn8 = jnp.minimum(mn8, fold(jnp.where(adm, score, INF), jnp.min))
            n8 = n8 + fold(jnp.where(adm, 1.0, 0.0), jnp.sum)
        return mx8, mn8, n8

    full8 = lambda v: jnp.full((ACC_ROWS, qb), v, F32)
    mx8, mn8, n8 = lax.fori_loop(0, n_kb, idx_block, (full8(-INF), full8(INF), full8(0.0)))
    mx, mn, n_adm = fin(mx8, jnp.max), fin(mn8, jnp.min), fin(n8, jnp.sum)

    @pl.when(n_kb % 2 == 1)
    def _():
        sc_ref[rows(n_kb), :] = jnp.full((lb, qb), -INF, F32)

    n_units, unit = (n_kb + 1) // 2, 2 * lb
    thr, tie = _select_threshold(sc_ref, n_units, unit, qb, n_sel, mx, mn, n_adm, lane >= 0)
    has_ties = _any_set(tie == 1) > 0
    thr_valid = jnp.maximum(thr, -F32_MAX)

    qpad_ref[...] = jnp.zeros(qpad_ref.shape, BF16)
    for h in range(N_HEADS):
        r = (h % 2) * HEAD_DIM
        qpad_ref[h, r:r + HEAD_DIM, :] = qT_ref[h * HEAD_DIM:(h + 1) * HEAD_DIM, :]

    def attention(with_ties):
        m_ref[...] = jnp.full(m_ref.shape, M_FLOOR, F32)
        acc_ref[...] = jnp.zeros(acc_ref.shape, F32)
        seen_ref[...] = jnp.zeros(seen_ref.shape, F32)
        need, tri = _tie_setup(sc_ref, n_units, unit, lb, qb, n_sel, thr) if with_ties else (None, None)

        def scores(kb, s_ref):
            bias_ref[...], seen_ref[...] = _selection_bias(sc_ref[rows(kb), :], thr, thr_valid, need,
                                                           seen_ref[...], tri, with_ties)
            for h in range(N_HEADS):
                p2 = (h // 2) * 2 * HEAD_DIM
                s = jnp.dot(k_ref[rows(kb), p2:p2 + 2 * HEAD_DIM], qpad_ref[h],
                            preferred_element_type=F32) + bias_ref[...]
                s_ref[h] = s
                bm_ref[h:h + 1, :] = jnp.max(s, axis=0, keepdims=True)
            m_old = m_ref[...]
            m_new = jnp.maximum(m_old, bm_ref[...])
            alpha_ref[...] = jnp.exp(m_old - m_new)
            m_ref[...] = m_new

        def values(kb, s_ref):
            m_new, alpha = m_ref[...], alpha_ref[...]
            for h in range(N_HEADS):
                p = jnp.exp(s_ref[h] - m_new[h:h + 1, :]).astype(BF16)
                vs = slice(h * V_SLAB, (h + 1) * V_SLAB)
                pv = jnp.dot(vT_ref[vs, rows(kb)], p, preferred_element_type=F32)
                acc_ref[vs, :] = acc_ref[vs, :] * alpha[h:h + 1, :] + pv

        def step(kb, s_prev, s_cur):
            values(kb - 1, s_prev)
            scores(kb, s_cur)

        scores(0, sa_ref)

        def body(kb, carry):
            @pl.when(kb % 2 == 1)
            def _():
                step(kb, sa_ref, sb_ref)

            @pl.when(kb % 2 == 0)
            def _():
                step(kb, sb_ref, sa_ref)
            return carry

        lax.fori_loop(1, n_kb, body, 0)

        @pl.when(n_kb % 2 == 1)
        def _():
            values(n_kb - 1, sa_ref)

        @pl.when(n_kb % 2 == 0)
        def _():
            values(n_kb - 1, sb_ref)

        for h in range(N_HEADS):
            num = acc_ref[h * V_SLAB:h * V_SLAB + HEAD_DIM, :]
            den = acc_ref[h * V_SLAB + HEAD_DIM:h * V_SLAB + HEAD_DIM + 1, :]
            oT_ref[h * HEAD_DIM:(h + 1) * HEAD_DIM, :] = num / den

    @pl.when(has_ties)
    def _():
        attention(True)

    @pl.when(jnp.logical_not(has_ties))
    def _():
        attention(False)


def _dsa(qT, qiT, iwT, k, vT, ki, *, b, qb, lb, n_sel):
    n = qT.shape[1]
    t = n // b
    nqb = t // qb
    assert t % (2 * lb) == 0 and qb == lb
    body = functools.partial(_dsa_body, qb=qb, lb=lb, sub=64, n_sel=n_sel)
    qcol = lambda height: pl.BlockSpec((height, qb), lambda bi, j: (0, bi * nqb + j))
    return pl.pallas_call(
        body,
        grid=(b, nqb),
        in_specs=[
            qcol(D_ATT), qcol(N_IDX_HEADS * IDX_DIM), qcol(N_IDX_HEADS),
            _resident((t, D_ATT), lambda bi, j: (bi, 0)),
            _resident((N_HEADS * V_SLAB, t), lambda bi, j: (0, bi)),
            _resident((t, IDX_DIM), lambda bi, j: (bi, 0)),
        ],
        out_specs=qcol(D_ATT),
        out_shape=jax.ShapeDtypeStruct((D_ATT, n), F32),
        scratch_shapes=[
            pltpu.VMEM((t, qb), F32),
            pltpu.VMEM((N_HEADS, 2 * HEAD_DIM, qb), BF16),
            pltpu.VMEM((lb, qb), F32),
            pltpu.VMEM((N_HEADS, lb, qb), F32),
            pltpu.VMEM((N_HEADS, lb, qb), F32),
            pltpu.VMEM((N_HEADS * V_SLAB, qb), F32),
            pltpu.VMEM((N_HEADS, qb), F32),
            pltpu.VMEM((N_HEADS, qb), F32),
            pltpu.VMEM((N_HEADS, qb), F32),
            pltpu.VMEM((1, qb), F32),
        ],
        compiler_params=_cparams(("parallel", "arbitrary")),
        name="dsa",
    )(qT, qiT, iwT, k, vT, ki)


def _dsa_sample_body(wq_ref, wqi_ref, iw_ref, kc_ref, vc_ref, kic_ref, kn_ref, vn_ref, kin_ref, o_ref,
                     sc_ref, s_ref, kx_ref, vx_ref, kix_ref, acc_ref, den_ref,
                     *, lb, n_cache_kb, past, t_new, n_sel):
    qb = LANES
    n_kb = n_cache_kb + 1
    l_valid = past + t_new
    lane = lax.broadcasted_iota(I32, (1, qb), 1)
    qchunk = (past + (lane & (t_new - 1))) >> CHUNK_SHIFT
    full8 = lambda v: jnp.full((ACC_ROWS, qb), v, F32)

    def rows(kb):
        return pl.ds(kb * lb, lb) if isinstance(kb, int) else pl.ds(pl.multiple_of(kb * lb, lb), lb)

    def over_blocks(fn, carry, cache_ref, new_ref):
        carry = lax.fori_loop(0, n_cache_kb, lambda kb, c: fn(kb, c, cache_ref[0, rows(kb), :].astype(BF16)), carry)
        return fn(n_cache_kb, carry, new_ref[...])

    for new_ref, stage_ref in ((kn_ref, kx_ref), (vn_ref, vx_ref), (kin_ref, kix_ref)):
        stage_ref[...] = jnp.zeros(stage_ref.shape, BF16)
        stage_ref[0:t_new, :] = new_ref[0].astype(BF16)

    def idx_block(kb, carry, kis):
        mx8, mn8, n8 = carry
        r = jnp.maximum(jnp.dot(kis, wqi_ref[0], preferred_element_type=F32), 0.0) * iw_ref[0]
        for shift in (t_new, 2 * t_new, 4 * t_new):
            r = r + pltpu.roll(r, shift, 1)
        kpos = kb * lb + lax.broadcasted_iota(I32, (lb, qb), 0)
        adm = jnp.where(kpos < l_valid, kpos >> CHUNK_SHIFT, qchunk + 1) <= qchunk
        sc_ref[rows(kb), :] = jnp.where(adm, r, -INF)
        return (jnp.maximum(mx8, _fold(jnp.where(adm, r, -INF), jnp.max)),
                jnp.minimum(mn8, _fold(jnp.where(adm, r, INF), jnp.min)),
                n8 + _fold(jnp.where(adm, 1.0, 0.0), jnp.sum))

    mx8, mn8, n8 = over_blocks(idx_block, (full8(-INF), full8(INF), full8(0.0)), kic_ref, kix_ref)
    mx, mn, n_adm = _fin(mx8, jnp.max), _fin(mn8, jnp.min), _fin(n8, jnp.sum)
    if n_kb % 2 == 1:
        sc_ref[rows(n_kb), :] = jnp.full((lb, qb), -INF, F32)
    n_units, unit = (n_kb + 1) // 2, 2 * lb
    thr, tie = _select_threshold(sc_ref, n_units, unit, qb, n_sel, mx, mn, n_adm, lane >= 0)
    has_ties = _any_set(tie == 1) > 0
    thr_valid = jnp.maximum(thr, -F32_MAX)

    def attention(with_ties):
        need, tri = _tie_setup(sc_ref, n_units, unit, lb, qb, n_sel, thr) if with_ties else (None, None)

        def score_block(kb, carry, kblk):
            m8, seen = carry
            bias, seen = _selection_bias(sc_ref[rows(kb), :], thr, thr_valid, need, seen, tri, with_ties)
            s = jnp.dot(kblk, wq_ref[0], preferred_element_type=F32) + bias
            s_ref[rows(kb), :] = s
            return jnp.maximum(m8, _fold(s, jnp.max)), seen

        m8, _ = over_blocks(score_block, (full8(M_FLOOR), jnp.zeros((1, qb), F32)), kc_ref, kx_ref)
        m = _fin(m8, jnp.max)
        acc_ref[...] = jnp.zeros(acc_ref.shape, F32)
        den_ref[...] = jnp.zeros(den_ref.shape, F32)
        ones = jnp.ones((lb, LANES), BF16)

        def value_block(kb, carry, vblk):
            pT = jnp.exp(s_ref[rows(kb), :] - m).T.astype(BF16)
            acc_ref[...] += jnp.dot(pT, vblk, preferred_element_type=F32)
            den_ref[...] += jnp.dot(pT, ones, preferred_element_type=F32)
            return carry

        over_blocks(value_block, 0, vc_ref, vx_ref)
        for h in range(N_HEADS):
            rs = slice(h * t_new, (h + 1) * t_new)
            cs = slice(h * HEAD_DIM, (h + 1) * HEAD_DIM)
            o_ref[0, :, cs] = acc_ref[rs, cs] / den_ref[rs, 0:1]

    @pl.when(has_ties)
    def _():
        attention(True)

    @pl.when(jnp.logical_not(has_ties))
    def _():
        attention(False)


def _dsa_sample(q, qi, iw, k_cache, v_cache, ki_cache, k_new, v_new, ki_new, *, lb, n_sel):
    b, t, _ = q.shape
    past = k_cache.shape[1]
    assert N_HEADS * t == LANES and past % lb == 0 and t <= lb
    n_cache_kb = past // lb
    lp = (n_cache_kb + 1 + (n_cache_kb + 1) % 2) * lb
    eye = jnp.eye(N_HEADS, dtype=q.dtype)
    qh = jnp.transpose(q.reshape(b, t, N_HEADS, HEAD_DIM), (0, 2, 3, 1))
    wq = jnp.einsum("bhdq,gh->bgdhq", qh, eye).reshape(b, D_ATT, LANES)
    wqi = jnp.transpose(qi.reshape(b, t, N_IDX_HEADS, IDX_DIM), (0, 3, 2, 1)).reshape(b, IDX_DIM, LANES)
    iw_row = jnp.transpose(iw, (0, 2, 1)).reshape(b, 1, LANES)
    per_b = lambda shape: pl.BlockSpec((1,) + shape, lambda bi: (bi, 0, 0))
    body = functools.partial(_dsa_sample_body, lb=lb, n_cache_kb=n_cache_kb, past=past, t_new=t, n_sel=n_sel)
    return pl.pallas_call(
        body,
        grid=(b,),
        in_specs=[per_b((D_ATT, LANES)), per_b((IDX_DIM, LANES)), per_b((1, LANES)),
                  per_b((past, D_ATT)), per_b((past, D_ATT)), per_b((past, IDX_DIM)),
                  per_b((t, D_ATT)), per_b((t, D_ATT)), per_b((t, IDX_DIM))],
        out_specs=per_b((t, D_ATT)),
        out_shape=jax.ShapeDtypeStruct((b, t, D_ATT), F32),
        scratch_shapes=[
            pltpu.VMEM((lp, LANES), F32),
            pltpu.VMEM((lp, LANES), F32),
            pltpu.VMEM((lb, D_ATT), BF16),
            pltpu.VMEM((lb, D_ATT), BF16),
            pltpu.VMEM((lb, IDX_DIM), BF16),
            pltpu.VMEM((LANES, D_ATT), F32),
            pltpu.VMEM((LANES, LANES), F32),
        ],
        compiler_params=_cparams(("parallel",)),
        name="dsa_sample",
    )(wq, wqi, iw_row, k_cache, v_cache, ki_cache, k_new, v_new, ki_new)


def _tail_body(x_ref, o_ref, mc_ref, ga_ref, wao_ref, wout_ref, g_ref, w1_ref, w2_ref, y_ref,
               *, ff_chunk, o_transposed):
    o = o_ref[...].T if o_transposed else o_ref[...]
    attn = jnp.dot(o.astype(BF16), wao_ref[...], preferred_element_type=F32)
    merged = mc_ref[...] + ga_ref[...] * attn
    x1 = x_ref[...] + jnp.dot(merged.astype(BF16), wout_ref[...], preferred_element_type=F32)
    ms = jnp.mean(x1 * x1, axis=-1, keepdims=True)
    h = (x1 * lax.rsqrt(ms + EPS) * g_ref[...]).astype(BF16)
    y = x1
    for c in range(D_FF // ff_chunk):
        cs = slice(c * ff_chunk, (c + 1) * ff_chunk)
        u = jnp.maximum(jnp.dot(h, w1_ref[:, cs], preferred_element_type=F32), 0.0)
        y = y + jnp.dot((u * u).astype(BF16), w2_ref[cs, :], preferred_element_type=F32)
    y_ref[...] = y


def _tail(x, o, mc, ga, wao, wout, g, w1, w2, tm, o_transposed):
    n = x.shape[0]
    row = lambda width: pl.BlockSpec((tm, width), lambda i: (i, 0))
    full = lambda a: _resident(a.shape, lambda i: (0, 0))
    o_spec = pl.BlockSpec((D_ATT, tm), lambda i: (0, i)) if o_transposed else row(D_ATT)
    return pl.pallas_call(
        functools.partial(_tail_body, ff_chunk=1024, o_transposed=o_transposed),
        grid=(n // tm,),
        in_specs=[row(D_MODEL), o_spec, row(D_MODEL), row(D_MODEL),
                  full(wao), full(wout), pl.BlockSpec((1, D_MODEL), lambda i: (0, 0)), full(w1), full(w2)],
        out_specs=row(D_MODEL),
        out_shape=jax.ShapeDtypeStruct((n, D_MODEL), F32),
        compiler_params=_cparams(("parallel",)),
        name="tail",
    )(x, o, mc, ga, wao, wout, g, w1, w2)


def _rope_tables(pos):
    inv = ROPE_THETA ** (-jnp.arange(ROT_HALF, dtype=F32) / ROT_HALF)
    ang = pos.astype(F32)[:, None] * inv[None, :]
    cos, sin = jnp.cos(ang), jnp.sin(ang)
    r = np.arange(LANES) % HEAD_DIM
    jj = r % ROT_HALF
    first = jnp.asarray(r < ROT_HALF)[None, :]
    second = jnp.asarray((r >= ROT_HALF) & (r < ROT_DIM))[None, :]
    c = jnp.where(first | second, cos[:, jj], 1.0)
    sa = jnp.where(first, -sin[:, jj], 0.0)
    sb = jnp.where(second, sin[:, jj], 0.0)
    return c, sa, sb


def _pack_w_in(w):
    pad = jnp.zeros((D_MODEL, LANES - (D_IN_HEAD - COL_KIW)), w.dtype)
    return jnp.concatenate([w[:, :D_IN_HEAD], pad, w[:, D_IN_HEAD:]], axis=1).astype(BF16)


def _tile2(v):
    return jnp.concatenate([v, v])[None, :].astype(F32)


def _layer(x, tabs, n_tab_blocks, conv_state, caches, w, *, tm, conv_tm, dsa_cfg):
    b, t, _ = x.shape
    n = b * t
    prompt = caches is None
    (norm_mix, w_in_p, conv_w, conv_b, ln_g, ln_b, w_conv_out, q_norm, k_norm,
     w_attn_out, w_out, norm_ffn, w_ff1, w_ff2) = w
    xf = x.reshape(n, D_MODEL)
    glu, q, k, v, qi, kiw, gc, ga, *extra = _inproj(
        xf, norm_mix[None, :], w_in_p, *tabs, _tile2(q_norm), _tile2(k_norm), tm, n_tab_blocks, prompt)

    glu3 = glu.reshape(b, t, D_CONV)
    state_p = jnp.pad(conv_state, ((0, 0), (HALO - (CONV_W - 1), 0), (0, 0)))
    halo_src = glu3 if t >= HALO else state_p
    cw_p = jnp.pad(conv_w, ((0, HALO - CONV_W), (0, 0)))
    mc = _conv_branch(glu3, halo_src, state_p, cw_p, conv_b[None, :], ln_g[None, :], ln_b[None, :],
                      w_conv_out, gc.reshape(b, t, D_MODEL), conv_tm)
    new_conv = jnp.concatenate([conv_state, glu3], axis=1)[:, -(CONV_W - 1):]

    ki = kiw[:, :IDX_DIM]
    if prompt:
        kb, vt_ones, iwT = extra
        o = _dsa(q, qi, iwT, kb, vt_ones, ki.astype(BF16), b=b, **dsa_cfg)
    else:
        k_prev, v_prev, ki_prev = caches
        p = k_prev.shape[1]
        iw = kiw[:, IDX_DIM:IDX_DIM + N_IDX_HEADS].reshape(b, t, N_IDX_HEADS)
        o = _dsa_sample(q.reshape(b, t, D_ATT), qi.reshape(b, t, D_ATT), iw,
                        k_prev.reshape(b, p, D_ATT), v_prev.reshape(b, p, D_ATT), ki_prev,
                        k.reshape(b, t, D_ATT), v.reshape(b, t, D_ATT), ki.reshape(b, t, IDX_DIM),
                        **dsa_cfg).reshape(n, D_ATT)

    y = _tail(xf, o, mc.reshape(n, D_MODEL), ga, w_attn_out, w_out, norm_ffn[None, :], w_ff1, w_ff2, tm, prompt)
    return (y.reshape(b, t, D_MODEL), k.reshape(b, t, N_HEADS, HEAD_DIM), v.reshape(b, t, N_HEADS, HEAD_DIM),
            ki.reshape(b, t, IDX_DIM), new_conv)


def kernel(x_prompt, x_sample, cache_k, cache_v, cache_kidx, state_conv, norm_mix, w_in, conv_w, conv_b,
           conv_ln_g, conv_ln_b, w_conv_out, q_norm, k_norm, w_attn_out, w_out, norm_ffn, w_ff1, w_ff2):
    bp, tp, _ = x_prompt.shape
    bs, ts, _ = x_sample.shape
    depth = norm_mix.shape[0]
    past = cache_k.shape[2]
    tm = 256
    tabs_p = _rope_tables(jnp.arange(tp, dtype=I32))
    tabs_s = tuple(jnp.tile(a, (bs, 1)) for a in _rope_tables(past + jnp.arange(ts, dtype=I32)))
    cfg_p = dict(qb=256, lb=256, n_sel=min(TOPK_MAX, tp // 4))
    cfg_s = dict(lb=256, n_sel=min(TOPK_MAX, (past + ts) // 4))
    hp, hs = x_prompt, x_sample
    outs_p, outs_s = [], []
    for l in range(depth):
        w = (norm_mix[l], _pack_w_in(w_in[l]), conv_w[l], conv_b[l], conv_ln_g[l], conv_ln_b[l],
             w_conv_out[l].astype(BF16), q_norm[l], k_norm[l], w_attn_out[l].astype(BF16),
             w_out[l].astype(BF16), norm_ffn[l], w_ff1[l].astype(BF16), w_ff2[l].astype(BF16))
        zero_state = jnp.zeros((bp, CONV_W - 1, D_CONV), F32)
        hp, *rest_p = _layer(hp, tabs_p, tp // tm, zero_state, None, w, tm=tm, conv_tm=tm, dsa_cfg=cfg_p)
        hs, *rest_s = _layer(hs, tabs_s, (bs * ts) // tm, state_conv[l],
                             (cache_k[l], cache_v[l], cache_kidx[l]), w, tm=tm, conv_tm=ts, dsa_cfg=cfg_s)
        outs_p.append(rest_p)
        outs_s.append(rest_s)
    stack = lambda outs, i: jnp.stack([o[i] for o in outs])
    return (hp, hs,
            stack(outs_p, 0), stack(outs_p, 1), stack(outs_p, 2), stack(outs_p, 3),
            stack(outs_s, 0), stack(outs_s, 1), stack(outs_s, 2), stack(outs_s, 3))
```

```python
import functools

import jax
import jax.numpy as jnp
import numpy as np
from jax import lax
from jax.experimental import pallas as pl
from jax.experimental.pallas import tpu as pltpu

F32 = jnp.float32
BF16 = jnp.bfloat16
I32 = jnp.int32

D_MODEL = 1024
CHUNK = 64
CHUNK_SHIFT = 6
N_HEADS = 8
HEAD_DIM = 64
ROT_DIM = HEAD_DIM // 4
ROT_HALF = ROT_DIM // 2
ROPE_THETA = 500000.0
N_IDX_HEADS = 8
IDX_DIM = 64
TOPK_MAX = 256
D_CONV = 512
CONV_W = 31
D_FF = 4 * D_MODEL
EPS = 1e-6
LN_EPS = 1e-5
ATTN_SCALE = HEAD_DIM ** -0.5
IDX_SCALE = (N_IDX_HEADS ** -0.5) * (IDX_DIM ** -0.5)
D_ATT = N_HEADS * HEAD_DIM

LANES = 128
SUBLANES = 8
ACC_ROWS = 4 * SUBLANES
VMEM_LIMIT_BYTES = 56 * 1024 * 1024

COL_KIW = 6 * 512
COL_GC = COL_KIW + LANES
COL_GA = COL_GC + D_MODEL
D_IN_PACKED = COL_GA + D_MODEL
D_IN_HEAD = 2 * D_CONV + 3 * D_ATT + N_IDX_HEADS * IDX_DIM + IDX_DIM + N_IDX_HEADS

HALO = 32
INF = float("inf")
F32_MAX = float(np.finfo(np.float32).max)
ONES_ROWS = 16
V_SLAB = HEAD_DIM + ONES_ROWS
M_FLOOR = -1e30
MASK_BIAS = -2e30
SEARCH_FAST_ITERS = 48
SEARCH_MAX_ITERS = 4096


def _cparams(sem):
    return pltpu.CompilerParams(dimension_semantics=sem, vmem_limit_bytes=VMEM_LIMIT_BYTES)


def _resident(block_shape, index_map):
    return pl.BlockSpec(block_shape, index_map, pipeline_mode=pl.Buffered(1))


def _rope(x, c, sa, sb):
    return x * c + pltpu.roll(x, LANES - ROT_HALF, 1) * sa + pltpu.roll(x, ROT_HALF, 1) * sb


def _head_rms(x, gain):
    lane = lax.broadcasted_iota(I32, x.shape, 1)
    lo = lane < HEAD_DIM
    x2 = x * x
    s_lo = jnp.sum(jnp.where(lo, x2, 0.0), axis=-1, keepdims=True)
    s_hi = jnp.sum(jnp.where(lo, 0.0, x2), axis=-1, keepdims=True)
    ms = jnp.where(lo, s_lo, s_hi) * (1.0 / HEAD_DIM)
    return x * lax.rsqrt(ms + EPS) * gain


def _inproj_body(*refs, transposed, n_aliased):
    x_ref, g_ref, w_ref, cos_ref, sa_ref, sb_ref, qg_ref, kg_ref = refs[:8]
    glu_ref, q_ref, qi_ref, kiw_ref, gc_ref, ga_ref, k5_ref, v5_ref, *extra = refs[8 + n_aliased:]
    tm = x_ref.shape[0]
    x = x_ref[...]
    ms = jnp.mean(x * x, axis=-1, keepdims=True)
    h = (x * lax.rsqrt(ms + EPS) * g_ref[...]).astype(BF16)

    def proj(c0, n):
        return jnp.dot(h, w_ref[:, c0:c0 + n], preferred_element_type=F32)

    glu_ref[...] = proj(0, D_CONV) * jax.nn.sigmoid(proj(D_CONV, D_CONV))

    c, sa, sb = cos_ref[...], sa_ref[...], sb_ref[...]
    zq = proj(2 * D_CONV, D_ATT)
    zk = proj(2 * D_CONV + D_ATT, D_ATT)
    zqi = proj(2 * D_CONV + 3 * D_ATT, N_IDX_HEADS * IDX_DIM)
    zv = proj(2 * D_CONV + 2 * D_ATT, D_ATT)
    for g in range(D_ATT // LANES):
        sl = slice(g * LANES, (g + 1) * LANES)
        qh = _rope(_head_rms(zq[:, sl], qg_ref[...]), c, sa, sb) * ATTN_SCALE
        kh = _rope(_head_rms(zk[:, sl], kg_ref[...]), c, sa, sb)
        qih = _rope(zqi[:, sl], c, sa, sb)
        for half in range(2):
            hs = slice(half * HEAD_DIM, (half + 1) * HEAD_DIM)
            head_rows = pl.ds(2 * g + half, tm, stride=N_HEADS)
            k5_ref[head_rows, :] = kh[:, hs]
            v5_ref[head_rows, :] = zv[:, sl][:, hs]
        if not transposed:
            extra[0][:, sl] = kh
            extra[1][:, sl] = zv[:, sl]
        if transposed:
            kb_ref, vt_ref, _ = extra
            q_ref[sl, :] = qh.T.astype(BF16)
            qi_ref[sl, :] = qih.T.astype(BF16)
            kb_ref[:, sl] = kh.astype(BF16)
            vt = zv[:, sl].T.astype(BF16)
            ones = jnp.ones((ONES_ROWS, vt.shape[1]), BF16)
            for half in range(2):
                r0 = (2 * g + half) * V_SLAB
                vt_ref[r0:r0 + HEAD_DIM, :] = vt[half * HEAD_DIM:(half + 1) * HEAD_DIM, :]
                vt_ref[r0 + HEAD_DIM:r0 + V_SLAB, :] = ones
        else:
            q_ref[:, sl] = qh.astype(BF16)
            qi_ref[:, sl] = qih.astype(BF16)

    zkiw = proj(COL_KIW, LANES)
    lane = lax.broadcasted_iota(I32, zkiw.shape, 1)
    is_ki = lane < IDX_DIM
    roped = _rope(zkiw, jnp.where(is_ki, c, 1.0), jnp.where(is_ki, sa, 0.0), jnp.where(is_ki, sb, 0.0))
    kiw = jnp.where(is_ki, roped, zkiw * IDX_SCALE)
    kiw_ref[...] = kiw
    if transposed:
        extra[2][...] = kiw.T[IDX_DIM:IDX_DIM + N_IDX_HEADS, :]

    gc_ref[...] = jax.nn.sigmoid(proj(COL_GC, D_MODEL))
    ga_ref[...] = jax.nn.sigmoid(proj(COL_GA, D_MODEL))


def _inproj(x, g, w, cos, sa, sb, qg, kg, kv_all, layer, depth, tm, n_tab_blocks, transposed):
    n = x.shape[0]
    nb = n // tm
    row = lambda width: pl.BlockSpec((tm, width), lambda i: (i, 0))
    col = lambda height: pl.BlockSpec((height, tm), lambda i: (0, i))
    tab = pl.BlockSpec((tm, LANES), lambda i: (i % n_tab_blocks, 0))
    vec = lambda width: pl.BlockSpec((1, width), lambda i: (0, 0))
    rows_of = lambda width, dt: (jax.ShapeDtypeStruct((n, width), dt), row(width))
    cols_of = lambda height, dt: (jax.ShapeDtypeStruct((height, n), dt), col(height))
    kv5 = (jax.ShapeDtypeStruct((depth * n * N_HEADS, HEAD_DIM), F32),
           pl.BlockSpec((tm * N_HEADS, HEAD_DIM), lambda i: (layer * nb + i, 0)))
    outs = [
        rows_of(D_CONV, F32),
        cols_of(D_ATT, BF16) if transposed else rows_of(D_ATT, BF16),
        cols_of(D_ATT, BF16) if transposed else rows_of(D_ATT, BF16),
        rows_of(LANES, F32),
        rows_of(D_MODEL, F32),
        rows_of(D_MODEL, F32),
        kv5, kv5,
    ]
    if transposed:
        outs += [rows_of(D_ATT, BF16), cols_of(N_HEADS * V_SLAB, BF16), cols_of(N_IDX_HEADS, F32)]
    else:
        outs += [rows_of(D_ATT, F32), rows_of(D_ATT, F32)]
    aliased = tuple(kv_all)
    n_in = 8
    return pl.pallas_call(
        functools.partial(_inproj_body, transposed=transposed, n_aliased=len(aliased)),
        grid=(nb,),
        in_specs=[row(D_MODEL), vec(D_MODEL), _resident((D_MODEL, D_IN_PACKED), lambda i: (0, 0)),
                  tab, tab, tab, vec(LANES), vec(LANES)] + [pl.BlockSpec(memory_space=pl.ANY)] * len(aliased),
        out_specs=[o[1] for o in outs],
        out_shape=[o[0] for o in outs],
        input_output_aliases={n_in + a: 6 + a for a in range(len(aliased))},
        compiler_params=_cparams(("parallel",)),
        name="inproj",
    )(x, g, w, cos, sa, sb, qg, kg, *aliased)


def _conv_body(cur_ref, halo_ref, st_ref, cw_ref, cb_ref, lg_ref, lb_ref, wco_ref, gc_ref,
               out_ref, win_ref, act_ref, *, tm, rows):
    i = pl.program_id(1)
    win_ref[0:HALO, :] = jnp.where(i == 0, st_ref[0], halo_ref[0])
    win_ref[HALO:HALO + tm, :] = cur_ref[0]
    off = HALO - (CONV_W - 1)
    for r in range(tm // rows):
        acc = jnp.zeros((rows, D_CONV), F32) + cb_ref[...]
        for j in range(CONV_W):
            acc = acc + win_ref[r * rows + off + j:r * rows + off + j + rows, :] * cw_ref[j:j + 1, :]
        mu = jnp.mean(acc, axis=-1, keepdims=True)
        d = acc - mu
        var = jnp.mean(d * d, axis=-1, keepdims=True)
        y = d * lax.rsqrt(var + LN_EPS) * lg_ref[...] + lb_ref[...]
        act_ref[r * rows:(r + 1) * rows, :] = (y * jax.nn.sigmoid(y)).astype(BF16)
    out_ref[0] = gc_ref[0] * jnp.dot(act_ref[...], wco_ref[...], preferred_element_type=F32)


def _conv_branch(glu, halo_src, state, cw, cb, lg, lb, wco, gc, tm):
    b, t, _ = glu.shape
    rows = min(tm, 32)
    hb = tm // HALO
    vec = pl.BlockSpec((1, D_CONV), lambda bi, i: (0, 0))
    return pl.pallas_call(
        functools.partial(_conv_body, tm=tm, rows=rows),
        grid=(b, t // tm),
        in_specs=[
            pl.BlockSpec((1, tm, D_CONV), lambda bi, i: (bi, i, 0)),
            pl.BlockSpec((1, HALO, D_CONV), lambda bi, i: (bi, jnp.maximum(i * hb - 1, 0), 0)),
            pl.BlockSpec((1, HALO, D_CONV), lambda bi, i: (bi, 0, 0)),
            pl.BlockSpec((HALO, D_CONV), lambda bi, i: (0, 0)),
            vec, vec, vec,
            _resident((D_CONV, D_MODEL), lambda bi, i: (0, 0)),
            pl.BlockSpec((1, tm, D_MODEL), lambda bi, i: (bi, i, 0)),
        ],
        out_specs=pl.BlockSpec((1, tm, D_MODEL), lambda bi, i: (bi, i, 0)),
        out_shape=jax.ShapeDtypeStruct((b, t, D_MODEL), F32),
        scratch_shapes=[pltpu.VMEM((HALO + tm, D_CONV), F32), pltpu.VMEM((tm, D_CONV), BF16)],
        compiler_params=_cparams(("parallel", "arbitrary")),
        name="conv_branch",
    )(glu, halo_src, state, cw, cb, lg, lb, wco, gc)


def _fold(x, op):
    return op(x.reshape(x.shape[0] // ACC_ROWS, ACC_ROWS, x.shape[1]), axis=0)


def _fin(x8, op):
    return op(x8, axis=0, keepdims=True)


def _any_set(flag):
    return jnp.max(jnp.where(flag, 1, 0))


def _select_threshold(sc_ref, n_units, unit, qb, n_sel, mx, mn, n_adm, lane_ok):
    kf = float(n_sel)
    log_target = float(np.log(n_sel + 0.5))
    full8 = lambda v: jnp.full((ACC_ROWS, qb), v, F32)
    row = lambda v: jnp.full((1, qb), v, F32)

    def rows(u):
        return pl.ds(pl.multiple_of(u * unit, unit), unit)

    def count2(p):
        def blk(u, c):
            x = sc_ref[rows(u), :]
            return (c[0] + _fold(jnp.where(x >= p, 1.0, 0.0), jnp.sum),
                    c[1] + _fold(jnp.where(x > p, 1.0, 0.0), jnp.sum))
        ge8, gt8 = lax.fori_loop(0, n_units, blk, (full8(0.0), full8(0.0)))
        return _fin(ge8, jnp.sum), _fin(gt8, jnp.sum)

    def scan(p, lo, up, snap):
        def blk(u, c):
            x = sc_ref[rows(u), :]
            out = [c[0] + _fold(jnp.where(x >= p, 1.0, 0.0), jnp.sum)]
            if snap:
                out.append(jnp.minimum(c[1], _fold(jnp.where(x >= lo, x, INF), jnp.min)))
                out.append(jnp.maximum(c[2], _fold(jnp.where(x < up, x, -INF), jnp.max)))
            return tuple(out)
        init = (full8(0.0), full8(INF), full8(-INF)) if snap else (full8(0.0),)
        res = lax.fori_loop(0, n_units, blk, init)
        if snap:
            return _fin(res[0], jnp.sum), _fin(res[1], jnp.min), _fin(res[2], jnp.max)
        return _fin(res[0], jnp.sum)

    def pivot(lo, c_lo, up, c_up, wl, wu, bisect):
        xu = jnp.where(up == INF, mx, up)
        gl = (jnp.log(c_lo) - log_target) * wl
        gu = (log_target - jnp.log(jnp.maximum(c_up, 0.5))) * wu
        p = jnp.where(bisect, 0.5 * lo + 0.5 * xu, lo + (xu - lo) * (gl / (gl + gu)))
        p = jnp.minimum(p, xu)
        stuck = jnp.logical_not(p > lo)
        return jnp.where(stuck, xu, p), stuck

    def update(p, c, lo, c_lo, up, c_up, live):
        ge = c >= kf
        to_lo = jnp.logical_and(live, ge)
        to_up = jnp.logical_and(live, jnp.logical_not(ge))
        return (jnp.where(to_lo, p, lo), jnp.where(to_lo, c, c_lo),
                jnp.where(to_up, p, up), jnp.where(to_up, c, c_up), to_lo, to_up)

    def count_pass(st):
        it, _, _, lo, c_lo, up, c_up, wl, wu, side, done = st
        live = done == 0
        p, stuck = pivot(lo, c_lo, up, c_up, wl, wu, it % 8 == 7)
        c = scan(p, lo, up, False)
        lo, c_lo, up, c_up, to_lo, to_up = update(p, c, lo, c_lo, up, c_up, live)
        wl = jnp.where(jnp.logical_and(to_up, side < 0.0), wl * 0.5, jnp.where(to_lo, 1.0, wl))
        wu = jnp.where(jnp.logical_and(to_lo, side > 0.0), wu * 0.5, jnp.where(to_up, 1.0, wu))
        side = jnp.where(to_lo, 1.0, jnp.where(to_up, -1.0, side))
        done = jnp.where(c_lo == kf, 1, done)
        flags = jnp.sum(jnp.where(done == 0, 1, 0) + jnp.where(jnp.logical_and(stuck, live), 1 << 16, 0))
        return (it + 1, flags & 0xFFFF, flags >> 16, lo, c_lo, up, c_up, wl, wu, side, done)

    def snap_pass(st):
        it, _, lo, c_lo, up, c_up, done, tie = st
        live = done == 0
        p, _ = pivot(lo, c_lo, up, c_up, row(1.0), row(1.0), it % 2 == 1)
        c, a, b = scan(p, lo, up, True)
        tied = jnp.logical_and(live, a == b)
        lo2, c_lo2, up2, c_up2, _, _ = update(p, c, lo, c_lo, up, c_up,
                                              jnp.logical_and(live, jnp.logical_not(tied)))
        lo2 = jnp.where(live, jnp.maximum(lo2, a), lo2)
        tie = jnp.where(tied, 1, tie)
        done = jnp.where(jnp.logical_or(tied, c_lo2 == kf), 1, done)
        return (it + 1, _any_set(done == 0), lo2, c_lo2, up2, c_up2, done, tie)

    few = n_adm <= kf
    live0 = jnp.logical_and(jnp.logical_not(few), lane_ok)
    ge0, gt0 = count2(row(0.0))
    tie0 = jnp.logical_and(live0, jnp.logical_and(gt0 < kf, ge0 >= kf))
    above = ge0 >= kf
    lo0 = jnp.where(few, -INF, jnp.where(tie0, 0.0, jnp.where(above, jnp.maximum(mn, 0.0), mn)))
    c_lo0 = jnp.where(jnp.logical_and(above, mn < 0.0), ge0, n_adm)
    up0 = jnp.where(above, INF, 0.0)
    c_up0 = jnp.where(above, 0.0, ge0)
    done0 = jnp.where(jnp.logical_and(live0, jnp.logical_not(tie0)), 0, 1)
    done0 = jnp.where(c_lo0 == kf, 1, done0)
    tie_init = jnp.where(tie0, 1, 0)
    st = (jnp.int32(0), _any_set(done0 == 0), jnp.int32(0), lo0, c_lo0, up0, c_up0,
          row(1.0), row(1.0), row(0.0), done0)
    st = lax.while_loop(lambda s: (s[0] < SEARCH_FAST_ITERS) & (s[1] > 0) & (s[2] == 0), count_pass, st)
    it1, active1, _, lo1, c_lo1, up1, c_up1, _, _, _, done1 = st
    st = (it1, active1, lo1, c_lo1, up1, c_up1, done1, tie_init)
    st = lax.while_loop(lambda s: (s[0] < SEARCH_MAX_ITERS) & (s[1] > 0), snap_pass, st)
    return st[2], st[7]


def _selection_bias(x, thr, thr_valid, need, seen, tri, with_ties):
    if not with_ties:
        return jnp.where(x >= thr_valid, 0.0, MASK_BIAS), seen
    eq = x == thr
    eqf = jnp.where(eq, 1.0, 0.0)
    pref = jnp.dot(tri, eqf.astype(BF16), preferred_element_type=F32) + seen
    take = jnp.where(eq, jnp.where(pref <= need, 1.0, 0.0), jnp.where(x > thr, 1.0, 0.0))
    bias = jnp.where(jnp.where(x > -INF, take, 0.0) > 0.5, 0.0, MASK_BIAS)
    return bias, seen + jnp.sum(eqf, axis=0, keepdims=True)


def _tie_setup(sc_ref, n_units, unit, lb, qb, n_sel, thr):
    def gt_blk(u, c):
        x = sc_ref[pl.ds(pl.multiple_of(u * unit, unit), unit), :]
        return c + _fold(jnp.where(x > thr, 1.0, 0.0), jnp.sum)
    gt = _fin(lax.fori_loop(0, n_units, gt_blk, jnp.zeros((ACC_ROWS, qb), F32)), jnp.sum)
    ri = lax.broadcasted_iota(I32, (lb, lb), 0)
    ci = lax.broadcasted_iota(I32, (lb, lb), 1)
    return float(n_sel) - gt, jnp.where(ci <= ri, 1.0, 0.0).astype(BF16)


def _dsa_body(qT_ref, qiT_ref, iwT_ref, k_ref, vT_ref, ki_ref, oT_ref,
              sc_ref, qpad_ref, bias_ref, sa_ref, sb_ref, acc_ref, m_ref, alpha_ref, bm_ref, seen_ref,
              *, qb, lb, sub, n_sel):
    j = pl.program_id(1)
    n_kb = j + 1
    lane = lax.broadcasted_iota(I32, (1, qb), 1)
    qchunk = (j * qb + lane) >> CHUNK_SHIFT
    fold, fin = _fold, _fin

    def rows(kb):
        return pl.ds(pl.multiple_of(kb * lb, lb), lb)

    def idx_block(kb, carry):
        mx8, mn8, n8 = carry
        r0 = pl.multiple_of(kb * lb, lb)
        for s in range(lb // sub):
            kis = ki_ref[pl.ds(r0 + s * sub, sub), :]
            score = jnp.zeros((sub, qb), F32)
            for h in range(N_IDX_HEADS):
                sh = jnp.dot(kis, qiT_ref[h * IDX_DIM:(h + 1) * IDX_DIM, :], preferred_element_type=F32)
                score = score + jnp.maximum(sh, 0.0) * iwT_ref[h:h + 1, :]
            kpos = r0 + s * sub + lax.broadcasted_iota(I32, (sub, qb), 0)
            adm = (kpos >> CHUNK_SHIFT) <= qchunk
            sc_ref[pl.ds(r0 + s * sub, sub), :] = jnp.where(adm, score, -INF)
            mx8 = jnp.maximum(mx8, fold(jnp.where(adm, score, -INF), jnp.max))
            mn8 = jnp.minimum(mn8, fold(jnp.where(adm, score, INF), jnp.min))
            n8 = n8 + fold(jnp.where(adm, 1.0, 0.0), jnp.sum)
        return mx8, mn8, n8

    full8 = lambda v: jnp.full((ACC_ROWS, qb), v, F32)
    mx8, mn8, n8 = lax.fori_loop(0, n_kb, idx_block, (full8(-INF), full8(INF), full8(0.0)))
    mx, mn, n_adm = fin(mx8, jnp.max), fin(mn8, jnp.min), fin(n8, jnp.sum)

    @pl.when(n_kb % 2 == 1)
    def _():
        sc_ref[rows(n_kb), :] = jnp.full((lb, qb), -INF, F32)

    n_units, unit = (n_kb + 1) // 2, 2 * lb
    thr, tie = _select_threshold(sc_ref, n_units, unit, qb, n_sel, mx, mn, n_adm, lane >= 0)
    has_ties = _any_set(tie == 1) > 0
    thr_valid = jnp.maximum(thr, -F32_MAX)

    qpad_ref[...] = jnp.zeros(qpad_ref.shape, BF16)
    for h in range(N_HEADS):
        r = (h % 2) * HEAD_DIM
        qpad_ref[h, r:r + HEAD_DIM, :] = qT_ref[h * HEAD_DIM:(h + 1) * HEAD_DIM, :]

    def attention(with_ties):
        m_ref[...] = jnp.full(m_ref.shape, M_FLOOR, F32)
        acc_ref[...] = jnp.zeros(acc_ref.shape, F32)
        seen_ref[...] = jnp.zeros(seen_ref.shape, F32)
        need, tri = _tie_setup(sc_ref, n_units, unit, lb, qb, n_sel, thr) if with_ties else (None, None)

        def scores(kb, s_ref):
            bias_ref[...], seen_ref[...] = _selection_bias(sc_ref[rows(kb), :], thr, thr_valid, need,
                                                           seen_ref[...], tri, with_ties)
            for h in range(N_HEADS):
                p2 = (h // 2) * 2 * HEAD_DIM
                s = jnp.dot(k_ref[rows(kb), p2:p2 + 2 * HEAD_DIM], qpad_ref[h],
                            preferred_element_type=F32) + bias_ref[...]
                s_ref[h] = s
                bm_ref[h:h + 1, :] = jnp.max(s, axis=0, keepdims=True)
            m_old = m_ref[...]
            m_new = jnp.maximum(m_old, bm_ref[...])
            alpha_ref[...] = jnp.exp(m_old - m_new)
            m_ref[...] = m_new

        def values(kb, s_ref):
            m_new, alpha = m_ref[...], alpha_ref[...]
            for h in range(N_HEADS):
                p = jnp.exp(s_ref[h] - m_new[h:h + 1, :]).astype(BF16)
                vs = slice(h * V_SLAB, (h + 1) * V_SLAB)
                pv = jnp.dot(vT_ref[vs, rows(kb)], p, preferred_element_type=F32)
                acc_ref[vs, :] = acc_ref[vs, :] * alpha[h:h + 1, :] + pv

        def step(kb, s_prev, s_cur):
            values(kb - 1, s_prev)
            scores(kb, s_cur)

        scores(0, sa_ref)

        def body(kb, carry):
            @pl.when(kb % 2 == 1)
            def _():
                step(kb, sa_ref, sb_ref)

            @pl.when(kb % 2 == 0)
            def _():
                step(kb, sb_ref, sa_ref)
            return carry

        lax.fori_loop(1, n_kb, body, 0)

        @pl.when(n_kb % 2 == 1)
        def _():
            values(n_kb - 1, sa_ref)

        @pl.when(n_kb % 2 == 0)
        def _():
            values(n_kb - 1, sb_ref)

        for h in range(N_HEADS):
            num = acc_ref[h * V_SLAB:h * V_SLAB + HEAD_DIM, :]
            den = acc_ref[h * V_SLAB + HEAD_DIM:h * V_SLAB + HEAD_DIM + 1, :]
            oT_ref[h * HEAD_DIM:(h + 1) * HEAD_DIM, :] = num / den

    @pl.when(has_ties)
    def _():
        attention(True)

    @pl.when(jnp.logical_not(has_ties))
    def _():
        attention(False)


def _dsa(qT, qiT, iwT, k, vT, ki, *, b, qb, lb, n_sel):
    n = qT.shape[1]
    t = n // b
    nqb = t // qb
    assert t % (2 * lb) == 0 and qb == lb
    body = functools.partial(_dsa_body, qb=qb, lb=lb, sub=64, n_sel=n_sel)
    qcol = lambda height: pl.BlockSpec((height, qb), lambda bi, j: (0, bi * nqb + j))
    return pl.pallas_call(
        body,
        grid=(b, nqb),
        in_specs=[
            qcol(D_ATT), qcol(N_IDX_HEADS * IDX_DIM), qcol(N_IDX_HEADS),
            _resident((t, D_ATT), lambda bi, j: (bi, 0)),
            _resident((N_HEADS * V_SLAB, t), lambda bi, j: (0, bi)),
            _resident((t, IDX_DIM), lambda bi, j: (bi, 0)),
        ],
        out_specs=qcol(D_ATT),
        out_shape=jax.ShapeDtypeStruct((D_ATT, n), F32),
        scratch_shapes=[
            pltpu.VMEM((t, qb), F32),
            pltpu.VMEM((N_HEADS, 2 * HEAD_DIM, qb), BF16),
            pltpu.VMEM((lb, qb), F32),
            pltpu.VMEM((N_HEADS, lb, qb), F32),
            pltpu.VMEM((N_HEADS, lb, qb), F32),
            pltpu.VMEM((N_HEADS * V_SLAB, qb), F32),
            pltpu.VMEM((N_HEADS, qb), F32),
            pltpu.VMEM((N_HEADS, qb), F32),
            pltpu.VMEM((N_HEADS, qb), F32),
            pltpu.VMEM((1, qb), F32),
        ],
        compiler_params=_cparams(("parallel", "arbitrary")),
        name="dsa",
    )(qT, qiT, iwT, k, vT, ki)


def _dsa_sample_body(wq_ref, wqi_ref, iw_ref, kc_ref, vc_ref, kic_ref, kn_ref, vn_ref, kin_ref, o_ref,
                     sc_ref, s_ref, kx_ref, vx_ref, kix_ref, acc_ref, den_ref,
                     *, lb, n_cache_kb, past, t_new, n_sel):
    qb = LANES
    n_kb = n_cache_kb + 1
    l_valid = past + t_new
    lane = lax.broadcasted_iota(I32, (1, qb), 1)
    qchunk = (past + (lane & (t_new - 1))) >> CHUNK_SHIFT
    full8 = lambda v: jnp.full((ACC_ROWS, qb), v, F32)

    def rows(kb):
        return pl.ds(kb * lb, lb) if isinstance(kb, int) else pl.ds(pl.multiple_of(kb * lb, lb), lb)

    def over_blocks(fn, carry, cache_get, new_get):
        carry = lax.fori_loop(0, n_cache_kb, lambda kb, c: fn(kb, c, functools.partial(cache_get, kb)), carry)
        return fn(n_cache_kb, carry, new_get)

    def cache_head(ref, kb, h):
        base = pl.multiple_of(kb * (lb * N_HEADS), lb * N_HEADS)
        return ref[0, pl.ds(base + h, lb, stride=N_HEADS), :].astype(BF16)

    kix_ref[...] = jnp.zeros(kix_ref.shape, BF16)
    kix_ref[0:t_new, :] = kin_ref[0].astype(BF16)
    for new_ref, stage_ref in ((kn_ref, kx_ref), (vn_ref, vx_ref)):
        stage_ref[...] = jnp.zeros(stage_ref.shape, BF16)
        for h in range(N_HEADS):
            stage_ref[h, 0:t_new, :] = new_ref[0][:, h * HEAD_DIM:(h + 1) * HEAD_DIM].astype(BF16)

    def idx_block(kb, carry, get):
        mx8, mn8, n8 = carry
        r = jnp.maximum(jnp.dot(get(None), wqi_ref[0], preferred_element_type=F32), 0.0) * iw_ref[0]
        for shift in (t_new, 2 * t_new, 4 * t_new):
            r = r + pltpu.roll(r, shift, 1)
        kpos = kb * lb + lax.broadcasted_iota(I32, (lb, qb), 0)
        adm = jnp.where(kpos < l_valid, kpos >> CHUNK_SHIFT, qchunk + 1) <= qchunk
        sc_ref[rows(kb), :] = jnp.where(adm, r, -INF)
        return (jnp.maximum(mx8, _fold(jnp.where(adm, r, -INF), jnp.max)),
                jnp.minimum(mn8, _fold(jnp.where(adm, r, INF), jnp.min)),
                n8 + _fold(jnp.where(adm, 1.0, 0.0), jnp.sum))

    mx8, mn8, n8 = over_blocks(idx_block, (full8(-INF), full8(INF), full8(0.0)),
                               lambda kb, h: kic_ref[0, rows(kb), :].astype(BF16), lambda h: kix_ref[...])
    mx, mn, n_adm = _fin(mx8, jnp.max), _fin(mn8, jnp.min), _fin(n8, jnp.sum)
    if n_kb % 2 == 1:
        sc_ref[rows(n_kb), :] = jnp.full((lb, qb), -INF, F32)
    n_units, unit = (n_kb + 1) // 2, 2 * lb
    thr, tie = _select_threshold(sc_ref, n_units, unit, qb, n_sel, mx, mn, n_adm, lane >= 0)
    has_ties = _any_set(tie == 1) > 0
    thr_valid = jnp.maximum(thr, -F32_MAX)

    def attention(with_ties):
        need, tri = _tie_setup(sc_ref, n_units, unit, lb, qb, n_sel, thr) if with_ties else (None, None)

        def score_block(kb, carry, get):
            m8, seen = carry
            s, seen = _selection_bias(sc_ref[rows(kb), :], thr, thr_valid, need, seen, tri, with_ties)
            for h in range(N_HEADS):
                s = s + jnp.dot(get(h), wq_ref[0, h * HEAD_DIM:(h + 1) * HEAD_DIM, :], preferred_element_type=F32)
            s_ref[rows(kb), :] = s
            return jnp.maximum(m8, _fold(s, jnp.max)), seen

        m8, _ = over_blocks(score_block, (full8(M_FLOOR), jnp.zeros((1, qb), F32)),
                            functools.partial(cache_head, kc_ref), lambda h: kx_ref[h])
        m = _fin(m8, jnp.max)
        acc_ref[...] = jnp.zeros(acc_ref.shape, F32)
        den_ref[...] = jnp.zeros(den_ref.shape, F32)
        ones = jnp.ones((lb, LANES), BF16)

        def value_block(kb, carry, get):
            pT = jnp.exp(s_ref[rows(kb), :] - m).T.astype(BF16)
            for h in range(N_HEADS):
                acc_ref[h] += jnp.dot(pT[h * t_new:(h + 1) * t_new, :], get(h), preferred_element_type=F32)
            den_ref[...] += jnp.dot(pT, ones, preferred_element_type=F32)
            return carry

        over_blocks(value_block, 0, functools.partial(cache_head, vc_ref), lambda h: vx_ref[h])
        for h in range(N_HEADS):
            o_ref[0, :, h * HEAD_DIM:(h + 1) * HEAD_DIM] = acc_ref[h] / den_ref[h * t_new:(h + 1) * t_new, 0:1]

    @pl.when(has_ties)
    def _():
        attention(True)

    @pl.when(jnp.logical_not(has_ties))
    def _():
        attention(False)


def _dsa_sample(q, qi, iw, k_cache, v_cache, ki_cache, layer, k_new, v_new, ki_new, *, lb, n_sel):
    b, t, _ = q.shape
    past = ki_cache.shape[1]
    assert N_HEADS * t == LANES and past % lb == 0 and t <= lb
    n_cache_kb = past // lb
    lp = (n_cache_kb + 1 + (n_cache_kb + 1) % 2) * lb
    eye = jnp.eye(N_HEADS, dtype=q.dtype)
    qh = jnp.transpose(q.reshape(b, t, N_HEADS, HEAD_DIM), (0, 2, 3, 1))
    wq = jnp.einsum("bhdq,gh->bgdhq", qh, eye).reshape(b, D_ATT, LANES)
    wqi = jnp.transpose(qi.reshape(b, t, N_IDX_HEADS, IDX_DIM), (0, 3, 2, 1)).reshape(b, IDX_DIM, LANES)
    iw_row = jnp.transpose(iw, (0, 2, 1)).reshape(b, 1, LANES)
    per_b = lambda shape: pl.BlockSpec((1,) + shape, lambda bi: (bi, 0, 0))
    cache = lambda shape: pl.BlockSpec((1,) + shape, lambda bi: (layer * b + bi, 0, 0))
    body = functools.partial(_dsa_sample_body, lb=lb, n_cache_kb=n_cache_kb, past=past, t_new=t, n_sel=n_sel)
    return pl.pallas_call(
        body,
        grid=(b,),
        in_specs=[per_b((D_ATT, LANES)), per_b((IDX_DIM, LANES)), per_b((1, LANES)),
                  cache((past * N_HEADS, HEAD_DIM)), cache((past * N_HEADS, HEAD_DIM)), cache((past, IDX_DIM)),
                  per_b((t, D_ATT)), per_b((t, D_ATT)), per_b((t, IDX_DIM))],
        out_specs=per_b((t, D_ATT)),
        out_shape=jax.ShapeDtypeStruct((b, t, D_ATT), F32),
        scratch_shapes=[
            pltpu.VMEM((lp, LANES), F32),
            pltpu.VMEM((lp, LANES), F32),
            pltpu.VMEM((N_HEADS, lb, HEAD_DIM), BF16),
            pltpu.VMEM((N_HEADS, lb, HEAD_DIM), BF16),
            pltpu.VMEM((lb, IDX_DIM), BF16),
            pltpu.VMEM((N_HEADS, t, HEAD_DIM), F32),
            pltpu.VMEM((LANES, LANES), F32),
        ],
        compiler_params=_cparams(("parallel",)),
        name="dsa_sample",
    )(wq, wqi, iw_row, k_cache, v_cache, ki_cache, k_new, v_new, ki_new)


def _tail_body(x_ref, o_ref, mc_ref, ga_ref, wao_ref, wout_ref, g_ref, w1_ref, w2_ref, y_ref,
               *, ff_chunk, o_transposed):
    o = o_ref[...].T if o_transposed else o_ref[...]
    attn = jnp.dot(o.astype(BF16), wao_ref[...], preferred_element_type=F32)
    merged = mc_ref[...] + ga_ref[...] * attn
    x1 = x_ref[...] + jnp.dot(merged.astype(BF16), wout_ref[...], preferred_element_type=F32)
    ms = jnp.mean(x1 * x1, axis=-1, keepdims=True)
    h = (x1 * lax.rsqrt(ms + EPS) * g_ref[...]).astype(BF16)
    y = x1
    for c in range(D_FF // ff_chunk):
        cs = slice(c * ff_chunk, (c + 1) * ff_chunk)
        u = jnp.maximum(jnp.dot(h, w1_ref[:, cs], preferred_element_type=F32), 0.0)
        y = y + jnp.dot((u * u).astype(BF16), w2_ref[cs, :], preferred_element_type=F32)
    y_ref[...] = y


def _tail(x, o, mc, ga, wao, wout, g, w1, w2, tm, o_transposed):
    n = x.shape[0]
    row = lambda width: pl.BlockSpec((tm, width), lambda i: (i, 0))
    full = lambda a: _resident(a.shape, lambda i: (0, 0))
    o_spec = pl.BlockSpec((D_ATT, tm), lambda i: (0, i)) if o_transposed else row(D_ATT)
    return pl.pallas_call(
        functools.partial(_tail_body, ff_chunk=1024, o_transposed=o_transposed),
        grid=(n // tm,),
        in_specs=[row(D_MODEL), o_spec, row(D_MODEL), row(D_MODEL),
                  full(wao), full(wout), pl.BlockSpec((1, D_MODEL), lambda i: (0, 0)), full(w1), full(w2)],
        out_specs=row(D_MODEL),
        out_shape=jax.ShapeDtypeStruct((n, D_MODEL), F32),
        compiler_params=_cparams(("parallel",)),
        name="tail",
    )(x, o, mc, ga, wao, wout, g, w1, w2)


def _rope_tables(pos):
    inv = ROPE_THETA ** (-jnp.arange(ROT_HALF, dtype=F32) / ROT_HALF)
    ang = pos.astype(F32)[:, None] * inv[None, :]
    cos, sin = jnp.cos(ang), jnp.sin(ang)
    r = np.arange(LANES) % HEAD_DIM
    jj = r % ROT_HALF
    first = jnp.asarray(r < ROT_HALF)[None, :]
    second = jnp.asarray((r >= ROT_HALF) & (r < ROT_DIM))[None, :]
    c = jnp.where(first | second, cos[:, jj], 1.0)
    sa = jnp.where(first, -sin[:, jj], 0.0)
    sb = jnp.where(second, sin[:, jj], 0.0)
    return c, sa, sb


def _pack_w_in(w):
    pad = jnp.zeros((D_MODEL, LANES - (D_IN_HEAD - COL_KIW)), w.dtype)
    return jnp.concatenate([w[:, :D_IN_HEAD], pad, w[:, D_IN_HEAD:]], axis=1).astype(BF16)


def _tile2(v):
    return jnp.concatenate([v, v])[None, :].astype(F32)


def _layer(x, tabs, n_tab_blocks, conv_state, caches, kv_all, layer, depth, w, *, tm, conv_tm, dsa_cfg):
    b, t, _ = x.shape
    n = b * t
    prompt = caches is None
    (norm_mix, w_in_p, conv_w, conv_b, ln_g, ln_b, w_conv_out, q_norm, k_norm,
     w_attn_out, w_out, norm_ffn, w_ff1, w_ff2) = w
    xf = x.reshape(n, D_MODEL)
    glu, q, qi, kiw, gc, ga, k5, v5, *extra = _inproj(
        xf, norm_mix[None, :], w_in_p, *tabs, _tile2(q_norm), _tile2(k_norm), kv_all, layer, depth,
        tm, n_tab_blocks, prompt)

    glu3 = glu.reshape(b, t, D_CONV)
    state_p = jnp.pad(conv_state, ((0, 0), (HALO - (CONV_W - 1), 0), (0, 0)))
    halo_src = glu3 if t >= HALO else state_p
    cw_p = jnp.pad(conv_w, ((0, HALO - CONV_W), (0, 0)))
    mc = _conv_branch(glu3, halo_src, state_p, cw_p, conv_b[None, :], ln_g[None, :], ln_b[None, :],
                      w_conv_out, gc.reshape(b, t, D_MODEL), conv_tm)
    new_conv = jnp.concatenate([conv_state, glu3], axis=1)[:, -(CONV_W - 1):]

    ki = kiw[:, :IDX_DIM]
    if prompt:
        kb, vt_ones, iwT = extra
        o = _dsa(q, qi, iwT, kb, vt_ones, ki.astype(BF16), b=b, **dsa_cfg)
    else:
        k, v = extra
        iw = kiw[:, IDX_DIM:IDX_DIM + N_IDX_HEADS].reshape(b, t, N_IDX_HEADS)
        o = _dsa_sample(q.reshape(b, t, D_ATT), qi.reshape(b, t, D_ATT), iw, *caches, layer,
                        k.reshape(b, t, D_ATT), v.reshape(b, t, D_ATT), ki.reshape(b, t, IDX_DIM),
                        **dsa_cfg).reshape(n, D_ATT)

    y = _tail(xf, o, mc.reshape(n, D_MODEL), ga, w_attn_out, w_out, norm_ffn[None, :], w_ff1, w_ff2, tm, prompt)
    return y.reshape(b, t, D_MODEL), (k5, v5), ki.reshape(b, t, IDX_DIM), new_conv


def kernel(x_prompt, x_sample, cache_k, cache_v, cache_kidx, state_conv, norm_mix, w_in, conv_w, conv_b,
           conv_ln_g, conv_ln_b, w_conv_out, q_norm, k_norm, w_attn_out, w_out, norm_ffn, w_ff1, w_ff2):
    bp, tp, _ = x_prompt.shape
    bs, ts, _ = x_sample.shape
    depth = norm_mix.shape[0]
    past = cache_k.shape[2]
    tm = 256
    tabs_p = _rope_tables(jnp.arange(tp, dtype=I32))
    tabs_s = tuple(jnp.tile(a, (bs, 1)) for a in _rope_tables(past + jnp.arange(ts, dtype=I32)))
    cfg_p = dict(qb=256, lb=256, n_sel=min(TOPK_MAX, tp // 4))
    cfg_s = dict(lb=256, n_sel=min(TOPK_MAX, (past + ts) // 4))
    caches = (cache_k.reshape(depth * bs, past * N_HEADS, HEAD_DIM),
              cache_v.reshape(depth * bs, past * N_HEADS, HEAD_DIM),
              cache_kidx.reshape(depth * bs, past, IDX_DIM))
    kv_rows = lambda b, t: jnp.zeros((depth * b * t * N_HEADS, HEAD_DIM), F32)
    kv_p = (kv_rows(bp, tp), kv_rows(bp, tp))
    kv_s = (kv_rows(bs, ts), kv_rows(bs, ts))
    hp, hs = x_prompt, x_sample
    ki_p, conv_p, ki_s, conv_s = [], [], [], []
    for l in range(depth):
        w = (norm_mix[l], _pack_w_in(w_in[l]), conv_w[l], conv_b[l], conv_ln_g[l], conv_ln_b[l],
             w_conv_out[l].astype(BF16), q_norm[l], k_norm[l], w_attn_out[l].astype(BF16),
             w_out[l].astype(BF16), norm_ffn[l], w_ff1[l].astype(BF16), w_ff2[l].astype(BF16))
        zero_state = jnp.zeros((bp, CONV_W - 1, D_CONV), F32)
        hp, kv_p, ki, conv = _layer(hp, tabs_p, tp // tm, zero_state, None, kv_p, l, depth, w,
                                    tm=tm, conv_tm=tm, dsa_cfg=cfg_p)
        ki_p.append(ki)
        conv_p.append(conv)
        hs, kv_s, ki, conv = _layer(hs, tabs_s, (bs * ts) // tm, state_conv[l], caches, kv_s, l, depth, w,
                                    tm=tm, conv_tm=ts, dsa_cfg=cfg_s)
        ki_s.append(ki)
        conv_s.append(conv)
    heads = lambda a, b, t: a.reshape(depth, b, t, N_HEADS, HEAD_DIM)
    return (hp, hs,
            heads(kv_p[0], bp, tp), heads(kv_p[1], bp, tp), jnp.stack(ki_p), jnp.stack(conv_p),
            heads(kv_s[0], bs, ts), heads(kv_s[1], bs, ts), jnp.stack(ki_s), jnp.stack(conv_s))
```

```python
import functools

import jax
import jax.numpy as jnp
import numpy as np
from jax import lax
from jax.experimental import pallas as pl
from jax.experimental.pallas import tpu as pltpu

F32 = jnp.float32
BF16 = jnp.bfloat16
I32 = jnp.int32

D_MODEL = 1024
CHUNK = 64
CHUNK_SHIFT = 6
N_HEADS = 8
HEAD_DIM = 64
ROT_DIM = HEAD_DIM // 4
ROT_HALF = ROT_DIM // 2
ROPE_THETA = 500000.0
N_IDX_HEADS = 8
IDX_DIM = 64
TOPK_MAX = 256
D_CONV = 512
CONV_W = 31
D_FF = 4 * D_MODEL
EPS = 1e-6
LN_EPS = 1e-5
ATTN_SCALE = HEAD_DIM ** -0.5
IDX_SCALE = (N_IDX_HEADS ** -0.5) * (IDX_DIM ** -0.5)
D_ATT = N_HEADS * HEAD_DIM

LANES = 128
SUBLANES = 8
ACC_ROWS = 4 * SUBLANES
VMEM_LIMIT_BYTES = 56 * 1024 * 1024

COL_KIW = 6 * 512
COL_GC = COL_KIW + LANES
COL_GA = COL_GC + D_MODEL
D_IN_PACKED = COL_GA + D_MODEL
D_IN_HEAD = 2 * D_CONV + 3 * D_ATT + N_IDX_HEADS * IDX_DIM + IDX_DIM + N_IDX_HEADS

HALO = 32
INF = float("inf")
F32_MAX = float(np.finfo(np.float32).max)
ONES_ROWS = 16
V_SLAB = HEAD_DIM + ONES_ROWS
M_FLOOR = -1e30
MASK_BIAS = -2e30
SEARCH_FAST_ITERS = 48
SEARCH_MAX_ITERS = 4096


def _cparams(sem):
    return pltpu.CompilerParams(dimension_semantics=sem, vmem_limit_bytes=VMEM_LIMIT_BYTES)


def _resident(block_shape, index_map):
    return pl.BlockSpec(block_shape, index_map, pipeline_mode=pl.Buffered(1))


def _rope(x, c, sa, sb):
    return x * c + pltpu.roll(x, LANES - ROT_HALF, 1) * sa + pltpu.roll(x, ROT_HALF, 1) * sb


def _head_rms(x, gain):
    lane = lax.broadcasted_iota(I32, x.shape, 1)
    lo = lane < HEAD_DIM
    x2 = x * x
    s_lo = jnp.sum(jnp.where(lo, x2, 0.0), axis=-1, keepdims=True)
    s_hi = jnp.sum(jnp.where(lo, 0.0, x2), axis=-1, keepdims=True)
    ms = jnp.where(lo, s_lo, s_hi) * (1.0 / HEAD_DIM)
    return x * lax.rsqrt(ms + EPS) * gain


def _inproj_body(*refs, transposed, n_aliased):
    x_ref, g_ref, w_ref, cos_ref, sa_ref, sb_ref, qg_ref, kg_ref = refs[:8]
    glu_ref, q_ref, qi_ref, gc_ref, ga_ref, kall_ref, vall_ref, kiall_ref, *extra = refs[8 + n_aliased:]
    tm = x_ref.shape[0]
    x = x_ref[...]
    ms = jnp.mean(x * x, axis=-1, keepdims=True)
    h = (x * lax.rsqrt(ms + EPS) * g_ref[...]).astype(BF16)

    def proj(c0, n):
        return jnp.dot(h, w_ref[:, c0:c0 + n], preferred_element_type=F32)

    glu_ref[...] = proj(0, D_CONV) * jax.nn.sigmoid(proj(D_CONV, D_CONV))

    c, sa, sb = cos_ref[...], sa_ref[...], sb_ref[...]
    zq = proj(2 * D_CONV, D_ATT)
    zk = proj(2 * D_CONV + D_ATT, D_ATT)
    zqi = proj(2 * D_CONV + 3 * D_ATT, N_IDX_HEADS * IDX_DIM)
    zv = proj(2 * D_CONV + 2 * D_ATT, D_ATT)
    for g in range(D_ATT // LANES):
        sl = slice(g * LANES, (g + 1) * LANES)
        qh = _rope(_head_rms(zq[:, sl], qg_ref[...]), c, sa, sb) * ATTN_SCALE
        kh = _rope(_head_rms(zk[:, sl], kg_ref[...]), c, sa, sb)
        qih = _rope(zqi[:, sl], c, sa, sb)
        if transposed:
            kb_ref, vt_ref = extra[0], extra[1]
            q_ref[sl, :] = qh.T.astype(BF16)
            qi_ref[sl, :] = qih.T.astype(BF16)
            kb_ref[:, sl] = kh.astype(BF16)
            kall_ref[sl, :] = kh.T
            vt32 = zv[:, sl].T
            vall_ref[sl, :] = vt32
            vt = vt32.astype(BF16)
            ones = jnp.ones((ONES_ROWS, tm), BF16)
            for half in range(2):
                r0 = (2 * g + half) * V_SLAB
                vt_ref[r0:r0 + HEAD_DIM, :] = vt[half * HEAD_DIM:(half + 1) * HEAD_DIM, :]
                vt_ref[r0 + HEAD_DIM:r0 + V_SLAB, :] = ones
        else:
            q_ref[:, sl] = qh.astype(BF16)
            qi_ref[:, sl] = qih.astype(BF16)
            extra[0][:, sl] = kh
            extra[1][:, sl] = zv[:, sl]
            for half in range(2):
                hs = slice(half * HEAD_DIM, (half + 1) * HEAD_DIM)
                head_rows = pl.ds(2 * g + half, tm, stride=N_HEADS)
                kall_ref[head_rows, :] = kh[:, hs]
                vall_ref[head_rows, :] = zv[:, sl][:, hs]

    zkiw = proj(COL_KIW, LANES)
    lane = lax.broadcasted_iota(I32, zkiw.shape, 1)
    is_ki = lane < IDX_DIM
    roped = _rope(zkiw, jnp.where(is_ki, c, 1.0), jnp.where(is_ki, sa, 0.0), jnp.where(is_ki, sb, 0.0))
    kiw = jnp.where(is_ki, roped, zkiw * IDX_SCALE)
    if transposed:
        kiwt = kiw.T
        kiall_ref[...] = kiwt[0:IDX_DIM, :]
        extra[2][...] = kiwt[IDX_DIM:IDX_DIM + N_IDX_HEADS, :]
        extra[3][...] = kiw[:, 0:IDX_DIM].astype(BF16)
    else:
        kiall_ref[...] = kiw[:, 0:IDX_DIM]
        extra[2][...] = kiw

    gc_ref[...] = jax.nn.sigmoid(proj(COL_GC, D_MODEL))
    ga_ref[...] = jax.nn.sigmoid(proj(COL_GA, D_MODEL))


def _kv_all_shapes(depth, b, t, transposed):
    if transposed:
        return ((depth * b * D_ATT, t), (depth * b * D_ATT, t), (depth * b * IDX_DIM, t))
    return ((depth * b * t * N_HEADS, HEAD_DIM), (depth * b * t * N_HEADS, HEAD_DIM), (depth * b * t, IDX_DIM))


def _inproj(x, g, w, cos, sa, sb, qg, kg, kv_all, layer, depth, b, tm, n_tab_blocks, transposed):
    n = x.shape[0]
    t = n // b
    nb = n // tm
    nt = t // tm if transposed else None
    row = lambda width: pl.BlockSpec((tm, width), lambda i: (i, 0))
    col = lambda height: pl.BlockSpec((height, tm), lambda i: (0, i))
    tab = pl.BlockSpec((tm, LANES), lambda i: (i % n_tab_blocks, 0))
    vec = lambda width: pl.BlockSpec((1, width), lambda i: (0, 0))
    rows_of = lambda width, dt: (jax.ShapeDtypeStruct((n, width), dt), row(width))
    cols_of = lambda height, dt: (jax.ShapeDtypeStruct((height, n), dt), col(height))
    shapes = _kv_all_shapes(depth, b, t, transposed)
    if transposed:
        slab = lambda height: pl.BlockSpec((height, tm), lambda i: (layer * b + i // nt, i % nt))
        all_specs = [slab(D_ATT), slab(D_ATT), slab(IDX_DIM)]
    else:
        rows5 = lambda r, width: pl.BlockSpec((r, width), lambda i: (layer * nb + i, 0))
        all_specs = [rows5(tm * N_HEADS, HEAD_DIM), rows5(tm * N_HEADS, HEAD_DIM), rows5(tm, IDX_DIM)]
    outs = [
        rows_of(D_CONV, F32),
        cols_of(D_ATT, BF16) if transposed else rows_of(D_ATT, BF16),
        cols_of(D_ATT, BF16) if transposed else rows_of(D_ATT, BF16),
        rows_of(D_MODEL, F32),
        rows_of(D_MODEL, F32),
    ] + [(jax.ShapeDtypeStruct(s, F32), spec) for s, spec in zip(shapes, all_specs)]
    n_common = len(outs) - len(shapes)
    if transposed:
        outs += [rows_of(D_ATT, BF16), cols_of(N_HEADS * V_SLAB, BF16), cols_of(N_IDX_HEADS, F32),
                 rows_of(IDX_DIM, BF16)]
    else:
        outs += [rows_of(D_ATT, F32), rows_of(D_ATT, F32), rows_of(LANES, F32)]
    aliased = tuple(kv_all)
    n_in = 8
    return pl.pallas_call(
        functools.partial(_inproj_body, transposed=transposed, n_aliased=len(aliased)),
        grid=(nb,),
        in_specs=[row(D_MODEL), vec(D_MODEL), _resident((D_MODEL, D_IN_PACKED), lambda i: (0, 0)),
                  tab, tab, tab, vec(LANES), vec(LANES)] + [pl.BlockSpec(memory_space=pl.ANY)] * len(aliased),
        out_specs=[o[1] for o in outs],
        out_shape=[o[0] for o in outs],
        input_output_aliases={n_in + a: n_common + a for a in range(len(aliased))},
        compiler_params=_cparams(("parallel",)),
        name="inproj",
    )(x, g, w, cos, sa, sb, qg, kg, *aliased)


def _conv_body(cur_ref, halo_ref, st_ref, cw_ref, cb_ref, lg_ref, lb_ref, wco_ref, gc_ref,
               out_ref, win_ref, act_ref, *, tm, rows):
    i = pl.program_id(1)
    win_ref[0:HALO, :] = jnp.where(i == 0, st_ref[0], halo_ref[0])
    win_ref[HALO:HALO + tm, :] = cur_ref[0]
    off = HALO - (CONV_W - 1)
    for r in range(tm // rows):
        acc = jnp.zeros((rows, D_CONV), F32) + cb_ref[...]
        for j in range(CONV_W):
            acc = acc + win_ref[r * rows + off + j:r * rows + off + j + rows, :] * cw_ref[j:j + 1, :]
        mu = jnp.mean(acc, axis=-1, keepdims=True)
        d = acc - mu
        var = jnp.mean(d * d, axis=-1, keepdims=True)
        y = d * lax.rsqrt(var + LN_EPS) * lg_ref[...] + lb_ref[...]
        act_ref[r * rows:(r + 1) * rows, :] = (y * jax.nn.sigmoid(y)).astype(BF16)
    out_ref[0] = gc_ref[0] * jnp.dot(act_ref[...], wco_ref[...], preferred_element_type=F32)


def _conv_branch(glu, halo_src, state, cw, cb, lg, lb, wco, gc, tm):
    b, t, _ = glu.shape
    rows = min(tm, 32)
    hb = tm // HALO
    vec = pl.BlockSpec((1, D_CONV), lambda bi, i: (0, 0))
    return pl.pallas_call(
        functools.partial(_conv_body, tm=tm, rows=rows),
        grid=(b, t // tm),
        in_specs=[
            pl.BlockSpec((1, tm, D_CONV), lambda bi, i: (bi, i, 0)),
            pl.BlockSpec((1, HALO, D_CONV), lambda bi, i: (bi, jnp.maximum(i * hb - 1, 0), 0)),
            pl.BlockSpec((1, HALO, D_CONV), lambda bi, i: (bi, 0, 0)),
            pl.BlockSpec((HALO, D_CONV), lambda bi, i: (0, 0)),
            vec, vec, vec,
            _resident((D_CONV, D_MODEL), lambda bi, i: (0, 0)),
            pl.BlockSpec((1, tm, D_MODEL), lambda bi, i: (bi, i, 0)),
        ],
        out_specs=pl.BlockSpec((1, tm, D_MODEL), lambda bi, i: (bi, i, 0)),
        out_shape=jax.ShapeDtypeStruct((b, t, D_MODEL), F32),
        scratch_shapes=[pltpu.VMEM((HALO + tm, D_CONV), F32), pltpu.VMEM((tm, D_CONV), BF16)],
        compiler_params=_cparams(("parallel", "arbitrary")),
        name="conv_branch",
    )(glu, halo_src, state, cw, cb, lg, lb, wco, gc)


def _fold(x, op):
    return op(x.reshape(x.shape[0] // ACC_ROWS, ACC_ROWS, x.shape[1]), axis=0)


def _fin(x8, op):
    return op(x8, axis=0, keepdims=True)


def _any_set(flag):
    return jnp.max(jnp.where(flag, 1, 0))


def _select_threshold(sc_ref, n_units, unit, qb, n_sel, mx, mn, n_adm, lane_ok):
    kf = float(n_sel)
    log_target = float(np.log(n_sel + 0.5))
    full8 = lambda v: jnp.full((ACC_ROWS, qb), v, F32)
    row = lambda v: jnp.full((1, qb), v, F32)

    def rows(u):
        return pl.ds(pl.multiple_of(u * unit, unit), unit)

    def count2(p):
        def blk(u, c):
            x = sc_ref[rows(u), :]
            return (c[0] + _fold(jnp.where(x >= p, 1.0, 0.0), jnp.sum),
                    c[1] + _fold(jnp.where(x > p, 1.0, 0.0), jnp.sum))
        ge8, gt8 = lax.fori_loop(0, n_units, blk, (full8(0.0), full8(0.0)))
        return _fin(ge8, jnp.sum), _fin(gt8, jnp.sum)

    def scan(p, lo, up, snap):
        def blk(u, c):
            x = sc_ref[rows(u), :]
            out = [c[0] + _fold(jnp.where(x >= p, 1.0, 0.0), jnp.sum)]
            if snap:
                out.append(jnp.minimum(c[1], _fold(jnp.where(x >= lo, x, INF), jnp.min)))
                out.append(jnp.maximum(c[2], _fold(jnp.where(x < up, x, -INF), jnp.max)))
            return tuple(out)
        init = (full8(0.0), full8(INF), full8(-INF)) if snap else (full8(0.0),)
        res = lax.fori_loop(0, n_units, blk, init)
        if snap:
            return _fin(res[0], jnp.sum), _fin(res[1], jnp.min), _fin(res[2], jnp.max)
        return _fin(res[0], jnp.sum)

    def pivot(lo, c_lo, up, c_up, wl, wu, bisect):
        xu = jnp.where(up == INF, mx, up)
        gl = (jnp.log(c_lo) - log_target) * wl
        gu = (log_target - jnp.log(jnp.maximum(c_up, 0.5))) * wu
        p = jnp.where(bisect, 0.5 * lo + 0.5 * xu, lo + (xu - lo) * (gl / (gl + gu)))
        p = jnp.minimum(p, xu)
        stuck = jnp.logical_not(p > lo)
        return jnp.where(stuck, xu, p), stuck

    def update(p, c, lo, c_lo, up, c_up, live):
        ge = c >= kf
        to_lo = jnp.logical_and(live, ge)
        to_up = jnp.logical_and(live, jnp.logical_not(ge))
        return (jnp.where(to_lo, p, lo), jnp.where(to_lo, c, c_lo),
                jnp.where(to_up, p, up), jnp.where(to_up, c, c_up), to_lo, to_up)

    def count_pass(st):
        it, _, _, lo, c_lo, up, c_up, wl, wu, side, done = st
        live = done == 0
        p, stuck = pivot(lo, c_lo, up, c_up, wl, wu, it % 8 == 7)
        c = scan(p, lo, up, False)
        lo, c_lo, up, c_up, to_lo, to_up = update(p, c, lo, c_lo, up, c_up, live)
        wl = jnp.where(jnp.logical_and(to_up, side < 0.0), wl * 0.5, jnp.where(to_lo, 1.0, wl))
        wu = jnp.where(jnp.logical_and(to_lo, side > 0.0), wu * 0.5, jnp.where(to_up, 1.0, wu))
        side = jnp.where(to_lo, 1.0, jnp.where(to_up, -1.0, side))
        done = jnp.where(c_lo == kf, 1, done)
        flags = jnp.sum(jnp.where(done == 0, 1, 0) + jnp.where(jnp.logical_and(stuck, live), 1 << 16, 0))
        return (it + 1, flags & 0xFFFF, flags >> 16, lo, c_lo, up, c_up, wl, wu, side, done)

    def snap_pass(st):
        it, _, lo, c_lo, up, c_up, done, tie = st
        live = done == 0
        p, _ = pivot(lo, c_lo, up, c_up, row(1.0), row(1.0), it % 2 == 1)
        c, a, b = scan(p, lo, up, True)
        tied = jnp.logical_and(live, a == b)
        lo2, c_lo2, up2, c_up2, _, _ = update(p, c, lo, c_lo, up, c_up,
                                              jnp.logical_and(live, jnp.logical_not(tied)))
        lo2 = jnp.where(live, jnp.maximum(lo2, a), lo2)
        tie = jnp.where(tied, 1, tie)
        done = jnp.where(jnp.logical_or(tied, c_lo2 == kf), 1, done)
        return (it + 1, _any_set(done == 0), lo2, c_lo2, up2, c_up2, done, tie)

    few = n_adm <= kf
    live0 = jnp.logical_and(jnp.logical_not(few), lane_ok)
    ge0, gt0 = count2(row(0.0))
    tie0 = jnp.logical_and(live0, jnp.logical_and(gt0 < kf, ge0 >= kf))
    above = ge0 >= kf
    lo0 = jnp.where(few, -INF, jnp.where(tie0, 0.0, jnp.where(above, jnp.maximum(mn, 0.0), mn)))
    c_lo0 = jnp.where(jnp.logical_and(above, mn < 0.0), ge0, n_adm)
    up0 = jnp.where(above, INF, 0.0)
    c_up0 = jnp.where(above, 0.0, ge0)
    done0 = jnp.where(jnp.logical_and(live0, jnp.logical_not(tie0)), 0, 1)
    done0 = jnp.where(c_lo0 == kf, 1, done0)
    tie_init = jnp.where(tie0, 1, 0)
    st = (jnp.int32(0), _any_set(done0 == 0), jnp.int32(0), lo0, c_lo0, up0, c_up0,
          row(1.0), row(1.0), row(0.0), done0)
    st = lax.while_loop(lambda s: (s[0] < SEARCH_FAST_ITERS) & (s[1] > 0) & (s[2] == 0), count_pass, st)
    it1, active1, _, lo1, c_lo1, up1, c_up1, _, _, _, done1 = st
    st = (it1, active1, lo1, c_lo1, up1, c_up1, done1, tie_init)
    st = lax.while_loop(lambda s: (s[0] < SEARCH_MAX_ITERS) & (s[1] > 0), snap_pass, st)
    return st[2], st[7]


def _selection_bias(x, thr, thr_valid, need, seen, tri, with_ties):
    if not with_ties:
        return jnp.where(x >= thr_valid, 0.0, MASK_BIAS), seen
    eq = x == thr
    eqf = jnp.where(eq, 1.0, 0.0)
    pref = jnp.dot(tri, eqf.astype(BF16), preferred_element_type=F32) + seen
    take = jnp.where(eq, jnp.where(pref <= need, 1.0, 0.0), jnp.where(x > thr, 1.0, 0.0))
    bias = jnp.where(jnp.where(x > -INF, take, 0.0) > 0.5, 0.0, MASK_BIAS)
    return bias, seen + jnp.sum(eqf, axis=0, keepdims=True)


def _tie_setup(sc_ref, n_units, unit, lb, qb, n_sel, thr):
    def gt_blk(u, c):
        x = sc_ref[pl.ds(pl.multiple_of(u * unit, unit), unit), :]
        return c + _fold(jnp.where(x > thr, 1.0, 0.0), jnp.sum)
    gt = _fin(lax.fori_loop(0, n_units, gt_blk, jnp.zeros((ACC_ROWS, qb), F32)), jnp.sum)
    ri = lax.broadcasted_iota(I32, (lb, lb), 0)
    ci = lax.broadcasted_iota(I32, (lb, lb), 1)
    return float(n_sel) - gt, jnp.where(ci <= ri, 1.0, 0.0).astype(BF16)


def _dsa_body(qT_ref, qiT_ref, iwT_ref, k_ref, vT_ref, ki_ref, oT_ref,
              sc_ref, qpad_ref, bias_ref, sa_ref, sb_ref, acc_ref, m_ref, alpha_ref, bm_ref, seen_ref,
              *, qb, lb, sub, n_sel):
    j = pl.program_id(1)
    n_kb = j + 1
    lane = lax.broadcasted_iota(I32, (1, qb), 1)
    qchunk = (j * qb + lane) >> CHUNK_SHIFT
    fold, fin = _fold, _fin

    def rows(kb):
        return pl.ds(pl.multiple_of(kb * lb, lb), lb)

    def idx_block(kb, carry):
        mx8, mn8, n8 = carry
        r0 = pl.multiple_of(kb * lb, lb)
        for s in range(lb // sub):
            kis = ki_ref[pl.ds(r0 + s * sub, sub), :]
            score = jnp.zeros((sub, qb), F32)
            for h in range(N_IDX_HEADS):
                sh = jnp.dot(kis, qiT_ref[h * IDX_DIM:(h + 1) * IDX_DIM, :], preferred_element_type=F32)
                score = score + jnp.maximum(sh, 0.0) * iwT_ref[h:h + 1, :]
            kpos = r0 + s * sub + lax.broadcasted_iota(I32, (sub, qb), 0)
            adm = (kpos >> CHUNK_SHIFT) <= qchunk
            sc_ref[pl.ds(r0 + s * sub, sub), :] = jnp.where(adm, score, -INF)
            mx8 = jnp.maximum(mx8, fold(jnp.where(adm, score, -INF), jnp.max))
            mn8 = jnp.minimum(mn8, fold(jnp.where(adm, score, INF), jnp.min))
            n8 = n8 + fold(jnp.where(adm, 1.0, 0.0), jnp.sum)
        return mx8, mn8, n8

    full8 = lambda v: jnp.full((ACC_ROWS, qb), v, F32)
    mx8, mn8, n8 = lax.fori_loop(0, n_kb, idx_block, (full8(-INF), full8(INF), full8(0.0)))
    mx, mn, n_adm = fin(mx8, jnp.max), fin(mn8, jnp.min), fin(n8, jnp.sum)

    @pl.when(n_kb % 2 == 1)
    def _():
        sc_ref[rows(n_kb), :] = jnp.full((lb, qb), -INF, F32)

    n_units, unit = (n_kb + 1) // 2, 2 * lb
    thr, tie = _select_threshold(sc_ref, n_units, unit, qb, n_sel, mx, mn, n_adm, lane >= 0)
    has_ties = _any_set(tie == 1) > 0
    thr_valid = jnp.maximum(thr, -F32_MAX)

    qpad_ref[...] = jnp.zeros(qpad_ref.shape, BF16)
    for h in range(N_HEADS):
        r = (h % 2) * HEAD_DIM
        qpad_ref[h, r:r + HEAD_DIM, :] = qT_ref[h * HEAD_DIM:(h + 1) * HEAD_DIM, :]

    def attention(with_ties):
        m_ref[...] = jnp.full(m_ref.shape, M_FLOOR, F32)
        acc_ref[...] = jnp.zeros(acc_ref.shape, F32)
        seen_ref[...] = jnp.zeros(seen_ref.shape, F32)
        need, tri = _tie_setup(sc_ref, n_units, unit, lb, qb, n_sel, thr) if with_ties else (None, None)

        def scores(kb, s_ref):
            bias_ref[...], seen_ref[...] = _selection_bias(sc_ref[rows(kb), :], thr, thr_valid, need,
                                                           seen_ref[...], tri, with_ties)
            for h in range(N_HEADS):
                p2 = (h // 2) * 2 * HEAD_DIM
                s = jnp.dot(k_ref[rows(kb), p2:p2 + 2 * HEAD_DIM], qpad_ref[h],
                            preferred_element_type=F32) + bias_ref[...]
                s_ref[h] = s
                bm_ref[h:h + 1, :] = jnp.max(s, axis=0, keepdims=True)
            m_old = m_ref[...]
            m_new = jnp.maximum(m_old, bm_ref[...])
            alpha_ref[...] = jnp.exp(m_old - m_new)
            m_ref[...] = m_new

        def values(kb, s_ref):
            m_new, alpha = m_ref[...], alpha_ref[...]
            for h in range(N_HEADS):
                p = jnp.exp(s_ref[h] - m_new[h:h + 1, :]).astype(BF16)
                vs = slice(h * V_SLAB, (h + 1) * V_SLAB)
                pv = jnp.dot(vT_ref[vs, rows(kb)], p, preferred_element_type=F32)
                acc_ref[vs, :] = acc_ref[vs, :] * alpha[h:h + 1, :] + pv

        def step(kb, s_prev, s_cur):
            values(kb - 1, s_prev)
            scores(kb, s_cur)

        scores(0, sa_ref)

        def body(kb, carry):
            @pl.when(kb % 2 == 1)
            def _():
                step(kb, sa_ref, sb_ref)

            @pl.when(kb % 2 == 0)
            def _():
                step(kb, sb_ref, sa_ref)
            return carry

        lax.fori_loop(1, n_kb, body, 0)

        @pl.when(n_kb % 2 == 1)
        def _():
            values(n_kb - 1, sa_ref)

        @pl.when(n_kb % 2 == 0)
        def _():
            values(n_kb - 1, sb_ref)

        for h in range(N_HEADS):
            num = acc_ref[h * V_SLAB:h * V_SLAB + HEAD_DIM, :]
            den = acc_ref[h * V_SLAB + HEAD_DIM:h * V_SLAB + HEAD_DIM + 1, :]
            oT_ref[h * HEAD_DIM:(h + 1) * HEAD_DIM, :] = num / den

    @pl.when(has_ties)
    def _():
        attention(True)

    @pl.when(jnp.logical_not(has_ties))
    def _():
        attention(False)


def _dsa(qT, qiT, iwT, k, vT, ki, *, b, qb, lb, n_sel):
    n = qT.shape[1]
    t = n // b
    nqb = t // qb
    assert t % (2 * lb) == 0 and qb == lb
    body = functools.partial(_dsa_body, qb=qb, lb=lb, sub=64, n_sel=n_sel)
    qcol = lambda height: pl.BlockSpec((height, qb), lambda bi, j: (0, bi * nqb + j))
    return pl.pallas_call(
        body,
        grid=(b, nqb),
        in_specs=[
            qcol(D_ATT), qcol(N_IDX_HEADS * IDX_DIM), qcol(N_IDX_HEADS),
            _resident((t, D_ATT), lambda bi, j: (bi, 0)),
            _resident((N_HEADS * V_SLAB, t), lambda bi, j: (0, bi)),
            _resident((t, IDX_DIM), lambda bi, j: (bi, 0)),
        ],
        out_specs=qcol(D_ATT),
        out_shape=jax.ShapeDtypeStruct((D_ATT, n), F32),
        scratch_shapes=[
            pltpu.VMEM((t, qb), F32),
            pltpu.VMEM((N_HEADS, 2 * HEAD_DIM, qb), BF16),
            pltpu.VMEM((lb, qb), F32),
            pltpu.VMEM((N_HEADS, lb, qb), F32),
            pltpu.VMEM((N_HEADS, lb, qb), F32),
            pltpu.VMEM((N_HEADS * V_SLAB, qb), F32),
            pltpu.VMEM((N_HEADS, qb), F32),
            pltpu.VMEM((N_HEADS, qb), F32),
            pltpu.VMEM((N_HEADS, qb), F32),
            pltpu.VMEM((1, qb), F32),
        ],
        compiler_params=_cparams(("parallel", "arbitrary")),
        name="dsa",
    )(qT, qiT, iwT, k, vT, ki)


def _dsa_sample_body(wq_ref, wqi_ref, iw_ref, kc_ref, vc_ref, kic_ref, kn_ref, vn_ref, kin_ref, o_ref,
                     sc_ref, s_ref, kx_ref, vx_ref, kix_ref, acc_ref,
                     *, lb, n_cache_kb, past, t_new, n_sel):
    qb = LANES
    n_kb = n_cache_kb + 1
    l_valid = past + t_new
    lane = lax.broadcasted_iota(I32, (1, qb), 1)
    qchunk = (past + (lane & (t_new - 1))) >> CHUNK_SHIFT
    full8 = lambda v: jnp.full((ACC_ROWS, qb), v, F32)

    def rows(kb):
        return pl.ds(kb * lb, lb) if isinstance(kb, int) else pl.ds(pl.multiple_of(kb * lb, lb), lb)

    def over_blocks(fn, carry, cache_ref, new_ref):
        def cached(kb, c):
            cols = pl.ds(pl.multiple_of(kb * lb, lb), lb)
            return fn(kb, c, cache_ref[0, :, cols].astype(BF16))
        return fn(n_cache_kb, lax.fori_loop(0, n_cache_kb, cached, carry), new_ref[...])

    for new_ref, stage_ref in ((kn_ref, kx_ref), (vn_ref, vx_ref), (kin_ref, kix_ref)):
        feat = stage_ref.shape[0]
        new = new_ref[0]
        if feat < LANES:
            new = jnp.concatenate([new, jnp.zeros((t_new, LANES - feat), F32)], axis=1)
        new = jnp.concatenate([new, jnp.zeros((LANES - t_new, new.shape[1]), F32)], axis=0)
        stage_ref[...] = jnp.zeros(stage_ref.shape, BF16)
        stage_ref[:, 0:LANES] = new.T[0:feat, :].astype(BF16)

    def idx_block(kb, carry, kit):
        mx8, mn8, n8 = carry
        r = jnp.maximum(jnp.dot(wqi_ref[0], kit, preferred_element_type=F32).T, 0.0) * iw_ref[0]
        for shift in (t_new, 2 * t_new, 4 * t_new):
            r = r + pltpu.roll(r, shift, 1)
        kpos = kb * lb + lax.broadcasted_iota(I32, (lb, qb), 0)
        adm = jnp.where(kpos < l_valid, kpos >> CHUNK_SHIFT, qchunk + 1) <= qchunk
        sc_ref[rows(kb), :] = jnp.where(adm, r, -INF)
        return (jnp.maximum(mx8, _fold(jnp.where(adm, r, -INF), jnp.max)),
                jnp.minimum(mn8, _fold(jnp.where(adm, r, INF), jnp.min)),
                n8 + _fold(jnp.where(adm, 1.0, 0.0), jnp.sum))

    mx8, mn8, n8 = over_blocks(idx_block, (full8(-INF), full8(INF), full8(0.0)), kic_ref, kix_ref)
    mx, mn, n_adm = _fin(mx8, jnp.max), _fin(mn8, jnp.min), _fin(n8, jnp.sum)
    if n_kb % 2 == 1:
        sc_ref[rows(n_kb), :] = jnp.full((lb, qb), -INF, F32)
    n_units, unit = (n_kb + 1) // 2, 2 * lb
    thr, tie = _select_threshold(sc_ref, n_units, unit, qb, n_sel, mx, mn, n_adm, lane >= 0)
    has_ties = _any_set(tie == 1) > 0
    thr_valid = jnp.maximum(thr, -F32_MAX)

    def attention(with_ties):
        need, tri = _tie_setup(sc_ref, n_units, unit, lb, qb, n_sel, thr) if with_ties else (None, None)

        def score_block(kb, carry, kt):
            m8, seen = carry
            bias, seen = _selection_bias(sc_ref[rows(kb), :], thr, thr_valid, need, seen, tri, with_ties)
            s = jnp.dot(wq_ref[0], kt, preferred_element_type=F32).T + bias
            s_ref[rows(kb), :] = s
            return jnp.maximum(m8, _fold(s, jnp.max)), seen

        m8, _ = over_blocks(score_block, (full8(M_FLOOR), jnp.zeros((1, qb), F32)), kc_ref, kx_ref)
        m = _fin(m8, jnp.max)
        acc_ref[...] = jnp.zeros(acc_ref.shape, F32)

        def value_block(kb, den8, vt):
            p = jnp.exp(s_ref[rows(kb), :] - m).astype(BF16)
            acc_ref[...] += jnp.dot(vt, p, preferred_element_type=F32)
            return den8 + _fold(p.astype(F32), jnp.sum)

        den = _fin(over_blocks(value_block, full8(0.0), vc_ref, vx_ref), jnp.sum)
        o_all = (acc_ref[...] / den).T
        for h in range(N_HEADS):
            rs = slice(h * t_new, (h + 1) * t_new)
            cs = slice(h * HEAD_DIM, (h + 1) * HEAD_DIM)
            o_ref[0, :, cs] = o_all[rs, cs]

    @pl.when(has_ties)
    def _():
        attention(True)

    @pl.when(jnp.logical_not(has_ties))
    def _():
        attention(False)


def _dsa_sample(q, qi, iw, k_cache, v_cache, ki_cache, layer, k_new, v_new, ki_new, *, lb, n_sel):
    b, t, _ = q.shape
    past = ki_cache.shape[2]
    assert N_HEADS * t == LANES and past % lb == 0 and LANES <= lb
    n_cache_kb = past // lb
    lp = (n_cache_kb + 1 + (n_cache_kb + 1) % 2) * lb
    eye = jnp.eye(N_HEADS, dtype=q.dtype)
    qh = q.reshape(b, t, N_HEADS, HEAD_DIM)
    wq = jnp.einsum("bqhd,gh->bhqgd", qh, eye).reshape(b, LANES, D_ATT)
    wqi = jnp.transpose(qi.reshape(b, t, N_IDX_HEADS, IDX_DIM), (0, 2, 1, 3)).reshape(b, LANES, IDX_DIM)
    iw_row = jnp.transpose(iw, (0, 2, 1)).reshape(b, 1, LANES)
    per_b = lambda shape: pl.BlockSpec((1,) + shape, lambda bi: (bi, 0, 0))
    cache = lambda shape: pl.BlockSpec((1,) + shape, lambda bi: (layer * b + bi, 0, 0))
    body = functools.partial(_dsa_sample_body, lb=lb, n_cache_kb=n_cache_kb, past=past, t_new=t, n_sel=n_sel)
    return pl.pallas_call(
        body,
        grid=(b,),
        in_specs=[per_b((LANES, D_ATT)), per_b((LANES, IDX_DIM)), per_b((1, LANES)),
                  cache((D_ATT, past)), cache((D_ATT, past)), cache((IDX_DIM, past)),
                  per_b((t, D_ATT)), per_b((t, D_ATT)), per_b((t, IDX_DIM))],
        out_specs=per_b((t, D_ATT)),
        out_shape=jax.ShapeDtypeStruct((b, t, D_ATT), F32),
        scratch_shapes=[
            pltpu.VMEM((lp, LANES), F32),
            pltpu.VMEM((lp, LANES), F32),
            pltpu.VMEM((D_ATT, lb), BF16),
            pltpu.VMEM((D_ATT, lb), BF16),
            pltpu.VMEM((IDX_DIM, lb), BF16),
            pltpu.VMEM((D_ATT, LANES), F32),
        ],
        compiler_params=_cparams(("parallel",)),
        name="dsa_sample",
    )(wq, wqi, iw_row, k_cache, v_cache, ki_cache, k_new, v_new, ki_new)


def _tail_body(x_ref, o_ref, mc_ref, ga_ref, wao_ref, wout_ref, g_ref, w1_ref, w2_ref, y_ref,
               *, ff_chunk, o_transposed):
    o = o_ref[...].T if o_transposed else o_ref[...]
    attn = jnp.dot(o.astype(BF16), wao_ref[...], preferred_element_type=F32)
    merged = mc_ref[...] + ga_ref[...] * attn
    x1 = x_ref[...] + jnp.dot(merged.astype(BF16), wout_ref[...], preferred_element_type=F32)
    ms = jnp.mean(x1 * x1, axis=-1, keepdims=True)
    h = (x1 * lax.rsqrt(ms + EPS) * g_ref[...]).astype(BF16)
    y = x1
    for c in range(D_FF // ff_chunk):
        cs = slice(c * ff_chunk, (c + 1) * ff_chunk)
        u = jnp.maximum(jnp.dot(h, w1_ref[:, cs], preferred_element_type=F32), 0.0)
        y = y + jnp.dot((u * u).astype(BF16), w2_ref[cs, :], preferred_element_type=F32)
    y_ref[...] = y


def _tail(x, o, mc, ga, wao, wout, g, w1, w2, tm, o_transposed):
    n = x.shape[0]
    row = lambda width: pl.BlockSpec((tm, width), lambda i: (i, 0))
    full = lambda a: _resident(a.shape, lambda i: (0, 0))
    o_spec = pl.BlockSpec((D_ATT, tm), lambda i: (0, i)) if o_transposed else row(D_ATT)
    return pl.pallas_call(
        functools.partial(_tail_body, ff_chunk=1024, o_transposed=o_transposed),
        grid=(n // tm,),
        in_specs=[row(D_MODEL), o_spec, row(D_MODEL), row(D_MODEL),
                  full(wao), full(wout), pl.BlockSpec((1, D_MODEL), lambda i: (0, 0)), full(w1), full(w2)],
        out_specs=row(D_MODEL),
        out_shape=jax.ShapeDtypeStruct((n, D_MODEL), F32),
        compiler_params=_cparams(("parallel",)),
        name="tail",
    )(x, o, mc, ga, wao, wout, g, w1, w2)


def _rope_tables(pos):
    inv = ROPE_THETA ** (-jnp.arange(ROT_HALF, dtype=F32) / ROT_HALF)
    ang = pos.astype(F32)[:, None] * inv[None, :]
    cos, sin = jnp.cos(ang), jnp.sin(ang)
    r = np.arange(LANES) % HEAD_DIM
    jj = r % ROT_HALF
    first = jnp.asarray(r < ROT_HALF)[None, :]
    second = jnp.asarray((r >= ROT_HALF) & (r < ROT_DIM))[None, :]
    c = jnp.where(first | second, cos[:, jj], 1.0)
    sa = jnp.where(first, -sin[:, jj], 0.0)
    sb = jnp.where(second, sin[:, jj], 0.0)
    return c, sa, sb


def _pack_w_in(w):
    pad = jnp.zeros((D_MODEL, LANES - (D_IN_HEAD - COL_KIW)), w.dtype)
    return jnp.concatenate([w[:, :D_IN_HEAD], pad, w[:, D_IN_HEAD:]], axis=1).astype(BF16)


def _tile2(v):
    return jnp.concatenate([v, v])[None, :].astype(F32)


def _layer(x, tabs, n_tab_blocks, conv_state, caches, kv_all, layer, depth, w, *, tm, conv_tm, dsa_cfg):
    b, t, _ = x.shape
    n = b * t
    prompt = caches is None
    (norm_mix, w_in_p, conv_w, conv_b, ln_g, ln_b, w_conv_out, q_norm, k_norm,
     w_attn_out, w_out, norm_ffn, w_ff1, w_ff2) = w
    xf = x.reshape(n, D_MODEL)
    glu, q, qi, gc, ga, k_all, v_all, ki_all, *extra = _inproj(
        xf, norm_mix[None, :], w_in_p, *tabs, _tile2(q_norm), _tile2(k_norm), kv_all, layer, depth, b,
        tm, n_tab_blocks, prompt)

    glu3 = glu.reshape(b, t, D_CONV)
    state_p = jnp.pad(conv_state, ((0, 0), (HALO - (CONV_W - 1), 0), (0, 0)))
    halo_src = glu3 if t >= HALO else state_p
    cw_p = jnp.pad(conv_w, ((0, HALO - CONV_W), (0, 0)))
    mc = _conv_branch(glu3, halo_src, state_p, cw_p, conv_b[None, :], ln_g[None, :], ln_b[None, :],
                      w_conv_out, gc.reshape(b, t, D_MODEL), conv_tm)
    new_conv = jnp.concatenate([conv_state, glu3], axis=1)[:, -(CONV_W - 1):]

    if prompt:
        kb, vt_ones, iwT, kib = extra
        o = _dsa(q, qi, iwT, kb, vt_ones, kib, b=b, **dsa_cfg)
    else:
        k, v, kiw = extra
        iw = kiw[:, IDX_DIM:IDX_DIM + N_IDX_HEADS].reshape(b, t, N_IDX_HEADS)
        o = _dsa_sample(q.reshape(b, t, D_ATT), qi.reshape(b, t, D_ATT), iw, *caches, layer,
                        k.reshape(b, t, D_ATT), v.reshape(b, t, D_ATT), kiw[:, :IDX_DIM].reshape(b, t, IDX_DIM),
                        **dsa_cfg).reshape(n, D_ATT)

    y = _tail(xf, o, mc.reshape(n, D_MODEL), ga, w_attn_out, w_out, norm_ffn[None, :], w_ff1, w_ff2, tm, prompt)
    return y.reshape(b, t, D_MODEL), (k_all, v_all, ki_all), new_conv


def kernel(x_prompt, x_sample, cache_k, cache_v, cache_kidx, state_conv, norm_mix, w_in, conv_w, conv_b,
           conv_ln_g, conv_ln_b, w_conv_out, q_norm, k_norm, w_attn_out, w_out, norm_ffn, w_ff1, w_ff2):
    bp, tp, _ = x_prompt.shape
    bs, ts, _ = x_sample.shape
    depth = norm_mix.shape[0]
    past = cache_k.shape[2]
    tm = 256
    tabs_p = _rope_tables(jnp.arange(tp, dtype=I32))
    tabs_s = tuple(jnp.tile(a, (bs, 1)) for a in _rope_tables(past + jnp.arange(ts, dtype=I32)))
    cfg_p = dict(qb=256, lb=256, n_sel=min(TOPK_MAX, tp // 4))
    cfg_s = dict(lb=256, n_sel=min(TOPK_MAX, (past + ts) // 4))
    caches = (jnp.transpose(cache_k, (0, 1, 3, 4, 2)).reshape(depth * bs, D_ATT, past),
              jnp.transpose(cache_v, (0, 1, 3, 4, 2)).reshape(depth * bs, D_ATT, past),
              jnp.transpose(cache_kidx, (0, 1, 3, 2)).reshape(depth * bs, IDX_DIM, past))
    kv_p = tuple(jnp.zeros(s, F32) for s in _kv_all_shapes(depth, bp, tp, True))
    kv_s = tuple(jnp.zeros(s, F32) for s in _kv_all_shapes(depth, bs, ts, False))
    hp, hs = x_prompt, x_sample
    conv_p, conv_s = [], []
    for l in range(depth):
        w = (norm_mix[l], _pack_w_in(w_in[l]), conv_w[l], conv_b[l], conv_ln_g[l], conv_ln_b[l],
             w_conv_out[l].astype(BF16), q_norm[l], k_norm[l], w_attn_out[l].astype(BF16),
             w_out[l].astype(BF16), norm_ffn[l], w_ff1[l].astype(BF16), w_ff2[l].astype(BF16))
        zero_state = jnp.zeros((bp, CONV_W - 1, D_CONV), F32)
        hp, kv_p, conv = _layer(hp, tabs_p, tp // tm, zero_state, None, kv_p, l, depth, w,
                                tm=tm, conv_tm=tm, dsa_cfg=cfg_p)
        conv_p.append(conv)
        hs, kv_s, conv = _layer(hs, tabs_s, (bs * ts) // tm, state_conv[l], caches, kv_s, l, depth, w,
                                tm=tm, conv_tm=ts, dsa_cfg=cfg_s)
        conv_s.append(conv)
    heads_p = lambda a: jnp.transpose(a.reshape(depth, bp, N_HEADS, HEAD_DIM, tp), (0, 1, 4, 2, 3))
    heads_s = lambda a: a.reshape(depth, bs, ts, N_HEADS, HEAD_DIM)
    return (hp, hs,
            heads_p(kv_p[0]), heads_p(kv_p[1]),
            jnp.transpose(kv_p[2].reshape(depth, bp, IDX_DIM, tp), (0, 1, 3, 2)), jnp.stack(conv_p),
            heads_s(kv_s[0]), heads_s(kv_s[1]), kv_s[2].reshape(depth, bs, ts, IDX_DIM), jnp.stack(conv_s))
```

```python
import functools

import jax
import jax.numpy as jnp
import numpy as np
from jax import lax
from jax.experimental import pallas as pl
from jax.experimental.pallas import tpu as pltpu

F32 = jnp.float32
BF16 = jnp.bfloat16
I32 = jnp.int32

D_MODEL = 1024
CHUNK = 64
CHUNK_SHIFT = 6
N_HEADS = 8
HEAD_DIM = 64
ROT_DIM = HEAD_DIM // 4
ROT_HALF = ROT_DIM // 2
ROPE_THETA = 500000.0
N_IDX_HEADS = 8
IDX_DIM = 64
TOPK_MAX = 256
D_CONV = 512
CONV_W = 31
D_FF = 4 * D_MODEL
EPS = 1e-6
LN_EPS = 1e-5
ATTN_SCALE = HEAD_DIM ** -0.5
IDX_SCALE = (N_IDX_HEADS ** -0.5) * (IDX_DIM ** -0.5)
D_ATT = N_HEADS * HEAD_DIM

LANES = 128
SUBLANES = 8
ACC_ROWS = 4 * SUBLANES
VMEM_LIMIT_BYTES = 56 * 1024 * 1024

COL_KIW = 6 * 512
COL_GC = COL_KIW + LANES
COL_GA = COL_GC + D_MODEL
D_IN_PACKED = COL_GA + D_MODEL
D_IN_HEAD = 2 * D_CONV + 3 * D_ATT + N_IDX_HEADS * IDX_DIM + IDX_DIM + N_IDX_HEADS

HALO = 32
INF = float("inf")
F32_MAX = float(np.finfo(np.float32).max)
ONES_ROWS = 16
V_SLAB = HEAD_DIM + ONES_ROWS
M_FLOOR = -1e30
MASK_BIAS = -2e30
SEARCH_FAST_ITERS = 48
SEARCH_MAX_ITERS = 4096


def _cparams(sem):
    return pltpu.CompilerParams(dimension_semantics=sem, vmem_limit_bytes=VMEM_LIMIT_BYTES)


def _resident(block_shape, index_map):
    return pl.BlockSpec(block_shape, index_map, pipeline_mode=pl.Buffered(1))


def _rope(x, c, sa, sb):
    return x * c + pltpu.roll(x, LANES - ROT_HALF, 1) * sa + pltpu.roll(x, ROT_HALF, 1) * sb


def _head_rms(x, gain):
    lane = lax.broadcasted_iota(I32, x.shape, 1)
    lo = lane < HEAD_DIM
    x2 = x * x
    s_lo = jnp.sum(jnp.where(lo, x2, 0.0), axis=-1, keepdims=True)
    s_hi = jnp.sum(jnp.where(lo, 0.0, x2), axis=-1, keepdims=True)
    ms = jnp.where(lo, s_lo, s_hi) * (1.0 / HEAD_DIM)
    return x * lax.rsqrt(ms + EPS) * gain


def _inproj_body(*refs, transposed, n_aliased):
    x_ref, g_ref, w_ref, cos_ref, sa_ref, sb_ref, qg_ref, kg_ref = refs[:8]
    glu_ref, q_ref, qi_ref, gc_ref, ga_ref, kall_ref, vall_ref, kiall_ref, *extra = refs[8 + n_aliased:]
    tm = x_ref.shape[0]
    x = x_ref[...]
    ms = jnp.mean(x * x, axis=-1, keepdims=True)
    h = (x * lax.rsqrt(ms + EPS) * g_ref[...]).astype(BF16)

    def proj(c0, n):
        return jnp.dot(h, w_ref[:, c0:c0 + n], preferred_element_type=F32)

    glu_ref[...] = proj(0, D_CONV) * jax.nn.sigmoid(proj(D_CONV, D_CONV))

    c, sa, sb = cos_ref[...], sa_ref[...], sb_ref[...]
    zq = proj(2 * D_CONV, D_ATT)
    zk = proj(2 * D_CONV + D_ATT, D_ATT)
    zqi = proj(2 * D_CONV + 3 * D_ATT, N_IDX_HEADS * IDX_DIM)
    zv = proj(2 * D_CONV + 2 * D_ATT, D_ATT)
    for g in range(D_ATT // LANES):
        sl = slice(g * LANES, (g + 1) * LANES)
        qh = _rope(_head_rms(zq[:, sl], qg_ref[...]), c, sa, sb) * ATTN_SCALE
        kh = _rope(_head_rms(zk[:, sl], kg_ref[...]), c, sa, sb)
        qih = _rope(zqi[:, sl], c, sa, sb)
        if transposed:
            kb_ref, vt_ref = extra[0], extra[1]
            q_ref[sl, :] = qh.T.astype(BF16)
            qi_ref[sl, :] = qih.T.astype(BF16)
            kb_ref[:, sl] = kh.astype(BF16)
            kall_ref[sl, :] = kh.T
            vt32 = zv[:, sl].T
            vall_ref[sl, :] = vt32
            vt = vt32.astype(BF16)
            ones = jnp.ones((ONES_ROWS, tm), BF16)
            for half in range(2):
                r0 = (2 * g + half) * V_SLAB
                vt_ref[r0:r0 + HEAD_DIM, :] = vt[half * HEAD_DIM:(half + 1) * HEAD_DIM, :]
                vt_ref[r0 + HEAD_DIM:r0 + V_SLAB, :] = ones
        else:
            q_ref[:, sl] = qh.astype(BF16)
            qi_ref[:, sl] = qih.astype(BF16)
            extra[0][:, sl] = kh
            extra[1][:, sl] = zv[:, sl]
            for half in range(2):
                hs = slice(half * HEAD_DIM, (half + 1) * HEAD_DIM)
                head_rows = pl.ds(2 * g + half, tm, stride=N_HEADS)
                kall_ref[head_rows, :] = kh[:, hs]
                vall_ref[head_rows, :] = zv[:, sl][:, hs]

    zkiw = proj(COL_KIW, LANES)
    lane = lax.broadcasted_iota(I32, zkiw.shape, 1)
    is_ki = lane < IDX_DIM
    roped = _rope(zkiw, jnp.where(is_ki, c, 1.0), jnp.where(is_ki, sa, 0.0), jnp.where(is_ki, sb, 0.0))
    kiw = jnp.where(is_ki, roped, zkiw * IDX_SCALE)
    if transposed:
        kiwt = kiw.T
        kiall_ref[...] = kiwt[0:IDX_DIM, :]
        extra[2][...] = kiwt[IDX_DIM:IDX_DIM + N_IDX_HEADS, :]
        extra[3][...] = kiw[:, 0:IDX_DIM].astype(BF16)
    else:
        kiall_ref[...] = kiw[:, 0:IDX_DIM]
        extra[2][...] = kiw

    gc_ref[...] = jax.nn.sigmoid(proj(COL_GC, D_MODEL))
    ga_ref[...] = jax.nn.sigmoid(proj(COL_GA, D_MODEL))


def _kv_all_shapes(depth, b, t, transposed):
    if transposed:
        return ((depth * b * D_ATT, t), (depth * b * D_ATT, t), (depth * b * IDX_DIM, t))
    return ((depth * b * t * N_HEADS, HEAD_DIM), (depth * b * t * N_HEADS, HEAD_DIM), (depth * b * t, IDX_DIM))


def _inproj(x, g, w, cos, sa, sb, qg, kg, kv_all, layer, depth, b, tm, n_tab_blocks, transposed):
    n = x.shape[0]
    t = n // b
    nb = n // tm
    nt = t // tm if transposed else None
    row = lambda width: pl.BlockSpec((tm, width), lambda i: (i, 0))
    col = lambda height: pl.BlockSpec((height, tm), lambda i: (0, i))
    tab = pl.BlockSpec((tm, LANES), lambda i: (i % n_tab_blocks, 0))
    vec = lambda width: pl.BlockSpec((1, width), lambda i: (0, 0))
    rows_of = lambda width, dt: (jax.ShapeDtypeStruct((n, width), dt), row(width))
    cols_of = lambda height, dt: (jax.ShapeDtypeStruct((height, n), dt), col(height))
    shapes = _kv_all_shapes(depth, b, t, transposed)
    if transposed:
        slab = lambda height: pl.BlockSpec((height, tm), lambda i: (layer * b + i // nt, i % nt))
        all_specs = [slab(D_ATT), slab(D_ATT), slab(IDX_DIM)]
    else:
        rows5 = lambda r, width: pl.BlockSpec((r, width), lambda i: (layer * nb + i, 0))
        all_specs = [rows5(tm * N_HEADS, HEAD_DIM), rows5(tm * N_HEADS, HEAD_DIM), rows5(tm, IDX_DIM)]
    outs = [
        rows_of(D_CONV, F32),
        cols_of(D_ATT, BF16) if transposed else rows_of(D_ATT, BF16),
        cols_of(D_ATT, BF16) if transposed else rows_of(D_ATT, BF16),
        rows_of(D_MODEL, F32),
        rows_of(D_MODEL, F32),
    ] + [(jax.ShapeDtypeStruct(s, F32), spec) for s, spec in zip(shapes, all_specs)]
    n_common = len(outs) - len(shapes)
    if transposed:
        outs += [rows_of(D_ATT, BF16), cols_of(N_HEADS * V_SLAB, BF16), cols_of(N_IDX_HEADS, F32),
                 rows_of(IDX_DIM, BF16)]
    else:
        outs += [rows_of(D_ATT, F32), rows_of(D_ATT, F32), rows_of(LANES, F32)]
    aliased = tuple(kv_all)
    n_in = 8
    return pl.pallas_call(
        functools.partial(_inproj_body, transposed=transposed, n_aliased=len(aliased)),
        grid=(nb,),
        in_specs=[row(D_MODEL), vec(D_MODEL), _resident((D_MODEL, D_IN_PACKED), lambda i: (0, 0)),
                  tab, tab, tab, vec(LANES), vec(LANES)] + [pl.BlockSpec(memory_space=pl.ANY)] * len(aliased),
        out_specs=[o[1] for o in outs],
        out_shape=[o[0] for o in outs],
        input_output_aliases={n_in + a: n_common + a for a in range(len(aliased))},
        compiler_params=_cparams(("parallel",)),
        name="inproj",
    )(x, g, w, cos, sa, sb, qg, kg, *aliased)


def _conv_body(cur_ref, halo_ref, st_ref, cw_ref, cb_ref, lg_ref, lb_ref, wco_ref, gc_ref,
               out_ref, win_ref, sh_ref, act_ref, *, tm, rows):
    i = pl.program_id(1)
    win_ref[0:HALO, :] = jnp.where(i == 0, st_ref[0], halo_ref[0])
    win_ref[HALO:HALO + tm, :] = cur_ref[0]
    span = HALO + tm - SUBLANES
    for s in range(1, SUBLANES):
        sh_ref[s, 0:span, :] = win_ref[s:s + span, :]
    off = HALO - (CONV_W - 1)
    for r in range(tm // rows):
        acc = jnp.zeros((rows, D_CONV), F32) + cb_ref[...]
        for j in range(CONV_W):
            s, base = (off + j) % SUBLANES, r * rows + (off + j) // SUBLANES * SUBLANES
            tap = win_ref[base:base + rows, :] if s == 0 else sh_ref[s, base:base + rows, :]
            acc = acc + tap * cw_ref[j:j + 1, :]
        mu = jnp.mean(acc, axis=-1, keepdims=True)
        d = acc - mu
        var = jnp.mean(d * d, axis=-1, keepdims=True)
        y = d * lax.rsqrt(var + LN_EPS) * lg_ref[...] + lb_ref[...]
        act_ref[r * rows:(r + 1) * rows, :] = (y * jax.nn.sigmoid(y)).astype(BF16)
    out_ref[0] = gc_ref[0] * jnp.dot(act_ref[...], wco_ref[...], preferred_element_type=F32)


def _conv_branch(glu, halo_src, state, cw, cb, lg, lb, wco, gc, tm):
    b, t, _ = glu.shape
    rows = min(tm, 32)
    hb = tm // HALO
    vec = pl.BlockSpec((1, D_CONV), lambda bi, i: (0, 0))
    return pl.pallas_call(
        functools.partial(_conv_body, tm=tm, rows=rows),
        grid=(b, t // tm),
        in_specs=[
            pl.BlockSpec((1, tm, D_CONV), lambda bi, i: (bi, i, 0)),
            pl.BlockSpec((1, HALO, D_CONV), lambda bi, i: (bi, jnp.maximum(i * hb - 1, 0), 0)),
            pl.BlockSpec((1, HALO, D_CONV), lambda bi, i: (bi, 0, 0)),
            pl.BlockSpec((HALO, D_CONV), lambda bi, i: (0, 0)),
            vec, vec, vec,
            _resident((D_CONV, D_MODEL), lambda bi, i: (0, 0)),
            pl.BlockSpec((1, tm, D_MODEL), lambda bi, i: (bi, i, 0)),
        ],
        out_specs=pl.BlockSpec((1, tm, D_MODEL), lambda bi, i: (bi, i, 0)),
        out_shape=jax.ShapeDtypeStruct((b, t, D_MODEL), F32),
        scratch_shapes=[pltpu.VMEM((HALO + tm, D_CONV), F32), pltpu.VMEM((SUBLANES, HALO + tm, D_CONV), F32),
                        pltpu.VMEM((tm, D_CONV), BF16)],
        compiler_params=_cparams(("parallel", "arbitrary")),
        name="conv_branch",
    )(glu, halo_src, state, cw, cb, lg, lb, wco, gc)


def _fold(x, op):
    return op(x.reshape(x.shape[0] // ACC_ROWS, ACC_ROWS, x.shape[1]), axis=0)


def _fin(x8, op):
    return op(x8, axis=0, keepdims=True)


def _any_set(flag):
    return jnp.max(jnp.where(flag, 1, 0))


def _select_threshold(sc_ref, n_units, unit, qb, n_sel, mx, mn, n_adm, lane_ok):
    kf = float(n_sel)
    log_target = float(np.log(n_sel + 0.5))
    full8 = lambda v: jnp.full((ACC_ROWS, qb), v, F32)
    row = lambda v: jnp.full((1, qb), v, F32)

    def rows(u):
        return pl.ds(pl.multiple_of(u * unit, unit), unit)

    def count2(p):
        def blk(u, c):
            x = sc_ref[rows(u), :]
            return (c[0] + _fold(jnp.where(x >= p, 1.0, 0.0), jnp.sum),
                    c[1] + _fold(jnp.where(x > p, 1.0, 0.0), jnp.sum))
        ge8, gt8 = lax.fori_loop(0, n_units, blk, (full8(0.0), full8(0.0)))
        return _fin(ge8, jnp.sum), _fin(gt8, jnp.sum)

    def scan(p, lo, up, snap):
        def blk(u, c):
            x = sc_ref[rows(u), :]
            out = [c[0] + _fold(jnp.where(x >= p, 1.0, 0.0), jnp.sum)]
            if snap:
                out.append(jnp.minimum(c[1], _fold(jnp.where(x >= lo, x, INF), jnp.min)))
                out.append(jnp.maximum(c[2], _fold(jnp.where(x < up, x, -INF), jnp.max)))
            return tuple(out)
        init = (full8(0.0), full8(INF), full8(-INF)) if snap else (full8(0.0),)
        res = lax.fori_loop(0, n_units, blk, init)
        if snap:
            return _fin(res[0], jnp.sum), _fin(res[1], jnp.min), _fin(res[2], jnp.max)
        return _fin(res[0], jnp.sum)

    def pivot(lo, c_lo, up, c_up, wl, wu, bisect):
        xu = jnp.where(up == INF, mx, up)
        gl = (jnp.log(c_lo) - log_target) * wl
        gu = (log_target - jnp.log(jnp.maximum(c_up, 0.5))) * wu
        p = jnp.where(bisect, 0.5 * lo + 0.5 * xu, lo + (xu - lo) * (gl / (gl + gu)))
        p = jnp.minimum(p, xu)
        stuck = jnp.logical_not(p > lo)
        return jnp.where(stuck, xu, p), stuck

    def update(p, c, lo, c_lo, up, c_up, live):
        ge = c >= kf
        to_lo = jnp.logical_and(live, ge)
        to_up = jnp.logical_and(live, jnp.logical_not(ge))
        return (jnp.where(to_lo, p, lo), jnp.where(to_lo, c, c_lo),
                jnp.where(to_up, p, up), jnp.where(to_up, c, c_up), to_lo, to_up)

    def count_pass(st):
        it, _, _, lo, c_lo, up, c_up, wl, wu, side, done = st
        live = done == 0
        p, stuck = pivot(lo, c_lo, up, c_up, wl, wu, it % 8 == 7)
        c = scan(p, lo, up, False)
        lo, c_lo, up, c_up, to_lo, to_up = update(p, c, lo, c_lo, up, c_up, live)
        wl = jnp.where(jnp.logical_and(to_up, side < 0.0), wl * 0.5, jnp.where(to_lo, 1.0, wl))
        wu = jnp.where(jnp.logical_and(to_lo, side > 0.0), wu * 0.5, jnp.where(to_up, 1.0, wu))
        side = jnp.where(to_lo, 1.0, jnp.where(to_up, -1.0, side))
        done = jnp.where(c_lo == kf, 1, done)
        flags = jnp.sum(jnp.where(done == 0, 1, 0) + jnp.where(jnp.logical_and(stuck, live), 1 << 16, 0))
        return (it + 1, flags & 0xFFFF, flags >> 16, lo, c_lo, up, c_up, wl, wu, side, done)

    def snap_pass(st):
        it, _, lo, c_lo, up, c_up, done, tie = st
        live = done == 0
        p, _ = pivot(lo, c_lo, up, c_up, row(1.0), row(1.0), it % 2 == 1)
        c, a, b = scan(p, lo, up, True)
        tied = jnp.logical_and(live, a == b)
        lo2, c_lo2, up2, c_up2, _, _ = update(p, c, lo, c_lo, up, c_up,
                                              jnp.logical_and(live, jnp.logical_not(tied)))
        lo2 = jnp.where(live, jnp.maximum(lo2, a), lo2)
        tie = jnp.where(tied, 1, tie)
        done = jnp.where(jnp.logical_or(tied, c_lo2 == kf), 1, done)
        return (it + 1, _any_set(done == 0), lo2, c_lo2, up2, c_up2, done, tie)

    few = n_adm <= kf
    live0 = jnp.logical_and(jnp.logical_not(few), lane_ok)
    ge0, gt0 = count2(row(0.0))
    tie0 = jnp.logical_and(live0, jnp.logical_and(gt0 < kf, ge0 >= kf))
    above = ge0 >= kf
    lo0 = jnp.where(few, -INF, jnp.where(tie0, 0.0, jnp.where(above, jnp.maximum(mn, 0.0), mn)))
    c_lo0 = jnp.where(jnp.logical_and(above, mn < 0.0), ge0, n_adm)
    up0 = jnp.where(above, INF, 0.0)
    c_up0 = jnp.where(above, 0.0, ge0)
    done0 = jnp.where(jnp.logical_and(live0, jnp.logical_not(tie0)), 0, 1)
    done0 = jnp.where(c_lo0 == kf, 1, done0)
    tie_init = jnp.where(tie0, 1, 0)
    st = (jnp.int32(0), _any_set(done0 == 0), jnp.int32(0), lo0, c_lo0, up0, c_up0,
          row(1.0), row(1.0), row(0.0), done0)
    st = lax.while_loop(lambda s: (s[0] < SEARCH_FAST_ITERS) & (s[1] > 0) & (s[2] == 0), count_pass, st)
    it1, active1, _, lo1, c_lo1, up1, c_up1, _, _, _, done1 = st
    st = (it1, active1, lo1, c_lo1, up1, c_up1, done1, tie_init)
    st = lax.while_loop(lambda s: (s[0] < SEARCH_MAX_ITERS) & (s[1] > 0), snap_pass, st)
    return st[2], st[7]


def _selection_bias(x, thr, thr_valid, need, seen, tri, with_ties):
    if not with_ties:
        return jnp.where(x >= thr_valid, 0.0, MASK_BIAS), seen
    eq = x == thr
    eqf = jnp.where(eq, 1.0, 0.0)
    pref = jnp.dot(tri, eqf.astype(BF16), preferred_element_type=F32) + seen
    take = jnp.where(eq, jnp.where(pref <= need, 1.0, 0.0), jnp.where(x > thr, 1.0, 0.0))
    bias = jnp.where(jnp.where(x > -INF, take, 0.0) > 0.5, 0.0, MASK_BIAS)
    return bias, seen + jnp.sum(eqf, axis=0, keepdims=True)


def _tie_setup(sc_ref, n_units, unit, lb, qb, n_sel, thr):
    def gt_blk(u, c):
        x = sc_ref[pl.ds(pl.multiple_of(u * unit, unit), unit), :]
        return c + _fold(jnp.where(x > thr, 1.0, 0.0), jnp.sum)
    gt = _fin(lax.fori_loop(0, n_units, gt_blk, jnp.zeros((ACC_ROWS, qb), F32)), jnp.sum)
    ri = lax.broadcasted_iota(I32, (lb, lb), 0)
    ci = lax.broadcasted_iota(I32, (lb, lb), 1)
    return float(n_sel) - gt, jnp.where(ci <= ri, 1.0, 0.0).astype(BF16)


def _dsa_body(qT_ref, qiT_ref, iwT_ref, k_ref, vT_ref, ki_ref, oT_ref,
              sc_ref, qpad_ref, bias_ref, sa_ref, sb_ref, acc_ref, m_ref, alpha_ref, bm_ref, seen_ref,
              *, qb, lb, sub, n_sel):
    j = pl.program_id(1)
    n_kb = j + 1
    lane = lax.broadcasted_iota(I32, (1, qb), 1)
    qchunk = (j * qb + lane) >> CHUNK_SHIFT
    fold, fin = _fold, _fin

    def rows(kb):
        return pl.ds(pl.multiple_of(kb * lb, lb), lb)

    def idx_block(kb, carry, diagonal):
        mx8, mn8 = carry
        r0 = pl.multiple_of(kb * lb, lb)
        for s in range(lb // sub):
            kis = ki_ref[pl.ds(r0 + s * sub, sub), :]
            score = jnp.zeros((sub, qb), F32)
            for h in range(N_IDX_HEADS):
                sh = jnp.dot(kis, qiT_ref[h * IDX_DIM:(h + 1) * IDX_DIM, :], preferred_element_type=F32)
                score = score + jnp.maximum(sh, 0.0) * iwT_ref[h:h + 1, :]
            lo_s = hi_s = score
            if diagonal:
                kpos = r0 + s * sub + lax.broadcasted_iota(I32, (sub, qb), 0)
                adm = (kpos >> CHUNK_SHIFT) <= qchunk
                hi_s, lo_s = jnp.where(adm, score, -INF), jnp.where(adm, score, INF)
            sc_ref[pl.ds(r0 + s * sub, sub), :] = hi_s
            mx8 = jnp.maximum(mx8, fold(hi_s, jnp.max))
            mn8 = jnp.minimum(mn8, fold(lo_s, jnp.min))
        return mx8, mn8

    full8 = lambda v: jnp.full((ACC_ROWS, qb), v, F32)
    mx8, mn8 = lax.fori_loop(0, j, lambda kb, c: idx_block(kb, c, False), (full8(-INF), full8(INF)))
    mx8, mn8 = idx_block(j, (mx8, mn8), True)
    mx, mn = fin(mx8, jnp.max), fin(mn8, jnp.min)
    n_adm = ((qchunk + 1) * CHUNK).astype(F32)

    n_units, unit = n_kb, lb
    thr, tie = _select_threshold(sc_ref, n_units, unit, qb, n_sel, mx, mn, n_adm, lane >= 0)
    has_ties = _any_set(tie == 1) > 0
    thr_valid = jnp.maximum(thr, -F32_MAX)

    qpad_ref[...] = jnp.zeros(qpad_ref.shape, BF16)
    for h in range(N_HEADS):
        r = (h % 2) * HEAD_DIM
        qpad_ref[h, r:r + HEAD_DIM, :] = qT_ref[h * HEAD_DIM:(h + 1) * HEAD_DIM, :]

    def attention(with_ties):
        m_ref[...] = jnp.full(m_ref.shape, M_FLOOR, F32)
        acc_ref[...] = jnp.zeros(acc_ref.shape, F32)
        seen_ref[...] = jnp.zeros(seen_ref.shape, F32)
        need, tri = _tie_setup(sc_ref, n_units, unit, lb, qb, n_sel, thr) if with_ties else (None, None)

        def scores(kb, s_ref):
            bias_ref[...], seen_ref[...] = _selection_bias(sc_ref[rows(kb), :], thr, thr_valid, need,
                                                           seen_ref[...], tri, with_ties)
            for h in range(N_HEADS):
                p2 = (h // 2) * 2 * HEAD_DIM
                s = jnp.dot(k_ref[rows(kb), p2:p2 + 2 * HEAD_DIM], qpad_ref[h],
                            preferred_element_type=F32) + bias_ref[...]
                s_ref[h] = s
                bm_ref[h:h + 1, :] = jnp.max(s, axis=0, keepdims=True)
            m_old = m_ref[...]
            m_new = jnp.maximum(m_old, bm_ref[...])
            alpha_ref[...] = jnp.exp(m_old - m_new)
            m_ref[...] = m_new

        def values(kb, s_ref):
            m_new, alpha = m_ref[...], alpha_ref[...]
            for h in range(N_HEADS):
                p = jnp.exp(s_ref[h] - m_new[h:h + 1, :]).astype(BF16)
                vs = slice(h * V_SLAB, (h + 1) * V_SLAB)
                pv = jnp.dot(vT_ref[vs, rows(kb)], p, preferred_element_type=F32)
                acc_ref[vs, :] = acc_ref[vs, :] * alpha[h:h + 1, :] + pv

        def step(kb, s_prev, s_cur):
            values(kb - 1, s_prev)
            scores(kb, s_cur)

        scores(0, sa_ref)

        def body(kb, carry):
            @pl.when(kb % 2 == 1)
            def _():
                step(kb, sa_ref, sb_ref)

            @pl.when(kb % 2 == 0)
            def _():
                step(kb, sb_ref, sa_ref)
            return carry

        lax.fori_loop(1, n_kb, body, 0)

        @pl.when(n_kb % 2 == 1)
        def _():
            values(n_kb - 1, sa_ref)

        @pl.when(n_kb % 2 == 0)
        def _():
            values(n_kb - 1, sb_ref)

        for h in range(N_HEADS):
            num = acc_ref[h * V_SLAB:h * V_SLAB + HEAD_DIM, :]
            den = acc_ref[h * V_SLAB + HEAD_DIM:h * V_SLAB + HEAD_DIM + 1, :]
            oT_ref[h * HEAD_DIM:(h + 1) * HEAD_DIM, :] = num / den

    @pl.when(has_ties)
    def _():
        attention(True)

    @pl.when(jnp.logical_not(has_ties))
    def _():
        attention(False)


def _dsa(qT, qiT, iwT, k, vT, ki, *, b, qb, lb, n_sel):
    n = qT.shape[1]
    t = n // b
    nqb = t // qb
    assert t % lb == 0 and qb == lb and qb % CHUNK == 0
    body = functools.partial(_dsa_body, qb=qb, lb=lb, sub=64, n_sel=n_sel)
    qcol = lambda height: pl.BlockSpec((height, qb), lambda bi, j: (0, bi * nqb + j))
    return pl.pallas_call(
        body,
        grid=(b, nqb),
        in_specs=[
            qcol(D_ATT), qcol(N_IDX_HEADS * IDX_DIM), qcol(N_IDX_HEADS),
            _resident((t, D_ATT), lambda bi, j: (bi, 0)),
            _resident((N_HEADS * V_SLAB, t), lambda bi, j: (0, bi)),
            _resident((t, IDX_DIM), lambda bi, j: (bi, 0)),
        ],
        out_specs=qcol(D_ATT),
        out_shape=jax.ShapeDtypeStruct((D_ATT, n), F32),
        scratch_shapes=[
            pltpu.VMEM((t, qb), F32),
            pltpu.VMEM((N_HEADS, 2 * HEAD_DIM, qb), BF16),
            pltpu.VMEM((lb, qb), F32),
            pltpu.VMEM((N_HEADS, lb, qb), F32),
            pltpu.VMEM((N_HEADS, lb, qb), F32),
            pltpu.VMEM((N_HEADS * V_SLAB, qb), F32),
            pltpu.VMEM((N_HEADS, qb), F32),
            pltpu.VMEM((N_HEADS, qb), F32),
            pltpu.VMEM((N_HEADS, qb), F32),
            pltpu.VMEM((1, qb), F32),
        ],
        compiler_params=_cparams(("parallel", "arbitrary")),
        name="dsa",
    )(qT, qiT, iwT, k, vT, ki)


def _dsa_sample_body(wq_ref, wqi_ref, iw_ref, kc_ref, vc_ref, kic_ref, kn_ref, vn_ref, kin_ref, o_ref,
                     sc_ref, s_ref, kx_ref, vx_ref, kix_ref, acc_ref,
                     *, lb, n_cache_kb, past, t_new, n_sel):
    qb = LANES
    n_kb = n_cache_kb + 1
    l_valid = past + t_new
    lane = lax.broadcasted_iota(I32, (1, qb), 1)
    qchunk = (past + (lane & (t_new - 1))) >> CHUNK_SHIFT
    full8 = lambda v: jnp.full((ACC_ROWS, qb), v, F32)

    def rows(kb):
        return pl.ds(kb * lb, lb) if isinstance(kb, int) else pl.ds(pl.multiple_of(kb * lb, lb), lb)

    def over_blocks(fn, carry, cache_ref, new_ref):
        def cached(kb, c):
            cols = pl.ds(pl.multiple_of(kb * lb, lb), lb)
            return fn(kb, c, cache_ref[0, :, cols].astype(BF16))
        return fn(n_cache_kb, lax.fori_loop(0, n_cache_kb, cached, carry), new_ref[...])

    for new_ref, stage_ref in ((kn_ref, kx_ref), (vn_ref, vx_ref), (kin_ref, kix_ref)):
        feat = stage_ref.shape[0]
        new = new_ref[0]
        if feat < LANES:
            new = jnp.concatenate([new, jnp.zeros((t_new, LANES - feat), F32)], axis=1)
        new = jnp.concatenate([new, jnp.zeros((LANES - t_new, new.shape[1]), F32)], axis=0)
        stage_ref[...] = jnp.zeros(stage_ref.shape, BF16)
        stage_ref[:, 0:LANES] = new.T[0:feat, :].astype(BF16)

    def idx_block(kb, carry, kit):
        mx8, mn8, n8 = carry
        r = jnp.maximum(jnp.dot(wqi_ref[0], kit, preferred_element_type=F32).T, 0.0) * iw_ref[0]
        for shift in (t_new, 2 * t_new, 4 * t_new):
            r = r + pltpu.roll(r, shift, 1)
        kpos = kb * lb + lax.broadcasted_iota(I32, (lb, qb), 0)
        adm = jnp.where(kpos < l_valid, kpos >> CHUNK_SHIFT, qchunk + 1) <= qchunk
        sc_ref[rows(kb), :] = jnp.where(adm, r, -INF)
        return (jnp.maximum(mx8, _fold(jnp.where(adm, r, -INF), jnp.max)),
                jnp.minimum(mn8, _fold(jnp.where(adm, r, INF), jnp.min)),
                n8 + _fold(jnp.where(adm, 1.0, 0.0), jnp.sum))

    mx8, mn8, n8 = over_blocks(idx_block, (full8(-INF), full8(INF), full8(0.0)), kic_ref, kix_ref)
    mx, mn, n_adm = _fin(mx8, jnp.max), _fin(mn8, jnp.min), _fin(n8, jnp.sum)
    if n_kb % 2 == 1:
        sc_ref[rows(n_kb), :] = jnp.full((lb, qb), -INF, F32)
    n_units, unit = (n_kb + 1) // 2, 2 * lb
    thr, tie = _select_threshold(sc_ref, n_units, unit, qb, n_sel, mx, mn, n_adm, lane >= 0)
    has_ties = _any_set(tie == 1) > 0
    thr_valid = jnp.maximum(thr, -F32_MAX)

    def attention(with_ties):
        need, tri = _tie_setup(sc_ref, n_units, unit, lb, qb, n_sel, thr) if with_ties else (None, None)

        def score_block(kb, carry, kt):
            m8, seen = carry
            bias, seen = _selection_bias(sc_ref[rows(kb), :], thr, thr_valid, need, seen, tri, with_ties)
            s = jnp.dot(wq_ref[0], kt, preferred_element_type=F32).T + bias
            s_ref[rows(kb), :] = s
            return jnp.maximum(m8, _fold(s, jnp.max)), seen

        m8, _ = over_blocks(score_block, (full8(M_FLOOR), jnp.zeros((1, qb), F32)), kc_ref, kx_ref)
        m = _fin(m8, jnp.max)
        acc_ref[...] = jnp.zeros(acc_ref.shape, F32)

        def value_block(kb, den8, vt):
            p = jnp.exp(s_ref[rows(kb), :] - m).astype(BF16)
            acc_ref[...] += jnp.dot(vt, p, preferred_element_type=F32)
            return den8 + _fold(p.astype(F32), jnp.sum)

        den = _fin(over_blocks(value_block, full8(0.0), vc_ref, vx_ref), jnp.sum)
        o_all = (acc_ref[...] / den).T
        for h in range(N_HEADS):
            rs = slice(h * t_new, (h + 1) * t_new)
            cs = slice(h * HEAD_DIM, (h + 1) * HEAD_DIM)
            o_ref[0, :, cs] = o_all[rs, cs]

    @pl.when(has_ties)
    def _():
        attention(True)

    @pl.when(jnp.logical_not(has_ties))
    def _():
        attention(False)


def _dsa_sample(q, qi, iw, k_cache, v_cache, ki_cache, layer, k_new, v_new, ki_new, *, lb, n_sel):
    b, t, _ = q.shape
    past = ki_cache.shape[2]
    assert N_HEADS * t == LANES and past % lb == 0 and LANES <= lb
    n_cache_kb = past // lb
    lp = (n_cache_kb + 1 + (n_cache_kb + 1) % 2) * lb
    eye = jnp.eye(N_HEADS, dtype=q.dtype)
    qh = q.reshape(b, t, N_HEADS, HEAD_DIM)
    wq = jnp.einsum("bqhd,gh->bhqgd", qh, eye).reshape(b, LANES, D_ATT)
    wqi = jnp.transpose(qi.reshape(b, t, N_IDX_HEADS, IDX_DIM), (0, 2, 1, 3)).reshape(b, LANES, IDX_DIM)
    iw_row = jnp.transpose(iw, (0, 2, 1)).reshape(b, 1, LANES)
    per_b = lambda shape: pl.BlockSpec((1,) + shape, lambda bi: (bi, 0, 0))
    cache = lambda shape: pl.BlockSpec((1,) + shape, lambda bi: (layer * b + bi, 0, 0))
    body = functools.partial(_dsa_sample_body, lb=lb, n_cache_kb=n_cache_kb, past=past, t_new=t, n_sel=n_sel)
    return pl.pallas_call(
        body,
        grid=(b,),
        in_specs=[per_b((LANES, D_ATT)), per_b((LANES, IDX_DIM)), per_b((1, LANES)),
                  cache((D_ATT, past)), cache((D_ATT, past)), cache((IDX_DIM, past)),
                  per_b((t, D_ATT)), per_b((t, D_ATT)), per_b((t, IDX_DIM))],
        out_specs=per_b((t, D_ATT)),
        out_shape=jax.ShapeDtypeStruct((b, t, D_ATT), F32),
        scratch_shapes=[
            pltpu.VMEM((lp, LANES), F32),
            pltpu.VMEM((lp, LANES), F32),
            pltpu.VMEM((D_ATT, lb), BF16),
            pltpu.VMEM((D_ATT, lb), BF16),
            pltpu.VMEM((IDX_DIM, lb), BF16),
            pltpu.VMEM((D_ATT, LANES), F32),
        ],
        compiler_params=_cparams(("parallel",)),
        name="dsa_sample",
    )(wq, wqi, iw_row, k_cache, v_cache, ki_cache, k_new, v_new, ki_new)


def _tail_body(x_ref, o_ref, mc_ref, ga_ref, wao_ref, wout_ref, g_ref, w1_ref, w2_ref, y_ref,
               *, ff_chunk, o_transposed):
    o = o_ref[...].T if o_transposed else o_ref[...]
    attn = jnp.dot(o.astype(BF16), wao_ref[...], preferred_element_type=F32)
    merged = mc_ref[...] + ga_ref[...] * attn
    x1 = x_ref[...] + jnp.dot(merged.astype(BF16), wout_ref[...], preferred_element_type=F32)
    ms = jnp.mean(x1 * x1, axis=-1, keepdims=True)
    h = (x1 * lax.rsqrt(ms + EPS) * g_ref[...]).astype(BF16)
    y = x1
    for c in range(D_FF // ff_chunk):
        cs = slice(c * ff_chunk, (c + 1) * ff_chunk)
        u = jnp.maximum(jnp.dot(h, w1_ref[:, cs], preferred_element_type=F32), 0.0)
        y = y + jnp.dot((u * u).astype(BF16), w2_ref[cs, :], preferred_element_type=F32)
    y_ref[...] = y


def _tail(x, o, mc, ga, wao, wout, g, w1, w2, tm, o_transposed):
    n = x.shape[0]
    row = lambda width: pl.BlockSpec((tm, width), lambda i: (i, 0))
    full = lambda a: _resident(a.shape, lambda i: (0, 0))
    o_spec = pl.BlockSpec((D_ATT, tm), lambda i: (0, i)) if o_transposed else row(D_ATT)
    return pl.pallas_call(
        functools.partial(_tail_body, ff_chunk=1024, o_transposed=o_transposed),
        grid=(n // tm,),
        in_specs=[row(D_MODEL), o_spec, row(D_MODEL), row(D_MODEL),
                  full(wao), full(wout), pl.BlockSpec((1, D_MODEL), lambda i: (0, 0)), full(w1), full(w2)],
        out_specs=row(D_MODEL),
        out_shape=jax.ShapeDtypeStruct((n, D_MODEL), F32),
        compiler_params=_cparams(("parallel",)),
        name="tail",
    )(x, o, mc, ga, wao, wout, g, w1, w2)


def _rope_tables(pos):
    inv = ROPE_THETA ** (-jnp.arange(ROT_HALF, dtype=F32) / ROT_HALF)
    ang = pos.astype(F32)[:, None] * inv[None, :]
    cos, sin = jnp.cos(ang), jnp.sin(ang)
    r = np.arange(LANES) % HEAD_DIM
    jj = r % ROT_HALF
    first = jnp.asarray(r < ROT_HALF)[None, :]
    second = jnp.asarray((r >= ROT_HALF) & (r < ROT_DIM))[None, :]
    c = jnp.where(first | second, cos[:, jj], 1.0)
    sa = jnp.where(first, -sin[:, jj], 0.0)
    sb = jnp.where(second, sin[:, jj], 0.0)
    return c, sa, sb


def _pack_w_in(w):
    pad = jnp.zeros((D_MODEL, LANES - (D_IN_HEAD - COL_KIW)), w.dtype)
    return jnp.concatenate([w[:, :D_IN_HEAD], pad, w[:, D_IN_HEAD:]], axis=1).astype(BF16)


def _tile2(v):
    return jnp.concatenate([v, v])[None, :].astype(F32)


def _layer(x, tabs, n_tab_blocks, conv_state, caches, kv_all, layer, depth, w, *, tm, conv_tm, dsa_cfg):
    b, t, _ = x.shape
    n = b * t
    prompt = caches is None
    (norm_mix, w_in_p, conv_w, conv_b, ln_g, ln_b, w_conv_out, q_norm, k_norm,
     w_attn_out, w_out, norm_ffn, w_ff1, w_ff2) = w
    xf = x.reshape(n, D_MODEL)
    glu, q, qi, gc, ga, k_all, v_all, ki_all, *extra = _inproj(
        xf, norm_mix[None, :], w_in_p, *tabs, _tile2(q_norm), _tile2(k_norm), kv_all, layer, depth, b,
        tm, n_tab_blocks, prompt)

    glu3 = glu.reshape(b, t, D_CONV)
    state_p = jnp.pad(conv_state, ((0, 0), (HALO - (CONV_W - 1), 0), (0, 0)))
    halo_src = glu3 if t >= HALO else state_p
    cw_p = jnp.pad(conv_w, ((0, HALO - CONV_W), (0, 0)))
    mc = _conv_branch(glu3, halo_src, state_p, cw_p, conv_b[None, :], ln_g[None, :], ln_b[None, :],
                      w_conv_out, gc.reshape(b, t, D_MODEL), conv_tm)
    new_conv = jnp.concatenate([conv_state, glu3], axis=1)[:, -(CONV_W - 1):]

    if prompt:
        kb, vt_ones, iwT, kib = extra
        o = _dsa(q, qi, iwT, kb, vt_ones, kib, b=b, **dsa_cfg)
    else:
        k, v, kiw = extra
        iw = kiw[:, IDX_DIM:IDX_DIM + N_IDX_HEADS].reshape(b, t, N_IDX_HEADS)
        o = _dsa_sample(q.reshape(b, t, D_ATT), qi.reshape(b, t, D_ATT), iw, *caches, layer,
                        k.reshape(b, t, D_ATT), v.reshape(b, t, D_ATT), kiw[:, :IDX_DIM].reshape(b, t, IDX_DIM),
                        **dsa_cfg).reshape(n, D_ATT)

    y = _tail(xf, o, mc.reshape(n, D_MODEL), ga, w_attn_out, w_out, norm_ffn[None, :], w_ff1, w_ff2, tm, prompt)
    return y.reshape(b, t, D_MODEL), (k_all, v_all, ki_all), new_conv


def kernel(x_prompt, x_sample, cache_k, cache_v, cache_kidx, state_conv, norm_mix, w_in, conv_w, conv_b,
           conv_ln_g, conv_ln_b, w_conv_out, q_norm, k_norm, w_attn_out, w_out, norm_ffn, w_ff1, w_ff2):
    bp, tp, _ = x_prompt.shape
    bs, ts, _ = x_sample.shape
    depth = norm_mix.shape[0]
    past = cache_k.shape[2]
    tm = 256
    tabs_p = _rope_tables(jnp.arange(tp, dtype=I32))
    tabs_s = tuple(jnp.tile(a, (bs, 1)) for a in _rope_tables(past + jnp.arange(ts, dtype=I32)))
    cfg_p = dict(qb=256, lb=256, n_sel=min(TOPK_MAX, tp // 4))
    cfg_s = dict(lb=256, n_sel=min(TOPK_MAX, (past + ts) // 4))
    caches = (jnp.transpose(cache_k, (0, 1, 3, 4, 2)).reshape(depth * bs, D_ATT, past),
              jnp.transpose(cache_v, (0, 1, 3, 4, 2)).reshape(depth * bs, D_ATT, past),
              jnp.transpose(cache_kidx, (0, 1, 3, 2)).reshape(depth * bs, IDX_DIM, past))
    kv_p = tuple(jnp.zeros(s, F32) for s in _kv_all_shapes(depth, bp, tp, True))
    kv_s = tuple(jnp.zeros(s, F32) for s in _kv_all_shapes(depth, bs, ts, False))
    hp, hs = x_prompt, x_sample
    conv_p, conv_s = [], []
    for l in range(depth):
        w = (norm_mix[l], _pack_w_in(w_in[l]), conv_w[l], conv_b[l], conv_ln_g[l], conv_ln_b[l],
             w_conv_out[l].astype(BF16), q_norm[l], k_norm[l], w_attn_out[l].astype(BF16),
             w_out[l].astype(BF16), norm_ffn[l], w_ff1[l].astype(BF16), w_ff2[l].astype(BF16))
        zero_state = jnp.zeros((bp, CONV_W - 1, D_CONV), F32)
        hp, kv_p, conv = _layer(hp, tabs_p, tp // tm, zero_state, None, kv_p, l, depth, w,
                                tm=tm, conv_tm=tm, dsa_cfg=cfg_p)
        conv_p.append(conv)
        hs, kv_s, conv = _layer(hs, tabs_s, (bs * ts) // tm, state_conv[l], caches, kv_s, l, depth, w,
                                tm=tm, conv_tm=ts, dsa_cfg=cfg_s)
        conv_s.append(conv)
    heads_p = lambda a: jnp.transpose(a.reshape(depth, bp, N_HEADS, HEAD_DIM, tp), (0, 1, 4, 2, 3))
    heads_s = lambda a: a.reshape(depth, bs, ts, N_HEADS, HEAD_DIM)
    return (hp, hs,
            heads_p(kv_p[0]), heads_p(kv_p[1]),
            jnp.transpose(kv_p[2].reshape(depth, bp, IDX_DIM, tp), (0, 1, 3, 2)), jnp.stack(conv_p),
            heads_s(kv_s[0]), heads_s(kv_s[1]), kv_s[2].reshape(depth, bs, ts, IDX_DIM), jnp.stack(conv_s))
```

```python
import functools

import jax
import jax.numpy as jnp
import numpy as np
from jax import lax
from jax.experimental import pallas as pl
from jax.experimental.pallas import tpu as pltpu

F32 = jnp.float32
BF16 = jnp.bfloat16
I32 = jnp.int32

D_MODEL = 1024
CHUNK = 64
CHUNK_SHIFT = 6
N_HEADS = 8
HEAD_DIM = 64
ROT_DIM = HEAD_DIM // 4
ROT_HALF = ROT_DIM // 2
ROPE_THETA = 500000.0
N_IDX_HEADS = 8
IDX_DIM = 64
TOPK_MAX = 256
D_CONV = 512
CONV_W = 31
D_FF = 4 * D_MODEL
EPS = 1e-6
LN_EPS = 1e-5
ATTN_SCALE = HEAD_DIM ** -0.5
IDX_SCALE = (N_IDX_HEADS ** -0.5) * (IDX_DIM ** -0.5)
D_ATT = N_HEADS * HEAD_DIM

LANES = 128
SUBLANES = 8
ACC_ROWS = 4 * SUBLANES
VMEM_LIMIT_BYTES = 56 * 1024 * 1024

COL_KIW = 6 * 512
COL_GC = COL_KIW + LANES
COL_GA = COL_GC + D_MODEL
D_IN_PACKED = COL_GA + D_MODEL
D_IN_HEAD = 2 * D_CONV + 3 * D_ATT + N_IDX_HEADS * IDX_DIM + IDX_DIM + N_IDX_HEADS

HALO = 32
INF = float("inf")
F32_MAX = float(np.finfo(np.float32).max)
ONES_ROWS = 16
V_SLAB = HEAD_DIM + ONES_ROWS
M_FLOOR = -1e30
MASK_BIAS = -2e30
SEARCH_FAST_ITERS = 48
SEARCH_MAX_ITERS = 4096


def _cparams(sem):
    return pltpu.CompilerParams(dimension_semantics=sem, vmem_limit_bytes=VMEM_LIMIT_BYTES)


def _resident(block_shape, index_map):
    return pl.BlockSpec(block_shape, index_map, pipeline_mode=pl.Buffered(1))


def _rope(x, c, sa, sb):
    return x * c + pltpu.roll(x, LANES - ROT_HALF, 1) * sa + pltpu.roll(x, ROT_HALF, 1) * sb


def _head_rms(x, gain):
    lane = lax.broadcasted_iota(I32, x.shape, 1)
    lo = lane < HEAD_DIM
    x2 = x * x
    s_lo = jnp.sum(jnp.where(lo, x2, 0.0), axis=-1, keepdims=True)
    s_hi = jnp.sum(jnp.where(lo, 0.0, x2), axis=-1, keepdims=True)
    ms = jnp.where(lo, s_lo, s_hi) * (1.0 / HEAD_DIM)
    return x * lax.rsqrt(ms + EPS) * gain


def _inproj_body(*refs, transposed, n_aliased):
    x_ref, g_ref, w_ref, cos_ref, sa_ref, sb_ref, qg_ref, kg_ref = refs[:8]
    glu_ref, q_ref, qi_ref, gc_ref, ga_ref, kall_ref, vall_ref, kiall_ref, *extra = refs[8 + n_aliased:]
    tm = x_ref.shape[0]
    x = x_ref[...]
    ms = jnp.mean(x * x, axis=-1, keepdims=True)
    h = (x * lax.rsqrt(ms + EPS) * g_ref[...]).astype(BF16)

    def proj(c0, n):
        return jnp.dot(h, w_ref[:, c0:c0 + n], preferred_element_type=F32)

    glu_ref[...] = proj(0, D_CONV) * jax.nn.sigmoid(proj(D_CONV, D_CONV))

    c, sa, sb = cos_ref[...], sa_ref[...], sb_ref[...]
    zq = proj(2 * D_CONV, D_ATT)
    zk = proj(2 * D_CONV + D_ATT, D_ATT)
    zqi = proj(2 * D_CONV + 3 * D_ATT, N_IDX_HEADS * IDX_DIM)
    zv = proj(2 * D_CONV + 2 * D_ATT, D_ATT)
    for g in range(D_ATT // LANES):
        sl = slice(g * LANES, (g + 1) * LANES)
        qh = _rope(_head_rms(zq[:, sl], qg_ref[...]), c, sa, sb) * ATTN_SCALE
        kh = _rope(_head_rms(zk[:, sl], kg_ref[...]), c, sa, sb)
        qih = _rope(zqi[:, sl], c, sa, sb)
        if transposed:
            kb_ref, vt_ref = extra[0], extra[1]
            q_ref[sl, :] = qh.T.astype(BF16)
            qi_ref[sl, :] = qih.T.astype(BF16)
            kb_ref[:, sl] = kh.astype(BF16)
            kall_ref[sl, :] = kh.T
            vt32 = zv[:, sl].T
            vall_ref[sl, :] = vt32
            vt = vt32.astype(BF16)
            ones = jnp.ones((ONES_ROWS, tm), BF16)
            for half in range(2):
                r0 = (2 * g + half) * V_SLAB
                vt_ref[r0:r0 + HEAD_DIM, :] = vt[half * HEAD_DIM:(half + 1) * HEAD_DIM, :]
                vt_ref[r0 + HEAD_DIM:r0 + V_SLAB, :] = ones
        else:
            q_ref[:, sl] = qh.astype(BF16)
            qi_ref[:, sl] = qih.astype(BF16)
            extra[0][:, sl] = kh
            extra[1][:, sl] = zv[:, sl]
            for half in range(2):
                hs = slice(half * HEAD_DIM, (half + 1) * HEAD_DIM)
                head_rows = pl.ds(2 * g + half, tm, stride=N_HEADS)
                kall_ref[head_rows, :] = kh[:, hs]
                vall_ref[head_rows, :] = zv[:, sl][:, hs]

    zkiw = proj(COL_KIW, LANES)
    lane = lax.broadcasted_iota(I32, zkiw.shape, 1)
    is_ki = lane < IDX_DIM
    roped = _rope(zkiw, jnp.where(is_ki, c, 1.0), jnp.where(is_ki, sa, 0.0), jnp.where(is_ki, sb, 0.0))
    kiw = jnp.where(is_ki, roped, zkiw * IDX_SCALE)
    if transposed:
        kiwt = kiw.T
        kiall_ref[...] = kiwt[0:IDX_DIM, :]
        extra[2][...] = kiwt[IDX_DIM:IDX_DIM + N_IDX_HEADS, :]
        extra[3][...] = kiw[:, 0:IDX_DIM].astype(BF16)
    else:
        kiall_ref[...] = kiw[:, 0:IDX_DIM]
        extra[2][...] = kiw

    gc_ref[...] = jax.nn.sigmoid(proj(COL_GC, D_MODEL))
    ga_ref[...] = jax.nn.sigmoid(proj(COL_GA, D_MODEL))


def _kv_all_shapes(depth, b, t, transposed):
    if transposed:
        return ((depth * b * D_ATT, t), (depth * b * D_ATT, t), (depth * b * IDX_DIM, t))
    return ((depth * b * t * N_HEADS, HEAD_DIM), (depth * b * t * N_HEADS, HEAD_DIM), (depth * b * t, IDX_DIM))


def _inproj(x, g, w, cos, sa, sb, qg, kg, kv_all, layer, depth, b, tm, n_tab_blocks, transposed):
    n = x.shape[0]
    t = n // b
    nb = n // tm
    nt = t // tm if transposed else None
    row = lambda width: pl.BlockSpec((tm, width), lambda i: (i, 0))
    col = lambda height: pl.BlockSpec((height, tm), lambda i: (0, i))
    tab = pl.BlockSpec((tm, LANES), lambda i: (i % n_tab_blocks, 0))
    vec = lambda width: pl.BlockSpec((1, width), lambda i: (0, 0))
    rows_of = lambda width, dt: (jax.ShapeDtypeStruct((n, width), dt), row(width))
    cols_of = lambda height, dt: (jax.ShapeDtypeStruct((height, n), dt), col(height))
    shapes = _kv_all_shapes(depth, b, t, transposed)
    if transposed:
        slab = lambda height: pl.BlockSpec((height, tm), lambda i: (layer * b + i // nt, i % nt))
        all_specs = [slab(D_ATT), slab(D_ATT), slab(IDX_DIM)]
    else:
        rows5 = lambda r, width: pl.BlockSpec((r, width), lambda i: (layer * nb + i, 0))
        all_specs = [rows5(tm * N_HEADS, HEAD_DIM), rows5(tm * N_HEADS, HEAD_DIM), rows5(tm, IDX_DIM)]
    outs = [
        rows_of(D_CONV, F32),
        cols_of(D_ATT, BF16) if transposed else rows_of(D_ATT, BF16),
        cols_of(D_ATT, BF16) if transposed else rows_of(D_ATT, BF16),
        rows_of(D_MODEL, F32),
        rows_of(D_MODEL, F32),
    ] + [(jax.ShapeDtypeStruct(s, F32), spec) for s, spec in zip(shapes, all_specs)]
    n_common = len(outs) - len(shapes)
    if transposed:
        outs += [rows_of(D_ATT, BF16), cols_of(N_HEADS * V_SLAB, BF16), cols_of(N_IDX_HEADS, F32),
                 rows_of(IDX_DIM, BF16)]
    else:
        outs += [rows_of(D_ATT, F32), rows_of(D_ATT, F32), rows_of(LANES, F32)]
    aliased = tuple(kv_all)
    n_in = 8
    return pl.pallas_call(
        functools.partial(_inproj_body, transposed=transposed, n_aliased=len(aliased)),
        grid=(nb,),
        in_specs=[row(D_MODEL), vec(D_MODEL), _resident((D_MODEL, D_IN_PACKED), lambda i: (0, 0)),
                  tab, tab, tab, vec(LANES), vec(LANES)] + [pl.BlockSpec(memory_space=pl.ANY)] * len(aliased),
        out_specs=[o[1] for o in outs],
        out_shape=[o[0] for o in outs],
        input_output_aliases={n_in + a: n_common + a for a in range(len(aliased))},
        compiler_params=_cparams(("parallel",)),
        name="inproj",
    )(x, g, w, cos, sa, sb, qg, kg, *aliased)


def _conv_body(cur_ref, halo_ref, st_ref, cw_ref, cb_ref, lg_ref, lb_ref, wco_ref, gc_ref,
               out_ref, win_ref, sh_ref, act_ref, *, tm, rows):
    i = pl.program_id(1)
    win_ref[0:HALO, :] = jnp.where(i == 0, st_ref[0], halo_ref[0])
    win_ref[HALO:HALO + tm, :] = cur_ref[0]
    span = HALO + tm - SUBLANES
    for s in range(1, SUBLANES):
        sh_ref[s, 0:span, :] = win_ref[s:s + span, :]
    off = HALO - (CONV_W - 1)
    for r in range(tm // rows):
        acc = jnp.zeros((rows, D_CONV), F32) + cb_ref[...]
        for j in range(CONV_W):
            s, base = (off + j) % SUBLANES, r * rows + (off + j) // SUBLANES * SUBLANES
            tap = win_ref[base:base + rows, :] if s == 0 else sh_ref[s, base:base + rows, :]
            acc = acc + tap * cw_ref[j:j + 1, :]
        mu = jnp.mean(acc, axis=-1, keepdims=True)
        d = acc - mu
        var = jnp.mean(d * d, axis=-1, keepdims=True)
        y = d * lax.rsqrt(var + LN_EPS) * lg_ref[...] + lb_ref[...]
        act_ref[r * rows:(r + 1) * rows, :] = (y * jax.nn.sigmoid(y)).astype(BF16)
    out_ref[0] = gc_ref[0] * jnp.dot(act_ref[...], wco_ref[...], preferred_element_type=F32)


def _conv_branch(glu, halo_src, state, cw, cb, lg, lb, wco, gc, tm):
    b, t, _ = glu.shape
    rows = min(tm, 32)
    hb = tm // HALO
    vec = pl.BlockSpec((1, D_CONV), lambda bi, i: (0, 0))
    return pl.pallas_call(
        functools.partial(_conv_body, tm=tm, rows=rows),
        grid=(b, t // tm),
        in_specs=[
            pl.BlockSpec((1, tm, D_CONV), lambda bi, i: (bi, i, 0)),
            pl.BlockSpec((1, HALO, D_CONV), lambda bi, i: (bi, jnp.maximum(i * hb - 1, 0), 0)),
            pl.BlockSpec((1, HALO, D_CONV), lambda bi, i: (bi, 0, 0)),
            pl.BlockSpec((HALO, D_CONV), lambda bi, i: (0, 0)),
            vec, vec, vec,
            _resident((D_CONV, D_MODEL), lambda bi, i: (0, 0)),
            pl.BlockSpec((1, tm, D_MODEL), lambda bi, i: (bi, i, 0)),
        ],
        out_specs=pl.BlockSpec((1, tm, D_MODEL), lambda bi, i: (bi, i, 0)),
        out_shape=jax.ShapeDtypeStruct((b, t, D_MODEL), F32),
        scratch_shapes=[pltpu.VMEM((HALO + tm, D_CONV), F32), pltpu.VMEM((SUBLANES, HALO + tm, D_CONV), F32),
                        pltpu.VMEM((tm, D_CONV), BF16)],
        compiler_params=_cparams(("parallel", "arbitrary")),
        name="conv_branch",
    )(glu, halo_src, state, cw, cb, lg, lb, wco, gc)


def _fold(x, op):
    return op(x.reshape(x.shape[0] // ACC_ROWS, ACC_ROWS, x.shape[1]), axis=0)


def _fin(x8, op):
    return op(x8, axis=0, keepdims=True)


def _any_set(flag):
    return jnp.max(jnp.where(flag, 1, 0))


def _select_threshold(sc_ref, n_units, unit, qb, n_sel, mx, mn, n_adm, lane_ok):
    kf = float(n_sel)
    log_target = float(np.log(n_sel + 0.5))
    full8 = lambda v: jnp.full((ACC_ROWS, qb), v, F32)
    row = lambda v: jnp.full((1, qb), v, F32)

    def rows(u):
        return pl.ds(pl.multiple_of(u * unit, unit), unit)

    def count2(p):
        def blk(u, c):
            x = sc_ref[rows(u), :]
            return (c[0] + _fold(jnp.where(x >= p, 1.0, 0.0), jnp.sum),
                    c[1] + _fold(jnp.where(x > p, 1.0, 0.0), jnp.sum))
        ge8, gt8 = lax.fori_loop(0, n_units, blk, (full8(0.0), full8(0.0)))
        return _fin(ge8, jnp.sum), _fin(gt8, jnp.sum)

    def scan(p, lo, up, snap):
        def blk(u, c):
            x = sc_ref[rows(u), :]
            out = [c[0] + _fold(jnp.where(x >= p, 1.0, 0.0), jnp.sum)]
            if snap:
                out.append(jnp.minimum(c[1], _fold(jnp.where(x >= lo, x, INF), jnp.min)))
                out.append(jnp.maximum(c[2], _fold(jnp.where(x < up, x, -INF), jnp.max)))
            return tuple(out)
        init = (full8(0.0), full8(INF), full8(-INF)) if snap else (full8(0.0),)
        res = lax.fori_loop(0, n_units, blk, init)
        if snap:
            return _fin(res[0], jnp.sum), _fin(res[1], jnp.min), _fin(res[2], jnp.max)
        return _fin(res[0], jnp.sum)

    def pivot(lo, c_lo, up, c_up, wl, wu, bisect):
        xu = jnp.where(up == INF, mx, up)
        gl = (jnp.log(c_lo) - log_target) * wl
        gu = (log_target - jnp.log(jnp.maximum(c_up, 0.5))) * wu
        p = jnp.where(bisect, 0.5 * lo + 0.5 * xu, lo + (xu - lo) * (gl / (gl + gu)))
        p = jnp.minimum(p, xu)
        stuck = jnp.logical_not(p > lo)
        return jnp.where(stuck, xu, p), stuck

    def update(p, c, lo, c_lo, up, c_up, live):
        ge = c >= kf
        to_lo = jnp.logical_and(live, ge)
        to_up = jnp.logical_and(live, jnp.logical_not(ge))
        return (jnp.where(to_lo, p, lo), jnp.where(to_lo, c, c_lo),
                jnp.where(to_up, p, up), jnp.where(to_up, c, c_up), to_lo, to_up)

    def count_pass(st):
        it, _, _, lo, c_lo, up, c_up, wl, wu, side, done = st
        live = done == 0
        p, stuck = pivot(lo, c_lo, up, c_up, wl, wu, it % 8 == 7)
        c = scan(p, lo, up, False)
        lo, c_lo, up, c_up, to_lo, to_up = update(p, c, lo, c_lo, up, c_up, live)
        wl = jnp.where(jnp.logical_and(to_up, side < 0.0), wl * 0.5, jnp.where(to_lo, 1.0, wl))
        wu = jnp.where(jnp.logical_and(to_lo, side > 0.0), wu * 0.5, jnp.where(to_up, 1.0, wu))
        side = jnp.where(to_lo, 1.0, jnp.where(to_up, -1.0, side))
        done = jnp.where(c_lo == kf, 1, done)
        flags = jnp.sum(jnp.where(done == 0, 1, 0) + jnp.where(jnp.logical_and(stuck, live), 1 << 16, 0))
        return (it + 1, flags & 0xFFFF, flags >> 16, lo, c_lo, up, c_up, wl, wu, side, done)

    def snap_pass(st):
        it, _, lo, c_lo, up, c_up, done, tie = st
        live = done == 0
        p, _ = pivot(lo, c_lo, up, c_up, row(1.0), row(1.0), it % 2 == 1)
        c, a, b = scan(p, lo, up, True)
        tied = jnp.logical_and(live, a == b)
        lo2, c_lo2, up2, c_up2, _, _ = update(p, c, lo, c_lo, up, c_up,
                                              jnp.logical_and(live, jnp.logical_not(tied)))
        lo2 = jnp.where(live, jnp.maximum(lo2, a), lo2)
        tie = jnp.where(tied, 1, tie)
        done = jnp.where(jnp.logical_or(tied, c_lo2 == kf), 1, done)
        return (it + 1, _any_set(done == 0), lo2, c_lo2, up2, c_up2, done, tie)

    few = n_adm <= kf
    live0 = jnp.logical_and(jnp.logical_not(few), lane_ok)
    ge0, gt0 = count2(row(0.0))
    tie0 = jnp.logical_and(live0, jnp.logical_and(gt0 < kf, ge0 >= kf))
    above = ge0 >= kf
    lo0 = jnp.where(few, -INF, jnp.where(tie0, 0.0, jnp.where(above, jnp.maximum(mn, 0.0), mn)))
    c_lo0 = jnp.where(jnp.logical_and(above, mn < 0.0), ge0, n_adm)
    up0 = jnp.where(above, INF, 0.0)
    c_up0 = jnp.where(above, 0.0, ge0)
    done0 = jnp.where(jnp.logical_and(live0, jnp.logical_not(tie0)), 0, 1)
    done0 = jnp.where(c_lo0 == kf, 1, done0)
    tie_init = jnp.where(tie0, 1, 0)
    st = (jnp.int32(0), _any_set(done0 == 0), jnp.int32(0), lo0, c_lo0, up0, c_up0,
          row(1.0), row(1.0), row(0.0), done0)
    st = lax.while_loop(lambda s: (s[0] < SEARCH_FAST_ITERS) & (s[1] > 0) & (s[2] == 0), count_pass, st)
    it1, active1, _, lo1, c_lo1, up1, c_up1, _, _, _, done1 = st
    st = (it1, active1, lo1, c_lo1, up1, c_up1, done1, tie_init)
    st = lax.while_loop(lambda s: (s[0] < SEARCH_MAX_ITERS) & (s[1] > 0), snap_pass, st)
    return st[2], st[7]


def _selection_bias(x, thr, thr_valid, need, seen, tri, with_ties):
    if not with_ties:
        return jnp.where(x >= thr_valid, 0.0, MASK_BIAS), seen
    eq = x == thr
    eqf = jnp.where(eq, 1.0, 0.0)
    pref = jnp.dot(tri, eqf.astype(BF16), preferred_element_type=F32) + seen
    take = jnp.where(eq, jnp.where(pref <= need, 1.0, 0.0), jnp.where(x > thr, 1.0, 0.0))
    bias = jnp.where(jnp.where(x > -INF, take, 0.0) > 0.5, 0.0, MASK_BIAS)
    return bias, seen + jnp.sum(eqf, axis=0, keepdims=True)


def _tie_setup(sc_ref, n_units, unit, lb, qb, n_sel, thr):
    def gt_blk(u, c):
        x = sc_ref[pl.ds(pl.multiple_of(u * unit, unit), unit), :]
        return c + _fold(jnp.where(x > thr, 1.0, 0.0), jnp.sum)
    gt = _fin(lax.fori_loop(0, n_units, gt_blk, jnp.zeros((ACC_ROWS, qb), F32)), jnp.sum)
    ri = lax.broadcasted_iota(I32, (lb, lb), 0)
    ci = lax.broadcasted_iota(I32, (lb, lb), 1)
    return float(n_sel) - gt, jnp.where(ci <= ri, 1.0, 0.0).astype(BF16)


def _dsa_body(qT_ref, qiT_ref, iwT_ref, k_ref, vT_ref, ki_ref, oT_ref,
              sc_ref, qpad_ref, bias_ref, sa_ref, sb_ref, acc_ref, m_ref, alpha_ref, bm_ref, seen_ref,
              *, qb, lb, sub, n_sel):
    j = pl.program_id(1)
    n_kb = j + 1
    lane = lax.broadcasted_iota(I32, (1, qb), 1)
    qchunk = (j * qb + lane) >> CHUNK_SHIFT
    fold, fin = _fold, _fin

    def rows(kb):
        return pl.ds(pl.multiple_of(kb * lb, lb), lb)

    def idx_block(kb, carry, diagonal):
        mx8, mn8 = carry
        r0 = pl.multiple_of(kb * lb, lb)
        for s in range(lb // sub):
            kis = ki_ref[pl.ds(r0 + s * sub, sub), :]
            score = jnp.zeros((sub, qb), F32)
            for h in range(N_IDX_HEADS):
                sh = jnp.dot(kis, qiT_ref[h * IDX_DIM:(h + 1) * IDX_DIM, :], preferred_element_type=F32)
                score = score + jnp.maximum(sh, 0.0) * iwT_ref[h:h + 1, :]
            lo_s = hi_s = score
            if diagonal:
                kpos = r0 + s * sub + lax.broadcasted_iota(I32, (sub, qb), 0)
                adm = (kpos >> CHUNK_SHIFT) <= qchunk
                hi_s, lo_s = jnp.where(adm, score, -INF), jnp.where(adm, score, INF)
            sc_ref[pl.ds(r0 + s * sub, sub), :] = hi_s
            mx8 = jnp.maximum(mx8, fold(hi_s, jnp.max))
            mn8 = jnp.minimum(mn8, fold(lo_s, jnp.min))
        return mx8, mn8

    full8 = lambda v: jnp.full((ACC_ROWS, qb), v, F32)
    mx8, mn8 = lax.fori_loop(0, j, lambda kb, c: idx_block(kb, c, False), (full8(-INF), full8(INF)))
    mx8, mn8 = idx_block(j, (mx8, mn8), True)
    mx, mn = fin(mx8, jnp.max), fin(mn8, jnp.min)
    n_adm = ((qchunk + 1) * CHUNK).astype(F32)

    n_units, unit = n_kb, lb
    thr, tie = _select_threshold(sc_ref, n_units, unit, qb, n_sel, mx, mn, n_adm, lane >= 0)
    has_ties = _any_set(tie == 1) > 0
    thr_valid = jnp.maximum(thr, -F32_MAX)

    qpad_ref[...] = jnp.zeros(qpad_ref.shape, BF16)
    for h in range(N_HEADS):
        r = (h % 2) * HEAD_DIM
        qpad_ref[h, r:r + HEAD_DIM, :] = qT_ref[h * HEAD_DIM:(h + 1) * HEAD_DIM, :]

    def attention(with_ties):
        m_ref[...] = jnp.full(m_ref.shape, M_FLOOR, F32)
        acc_ref[...] = jnp.zeros(acc_ref.shape, F32)
        seen_ref[...] = jnp.zeros(seen_ref.shape, F32)
        need, tri = _tie_setup(sc_ref, n_units, unit, lb, qb, n_sel, thr) if with_ties else (None, None)

        def select(kb):
            bias_ref[...], seen_ref[...] = _selection_bias(sc_ref[rows(kb), :], thr, thr_valid, need,
                                                           seen_ref[...], tri, with_ties)

        def scores_head(kb, s_ref, h):
            p2 = (h // 2) * 2 * HEAD_DIM
            s = jnp.dot(k_ref[rows(kb), p2:p2 + 2 * HEAD_DIM], qpad_ref[h],
                        preferred_element_type=F32) + bias_ref[...]
            s_ref[h] = s
            bm_ref[h:h + 1, :] = jnp.max(s, axis=0, keepdims=True)

        def rescale():
            m_old = m_ref[...]
            m_new = jnp.maximum(m_old, bm_ref[...])
            alpha_ref[...] = jnp.exp(m_old - m_new)
            m_ref[...] = m_new

        def values_head(kb, s_ref, h, m_new, alpha):
            p = jnp.exp(s_ref[h] - m_new[h:h + 1, :]).astype(BF16)
            vs = slice(h * V_SLAB, (h + 1) * V_SLAB)
            pv = jnp.dot(vT_ref[vs, rows(kb)], p, preferred_element_type=F32)
            acc_ref[vs, :] = acc_ref[vs, :] * alpha[h:h + 1, :] + pv

        def scores(kb, s_ref):
            select(kb)
            for h in range(N_HEADS):
                scores_head(kb, s_ref, h)
            rescale()

        def values(kb, s_ref):
            m_new, alpha = m_ref[...], alpha_ref[...]
            for h in range(N_HEADS):
                values_head(kb, s_ref, h, m_new, alpha)

        def step(kb, s_prev, s_cur):
            m_new, alpha = m_ref[...], alpha_ref[...]
            select(kb)
            for h in range(N_HEADS):
                scores_head(kb, s_cur, h)
                values_head(kb - 1, s_prev, h, m_new, alpha)
            rescale()

        scores(0, sa_ref)

        def body(kb, carry):
            @pl.when(kb % 2 == 1)
            def _():
                step(kb, sa_ref, sb_ref)

            @pl.when(kb % 2 == 0)
            def _():
                step(kb, sb_ref, sa_ref)
            return carry

        lax.fori_loop(1, n_kb, body, 0)

        @pl.when(n_kb % 2 == 1)
        def _():
            values(n_kb - 1, sa_ref)

        @pl.when(n_kb % 2 == 0)
        def _():
            values(n_kb - 1, sb_ref)

        for h in range(N_HEADS):
            num = acc_ref[h * V_SLAB:h * V_SLAB + HEAD_DIM, :]
            den = acc_ref[h * V_SLAB + HEAD_DIM:h * V_SLAB + HEAD_DIM + 1, :]
            oT_ref[h * HEAD_DIM:(h + 1) * HEAD_DIM, :] = num / den

    @pl.when(has_ties)
    def _():
        attention(True)

    @pl.when(jnp.logical_not(has_ties))
    def _():
        attention(False)


def _dsa(qT, qiT, iwT, k, vT, ki, *, b, qb, lb, n_sel):
    n = qT.shape[1]
    t = n // b
    nqb = t // qb
    assert t % lb == 0 and qb == lb and qb % CHUNK == 0
    body = functools.partial(_dsa_body, qb=qb, lb=lb, sub=64, n_sel=n_sel)
    qcol = lambda height: pl.BlockSpec((height, qb), lambda bi, j: (0, bi * nqb + j))
    return pl.pallas_call(
        body,
        grid=(b, nqb),
        in_specs=[
            qcol(D_ATT), qcol(N_IDX_HEADS * IDX_DIM), qcol(N_IDX_HEADS),
            _resident((t, D_ATT), lambda bi, j: (bi, 0)),
            _resident((N_HEADS * V_SLAB, t), lambda bi, j: (0, bi)),
            _resident((t, IDX_DIM), lambda bi, j: (bi, 0)),
        ],
        out_specs=qcol(D_ATT),
        out_shape=jax.ShapeDtypeStruct((D_ATT, n), F32),
        scratch_shapes=[
            pltpu.VMEM((t, qb), F32),
            pltpu.VMEM((N_HEADS, 2 * HEAD_DIM, qb), BF16),
            pltpu.VMEM((lb, qb), F32),
            pltpu.VMEM((N_HEADS, lb, qb), F32),
            pltpu.VMEM((N_HEADS, lb, qb), F32),
            pltpu.VMEM((N_HEADS * V_SLAB, qb), F32),
            pltpu.VMEM((N_HEADS, qb), F32),
            pltpu.VMEM((N_HEADS, qb), F32),
            pltpu.VMEM((N_HEADS, qb), F32),
            pltpu.VMEM((1, qb), F32),
        ],
        compiler_params=_cparams(("parallel", "arbitrary")),
        name="dsa",
    )(qT, qiT, iwT, k, vT, ki)


def _dsa_sample_body(wq_ref, wqi_ref, iw_ref, kc_ref, vc_ref, kic_ref, kn_ref, vn_ref, kin_ref, o_ref,
                     sc_ref, s_ref, kx_ref, vx_ref, kix_ref, acc_ref,
                     *, lb, n_cache_kb, past, t_new, n_sel):
    qb = LANES
    n_kb = n_cache_kb + 1
    l_valid = past + t_new
    lane = lax.broadcasted_iota(I32, (1, qb), 1)
    qchunk = (past + (lane & (t_new - 1))) >> CHUNK_SHIFT
    full8 = lambda v: jnp.full((ACC_ROWS, qb), v, F32)

    def rows(kb):
        return pl.ds(kb * lb, lb) if isinstance(kb, int) else pl.ds(pl.multiple_of(kb * lb, lb), lb)

    def over_blocks(fn, carry, cache_ref, new_ref):
        def cached(kb, c):
            cols = pl.ds(pl.multiple_of(kb * lb, lb), lb)
            return fn(kb, c, cache_ref[0, :, cols].astype(BF16))
        return fn(n_cache_kb, lax.fori_loop(0, n_cache_kb, cached, carry), new_ref[...])

    for new_ref, stage_ref in ((kn_ref, kx_ref), (vn_ref, vx_ref), (kin_ref, kix_ref)):
        feat = stage_ref.shape[0]
        new = new_ref[0]
        if feat < LANES:
            new = jnp.concatenate([new, jnp.zeros((t_new, LANES - feat), F32)], axis=1)
        new = jnp.concatenate([new, jnp.zeros((LANES - t_new, new.shape[1]), F32)], axis=0)
        stage_ref[...] = jnp.zeros(stage_ref.shape, BF16)
        stage_ref[:, 0:LANES] = new.T[0:feat, :].astype(BF16)

    def idx_block(kb, carry, kit):
        mx8, mn8, n8 = carry
        r = jnp.maximum(jnp.dot(wqi_ref[0], kit, preferred_element_type=F32).T, 0.0) * iw_ref[0]
        for shift in (t_new, 2 * t_new, 4 * t_new):
            r = r + pltpu.roll(r, shift, 1)
        kpos = kb * lb + lax.broadcasted_iota(I32, (lb, qb), 0)
        adm = jnp.where(kpos < l_valid, kpos >> CHUNK_SHIFT, qchunk + 1) <= qchunk
        sc_ref[rows(kb), :] = jnp.where(adm, r, -INF)
        return (jnp.maximum(mx8, _fold(jnp.where(adm, r, -INF), jnp.max)),
                jnp.minimum(mn8, _fold(jnp.where(adm, r, INF), jnp.min)),
                n8 + _fold(jnp.where(adm, 1.0, 0.0), jnp.sum))

    mx8, mn8, n8 = over_blocks(idx_block, (full8(-INF), full8(INF), full8(0.0)), kic_ref, kix_ref)
    mx, mn, n_adm = _fin(mx8, jnp.max), _fin(mn8, jnp.min), _fin(n8, jnp.sum)
    if n_kb % 2 == 1:
        sc_ref[rows(n_kb), :] = jnp.full((lb, qb), -INF, F32)
    n_units, unit = (n_kb + 1) // 2, 2 * lb
    thr, tie = _select_threshold(sc_ref, n_units, unit, qb, n_sel, mx, mn, n_adm, lane >= 0)
    has_ties = _any_set(tie == 1) > 0
    thr_valid = jnp.maximum(thr, -F32_MAX)

    def attention(with_ties):
        need, tri = _tie_setup(sc_ref, n_units, unit, lb, qb, n_sel, thr) if with_ties else (None, None)

        def score_block(kb, carry, kt):
            m8, seen = carry
            bias, seen = _selection_bias(sc_ref[rows(kb), :], thr, thr_valid, need, seen, tri, with_ties)
            s = jnp.dot(wq_ref[0], kt, preferred_element_type=F32).T + bias
            s_ref[rows(kb), :] = s
            return jnp.maximum(m8, _fold(s, jnp.max)), seen

        m8, _ = over_blocks(score_block, (full8(M_FLOOR), jnp.zeros((1, qb), F32)), kc_ref, kx_ref)
        m = _fin(m8, jnp.max)
        acc_ref[...] = jnp.zeros(acc_ref.shape, F32)

        def value_block(kb, den8, vt):
            p = jnp.exp(s_ref[rows(kb), :] - m).astype(BF16)
            acc_ref[...] += jnp.dot(vt, p, preferred_element_type=F32)
            return den8 + _fold(p.astype(F32), jnp.sum)

        den = _fin(over_blocks(value_block, full8(0.0), vc_ref, vx_ref), jnp.sum)
        o_all = (acc_ref[...] / den).T
        for h in range(N_HEADS):
            rs = slice(h * t_new, (h + 1) * t_new)
            cs = slice(h * HEAD_DIM, (h + 1) * HEAD_DIM)
            o_ref[0, :, cs] = o_all[rs, cs]

    @pl.when(has_ties)
    def _():
        attention(True)

    @pl.when(jnp.logical_not(has_ties))
    def _():
        attention(False)


def _dsa_sample(q, qi, iw, k_cache, v_cache, ki_cache, layer, k_new, v_new, ki_new, *, lb, n_sel):
    b, t, _ = q.shape
    past = ki_cache.shape[2]
    assert N_HEADS * t == LANES and past % lb == 0 and LANES <= lb
    n_cache_kb = past // lb
    lp = (n_cache_kb + 1 + (n_cache_kb + 1) % 2) * lb
    eye = jnp.eye(N_HEADS, dtype=q.dtype)
    qh = q.reshape(b, t, N_HEADS, HEAD_DIM)
    wq = jnp.einsum("bqhd,gh->bhqgd", qh, eye).reshape(b, LANES, D_ATT)
    wqi = jnp.transpose(qi.reshape(b, t, N_IDX_HEADS, IDX_DIM), (0, 2, 1, 3)).reshape(b, LANES, IDX_DIM)
    iw_row = jnp.transpose(iw, (0, 2, 1)).reshape(b, 1, LANES)
    per_b = lambda shape: pl.BlockSpec((1,) + shape, lambda bi: (bi, 0, 0))
    cache = lambda shape: pl.BlockSpec((1,) + shape, lambda bi: (layer * b + bi, 0, 0))
    body = functools.partial(_dsa_sample_body, lb=lb, n_cache_kb=n_cache_kb, past=past, t_new=t, n_sel=n_sel)
    return pl.pallas_call(
        body,
        grid=(b,),
        in_specs=[per_b((LANES, D_ATT)), per_b((LANES, IDX_DIM)), per_b((1, LANES)),
                  cache((D_ATT, past)), cache((D_ATT, past)), cache((IDX_DIM, past)),
                  per_b((t, D_ATT)), per_b((t, D_ATT)), per_b((t, IDX_DIM))],
        out_specs=per_b((t, D_ATT)),
        out_shape=jax.ShapeDtypeStruct((b, t, D_ATT), F32),
        scratch_shapes=[
            pltpu.VMEM((lp, LANES), F32),
            pltpu.VMEM((lp, LANES), F32),
            pltpu.VMEM((D_ATT, lb), BF16),
            pltpu.VMEM((D_ATT, lb), BF16),
            pltpu.VMEM((IDX_DIM, lb), BF16),
            pltpu.VMEM((D_ATT, LANES), F32),
        ],
        compiler_params=_cparams(("parallel",)),
        name="dsa_sample",
    )(wq, wqi, iw_row, k_cache, v_cache, ki_cache, k_new, v_new, ki_new)


def _tail_body(x_ref, o_ref, mc_ref, ga_ref, wao_ref, wout_ref, g_ref, w1_ref, w2_ref, y_ref,
               *, ff_chunk, o_transposed):
    o = o_ref[...].T if o_transposed else o_ref[...]
    attn = jnp.dot(o.astype(BF16), wao_ref[...], preferred_element_type=F32)
    merged = mc_ref[...] + ga_ref[...] * attn
    x1 = x_ref[...] + jnp.dot(merged.astype(BF16), wout_ref[...], preferred_element_type=F32)
    ms = jnp.mean(x1 * x1, axis=-1, keepdims=True)
    h = (x1 * lax.rsqrt(ms + EPS) * g_ref[...]).astype(BF16)
    y = x1
    for c in range(D_FF // ff_chunk):
        cs = slice(c * ff_chunk, (c + 1) * ff_chunk)
        u = jnp.maximum(jnp.dot(h, w1_ref[:, cs], preferred_element_type=F32), 0.0)
        y = y + jnp.dot((u * u).astype(BF16), w2_ref[cs, :], preferred_element_type=F32)
    y_ref[...] = y


def _tail(x, o, mc, ga, wao, wout, g, w1, w2, tm, o_transposed):
    n = x.shape[0]
    row = lambda width: pl.BlockSpec((tm, width), lambda i: (i, 0))
    full = lambda a: _resident(a.shape, lambda i: (0, 0))
    o_spec = pl.BlockSpec((D_ATT, tm), lambda i: (0, i)) if o_transposed else row(D_ATT)
    return pl.pallas_call(
        functools.partial(_tail_body, ff_chunk=1024, o_transposed=o_transposed),
        grid=(n // tm,),
        in_specs=[row(D_MODEL), o_spec, row(D_MODEL), row(D_MODEL),
                  full(wao), full(wout), pl.BlockSpec((1, D_MODEL), lambda i: (0, 0)), full(w1), full(w2)],
        out_specs=row(D_MODEL),
        out_shape=jax.ShapeDtypeStruct((n, D_MODEL), F32),
        compiler_params=_cparams(("parallel",)),
        name="tail",
    )(x, o, mc, ga, wao, wout, g, w1, w2)


def _rope_tables(pos):
    inv = ROPE_THETA ** (-jnp.arange(ROT_HALF, dtype=F32) / ROT_HALF)
    ang = pos.astype(F32)[:, None] * inv[None, :]
    cos, sin = jnp.cos(ang), jnp.sin(ang)
    r = np.arange(LANES) % HEAD_DIM
    jj = r % ROT_HALF
    first = jnp.asarray(r < ROT_HALF)[None, :]
    second = jnp.asarray((r >= ROT_HALF) & (r < ROT_DIM))[None, :]
    c = jnp.where(first | second, cos[:, jj], 1.0)
    sa = jnp.where(first, -sin[:, jj], 0.0)
    sb = jnp.where(second, sin[:, jj], 0.0)
    return c, sa, sb


def _pack_w_in(w):
    pad = jnp.zeros((D_MODEL, LANES - (D_IN_HEAD - COL_KIW)), w.dtype)
    return jnp.concatenate([w[:, :D_IN_HEAD], pad, w[:, D_IN_HEAD:]], axis=1).astype(BF16)


def _tile2(v):
    return jnp.concatenate([v, v])[None, :].astype(F32)


def _layer(x, tabs, n_tab_blocks, conv_state, caches, kv_all, layer, depth, w, *, tm, conv_tm, dsa_cfg):
    b, t, _ = x.shape
    n = b * t
    prompt = caches is None
    (norm_mix, w_in_p, conv_w, conv_b, ln_g, ln_b, w_conv_out, q_norm, k_norm,
     w_attn_out, w_out, norm_ffn, w_ff1, w_ff2) = w
    xf = x.reshape(n, D_MODEL)
    glu, q, qi, gc, ga, k_all, v_all, ki_all, *extra = _inproj(
        xf, norm_mix[None, :], w_in_p, *tabs, _tile2(q_norm), _tile2(k_norm), kv_all, layer, depth, b,
        tm, n_tab_blocks, prompt)

    glu3 = glu.reshape(b, t, D_CONV)
    state_p = jnp.pad(conv_state, ((0, 0), (HALO - (CONV_W - 1), 0), (0, 0)))
    halo_src = glu3 if t >= HALO else state_p
    cw_p = jnp.pad(conv_w, ((0, HALO - CONV_W), (0, 0)))
    mc = _conv_branch(glu3, halo_src, state_p, cw_p, conv_b[None, :], ln_g[None, :], ln_b[None, :],
                      w_conv_out, gc.reshape(b, t, D_MODEL), conv_tm)
    new_conv = jnp.concatenate([conv_state, glu3], axis=1)[:, -(CONV_W - 1):]

    if prompt:
        kb, vt_ones, iwT, kib = extra
        o = _dsa(q, qi, iwT, kb, vt_ones, kib, b=b, **dsa_cfg)
    else:
        k, v, kiw = extra
        iw = kiw[:, IDX_DIM:IDX_DIM + N_IDX_HEADS].reshape(b, t, N_IDX_HEADS)
        o = _dsa_sample(q.reshape(b, t, D_ATT), qi.reshape(b, t, D_ATT), iw, *caches, layer,
                        k.reshape(b, t, D_ATT), v.reshape(b, t, D_ATT), kiw[:, :IDX_DIM].reshape(b, t, IDX_DIM),
                        **dsa_cfg).reshape(n, D_ATT)

    y = _tail(xf, o, mc.reshape(n, D_MODEL), ga, w_attn_out, w_out, norm_ffn[None, :], w_ff1, w_ff2, tm, prompt)
    return y.reshape(b, t, D_MODEL), (k_all, v_all, ki_all), new_conv


def kernel(x_prompt, x_sample, cache_k, cache_v, cache_kidx, state_conv, norm_mix, w_in, conv_w, conv_b,
           conv_ln_g, conv_ln_b, w_conv_out, q_norm, k_norm, w_attn_out, w_out, norm_ffn, w_ff1, w_ff2):
    bp, tp, _ = x_prompt.shape
    bs, ts, _ = x_sample.shape
    depth = norm_mix.shape[0]
    past = cache_k.shape[2]
    tm = 256
    tabs_p = _rope_tables(jnp.arange(tp, dtype=I32))
    tabs_s = tuple(jnp.tile(a, (bs, 1)) for a in _rope_tables(past + jnp.arange(ts, dtype=I32)))
    cfg_p = dict(qb=256, lb=256, n_sel=min(TOPK_MAX, tp // 4))
    cfg_s = dict(lb=256, n_sel=min(TOPK_MAX, (past + ts) // 4))
    caches = (jnp.transpose(cache_k, (0, 1, 3, 4, 2)).reshape(depth * bs, D_ATT, past),
              jnp.transpose(cache_v, (0, 1, 3, 4, 2)).reshape(depth * bs, D_ATT, past),
              jnp.transpose(cache_kidx, (0, 1, 3, 2)).reshape(depth * bs, IDX_DIM, past))
    kv_p = tuple(jnp.zeros(s, F32) for s in _kv_all_shapes(depth, bp, tp, True))
    kv_s = tuple(jnp.zeros(s, F32) for s in _kv_all_shapes(depth, bs, ts, False))
    hp, hs = x_prompt, x_sample
    conv_p, conv_s = [], []
    for l in range(depth):
        w = (norm_mix[l], _pack_w_in(w_in[l]), conv_w[l], conv_b[l], conv_ln_g[l], conv_ln_b[l],
             w_conv_out[l].astype(BF16), q_norm[l], k_norm[l], w_attn_out[l].astype(BF16),
             w_out[l].astype(BF16), norm_ffn[l], w_ff1[l].astype(BF16), w_ff2[l].astype(BF16))
        zero_state = jnp.zeros((bp, CONV_W - 1, D_CONV), F32)
        hp, kv_p, conv = _layer(hp, tabs_p, tp // tm, zero_state, None, kv_p, l, depth, w,
                                tm=tm, conv_tm=tm, dsa_cfg=cfg_p)
        conv_p.append(conv)
        hs, kv_s, conv = _layer(hs, tabs_s, (bs * ts) // tm, state_conv[l], caches, kv_s, l, depth, w,
                                tm=tm, conv_tm=ts, dsa_cfg=cfg_s)
        conv_s.append(conv)
    heads_p = lambda a: jnp.transpose(a.reshape(depth, bp, N_HEADS, HEAD_DIM, tp), (0, 1, 4, 2, 3))
    heads_s = lambda a: a.reshape(depth, bs, ts, N_HEADS, HEAD_DIM)
    return (hp, hs,
            heads_p(kv_p[0]), heads_p(kv_p[1]),
            jnp.transpose(kv_p[2].reshape(depth, bp, IDX_DIM, tp), (0, 1, 3, 2)), jnp.stack(conv_p),
            heads_s(kv_s[0]), heads_s(kv_s[1]), kv_s[2].reshape(depth, bs, ts, IDX_DIM), jnp.stack(conv_s))
```

```python
import functools

import jax
import jax.numpy as jnp
import numpy as np
from jax import lax
from jax.experimental import pallas as pl
from jax.experimental.pallas import tpu as pltpu

F32 = jnp.float32
BF16 = jnp.bfloat16
I32 = jnp.int32

D_MODEL = 1024
CHUNK = 64
CHUNK_SHIFT = 6
N_HEADS = 8
HEAD_DIM = 64
ROT_DIM = HEAD_DIM // 4
ROT_HALF = ROT_DIM // 2
ROPE_THETA = 500000.0
N_IDX_HEADS = 8
IDX_DIM = 64
TOPK_MAX = 256
D_CONV = 512
CONV_W = 31
D_FF = 4 * D_MODEL
EPS = 1e-6
LN_EPS = 1e-5
ATTN_SCALE = HEAD_DIM ** -0.5
IDX_SCALE = (N_IDX_HEADS ** -0.5) * (IDX_DIM ** -0.5)
D_ATT = N_HEADS * HEAD_DIM

LANES = 128
SUBLANES = 8
ACC_ROWS = 4 * SUBLANES
VMEM_LIMIT_BYTES = 56 * 1024 * 1024

COL_KIW = 6 * 512
COL_GC = COL_KIW + LANES
COL_GA = COL_GC + D_MODEL
D_IN_PACKED = COL_GA + D_MODEL
D_IN_HEAD = 2 * D_CONV + 3 * D_ATT + N_IDX_HEADS * IDX_DIM + IDX_DIM + N_IDX_HEADS

HALO = 32
INF = float("inf")
F32_MAX = float(np.finfo(np.float32).max)
ONES_ROWS = 16
V_SLAB = HEAD_DIM + ONES_ROWS
M_FLOOR = -1e30
MASK_BIAS = -2e30
SEARCH_FAST_ITERS = 48
SEARCH_MAX_ITERS = 4096


def _cparams(sem):
    return pltpu.CompilerParams(dimension_semantics=sem, vmem_limit_bytes=VMEM_LIMIT_BYTES)


def _resident(block_shape, index_map):
    return pl.BlockSpec(block_shape, index_map, pipeline_mode=pl.Buffered(1))


def _rope(x, c, sa, sb):
    return x * c + pltpu.roll(x, LANES - ROT_HALF, 1) * sa + pltpu.roll(x, ROT_HALF, 1) * sb


def _head_rms(x, gain):
    lane = lax.broadcasted_iota(I32, x.shape, 1)
    lo = lane < HEAD_DIM
    x2 = x * x
    s_lo = jnp.sum(jnp.where(lo, x2, 0.0), axis=-1, keepdims=True)
    s_hi = jnp.sum(jnp.where(lo, 0.0, x2), axis=-1, keepdims=True)
    ms = jnp.where(lo, s_lo, s_hi) * (1.0 / HEAD_DIM)
    return x * lax.rsqrt(ms + EPS) * gain


def _inproj_body(*refs, transposed, n_aliased):
    x_ref, g_ref, w_ref, cos_ref, sa_ref, sb_ref, qg_ref, kg_ref = refs[:8]
    glu_ref, q_ref, qi_ref, gc_ref, ga_ref, kall_ref, vall_ref, kiall_ref, *extra = refs[8 + n_aliased:]
    tm = x_ref.shape[0]
    x = x_ref[...]
    ms = jnp.mean(x * x, axis=-1, keepdims=True)
    h = (x * lax.rsqrt(ms + EPS) * g_ref[...]).astype(BF16)

    def proj(c0, n):
        return jnp.dot(h, w_ref[:, c0:c0 + n], preferred_element_type=F32)

    glu_ref[...] = proj(0, D_CONV) * jax.nn.sigmoid(proj(D_CONV, D_CONV))

    c, sa, sb = cos_ref[...], sa_ref[...], sb_ref[...]
    zq = proj(2 * D_CONV, D_ATT)
    zk = proj(2 * D_CONV + D_ATT, D_ATT)
    zqi = proj(2 * D_CONV + 3 * D_ATT, N_IDX_HEADS * IDX_DIM)
    zv = proj(2 * D_CONV + 2 * D_ATT, D_ATT)
    for g in range(D_ATT // LANES):
        sl = slice(g * LANES, (g + 1) * LANES)
        qh = _rope(_head_rms(zq[:, sl], qg_ref[...]), c, sa, sb) * ATTN_SCALE
        kh = _rope(_head_rms(zk[:, sl], kg_ref[...]), c, sa, sb)
        qih = _rope(zqi[:, sl], c, sa, sb)
        if transposed:
            kb_ref, vt_ref = extra[0], extra[1]
            q_ref[sl, :] = qh.T.astype(BF16)
            qi_ref[sl, :] = qih.T.astype(BF16)
            kb_ref[:, sl] = kh.astype(BF16)
            kall_ref[sl, :] = kh.T
            vt32 = zv[:, sl].T
            vall_ref[sl, :] = vt32
            vt = vt32.astype(BF16)
            ones = jnp.ones((ONES_ROWS, tm), BF16)
            for half in range(2):
                r0 = (2 * g + half) * V_SLAB
                vt_ref[r0:r0 + HEAD_DIM, :] = vt[half * HEAD_DIM:(half + 1) * HEAD_DIM, :]
                vt_ref[r0 + HEAD_DIM:r0 + V_SLAB, :] = ones
        else:
            q_ref[:, sl] = qh.astype(BF16)
            qi_ref[:, sl] = qih.astype(BF16)
            extra[0][:, sl] = kh
            extra[1][:, sl] = zv[:, sl]
            for half in range(2):
                hs = slice(half * HEAD_DIM, (half + 1) * HEAD_DIM)
                head_rows = pl.ds(2 * g + half, tm, stride=N_HEADS)
                kall_ref[head_rows, :] = kh[:, hs]
                vall_ref[head_rows, :] = zv[:, sl][:, hs]

    zkiw = proj(COL_KIW, LANES)
    lane = lax.broadcasted_iota(I32, zkiw.shape, 1)
    is_ki = lane < IDX_DIM
    roped = _rope(zkiw, jnp.where(is_ki, c, 1.0), jnp.where(is_ki, sa, 0.0), jnp.where(is_ki, sb, 0.0))
    kiw = jnp.where(is_ki, roped, zkiw * IDX_SCALE)
    if transposed:
        kiwt = kiw.T
        kiall_ref[...] = kiwt[0:IDX_DIM, :]
        extra[2][...] = kiwt[IDX_DIM:IDX_DIM + N_IDX_HEADS, :]
        extra[3][...] = kiw[:, 0:IDX_DIM].astype(BF16)
    else:
        kiall_ref[...] = kiw[:, 0:IDX_DIM]
        extra[2][...] = kiw

    gc_ref[...] = jax.nn.sigmoid(proj(COL_GC, D_MODEL))
    ga_ref[...] = jax.nn.sigmoid(proj(COL_GA, D_MODEL))


def _kv_all_shapes(depth, b, t, transposed):
    if transposed:
        return ((depth * b * D_ATT, t), (depth * b * D_ATT, t), (depth * b * IDX_DIM, t))
    return ((depth * b * t * N_HEADS, HEAD_DIM), (depth * b * t * N_HEADS, HEAD_DIM), (depth * b * t, IDX_DIM))


def _inproj(x, g, w, cos, sa, sb, qg, kg, kv_all, layer, depth, b, tm, n_tab_blocks, transposed):
    n = x.shape[0]
    t = n // b
    nb = n // tm
    nt = t // tm if transposed else None
    row = lambda width: pl.BlockSpec((tm, width), lambda i: (i, 0))
    col = lambda height: pl.BlockSpec((height, tm), lambda i: (0, i))
    tab = pl.BlockSpec((tm, LANES), lambda i: (i % n_tab_blocks, 0))
    vec = lambda width: pl.BlockSpec((1, width), lambda i: (0, 0))
    rows_of = lambda width, dt: (jax.ShapeDtypeStruct((n, width), dt), row(width))
    cols_of = lambda height, dt: (jax.ShapeDtypeStruct((height, n), dt), col(height))
    shapes = _kv_all_shapes(depth, b, t, transposed)
    if transposed:
        slab = lambda height: pl.BlockSpec((height, tm), lambda i: (layer * b + i // nt, i % nt))
        all_specs = [slab(D_ATT), slab(D_ATT), slab(IDX_DIM)]
    else:
        rows5 = lambda r, width: pl.BlockSpec((r, width), lambda i: (layer * nb + i, 0))
        all_specs = [rows5(tm * N_HEADS, HEAD_DIM), rows5(tm * N_HEADS, HEAD_DIM), rows5(tm, IDX_DIM)]
    outs = [
        rows_of(D_CONV, F32),
        cols_of(D_ATT, BF16) if transposed else rows_of(D_ATT, BF16),
        cols_of(D_ATT, BF16) if transposed else rows_of(D_ATT, BF16),
        rows_of(D_MODEL, F32),
        rows_of(D_MODEL, F32),
    ] + [(jax.ShapeDtypeStruct(s, F32), spec) for s, spec in zip(shapes, all_specs)]
    n_common = len(outs) - len(shapes)
    if transposed:
        outs += [rows_of(D_ATT, BF16), cols_of(N_HEADS * V_SLAB, BF16), cols_of(N_IDX_HEADS, F32),
                 rows_of(IDX_DIM, BF16)]
    else:
        outs += [rows_of(D_ATT, F32), rows_of(D_ATT, F32), rows_of(LANES, F32)]
    aliased = tuple(kv_all)
    n_in = 8
    return pl.pallas_call(
        functools.partial(_inproj_body, transposed=transposed, n_aliased=len(aliased)),
        grid=(nb,),
        in_specs=[row(D_MODEL), vec(D_MODEL), _resident((D_MODEL, D_IN_PACKED), lambda i: (0, 0)),
                  tab, tab, tab, vec(LANES), vec(LANES)] + [pl.BlockSpec(memory_space=pl.ANY)] * len(aliased),
        out_specs=[o[1] for o in outs],
        out_shape=[o[0] for o in outs],
        input_output_aliases={n_in + a: n_common + a for a in range(len(aliased))},
        compiler_params=_cparams(("parallel",)),
        name="inproj",
    )(x, g, w, cos, sa, sb, qg, kg, *aliased)


def _conv_body(cur_ref, halo_ref, st_ref, cw_ref, cb_ref, lg_ref, lb_ref, wco_ref, gc_ref,
               out_ref, win_ref, sh_ref, act_ref, *, tm, rows):
    i = pl.program_id(1)
    win_ref[0:HALO, :] = jnp.where(i == 0, st_ref[0], halo_ref[0])
    win_ref[HALO:HALO + tm, :] = cur_ref[0]
    span = HALO + tm - SUBLANES
    for s in range(1, SUBLANES):
        sh_ref[s, 0:span, :] = win_ref[s:s + span, :]
    off = HALO - (CONV_W - 1)
    for r in range(tm // rows):
        acc = jnp.zeros((rows, D_CONV), F32) + cb_ref[...]
        for j in range(CONV_W):
            s, base = (off + j) % SUBLANES, r * rows + (off + j) // SUBLANES * SUBLANES
            tap = win_ref[base:base + rows, :] if s == 0 else sh_ref[s, base:base + rows, :]
            acc = acc + tap * cw_ref[j:j + 1, :]
        mu = jnp.mean(acc, axis=-1, keepdims=True)
        d = acc - mu
        var = jnp.mean(d * d, axis=-1, keepdims=True)
        y = d * lax.rsqrt(var + LN_EPS) * lg_ref[...] + lb_ref[...]
        act_ref[r * rows:(r + 1) * rows, :] = (y * jax.nn.sigmoid(y)).astype(BF16)
    out_ref[0] = gc_ref[0] * jnp.dot(act_ref[...], wco_ref[...], preferred_element_type=F32)


def _conv_branch(glu, halo_src, state, cw, cb, lg, lb, wco, gc, tm):
    b, t, _ = glu.shape
    rows = min(tm, 32)
    hb = tm // HALO
    vec = pl.BlockSpec((1, D_CONV), lambda bi, i: (0, 0))
    return pl.pallas_call(
        functools.partial(_conv_body, tm=tm, rows=rows),
        grid=(b, t // tm),
        in_specs=[
            pl.BlockSpec((1, tm, D_CONV), lambda bi, i: (bi, i, 0)),
            pl.BlockSpec((1, HALO, D_CONV), lambda bi, i: (bi, jnp.maximum(i * hb - 1, 0), 0)),
            pl.BlockSpec((1, HALO, D_CONV), lambda bi, i: (bi, 0, 0)),
            pl.BlockSpec((HALO, D_CONV), lambda bi, i: (0, 0)),
            vec, vec, vec,
            _resident((D_CONV, D_MODEL), lambda bi, i: (0, 0)),
            pl.BlockSpec((1, tm, D_MODEL), lambda bi, i: (bi, i, 0)),
        ],
        out_specs=pl.BlockSpec((1, tm, D_MODEL), lambda bi, i: (bi, i, 0)),
        out_shape=jax.ShapeDtypeStruct((b, t, D_MODEL), F32),
        scratch_shapes=[pltpu.VMEM((HALO + tm, D_CONV), F32), pltpu.VMEM((SUBLANES, HALO + tm, D_CONV), F32),
                        pltpu.VMEM((tm, D_CONV), BF16)],
        compiler_params=_cparams(("parallel", "arbitrary")),
        name="conv_branch",
    )(glu, halo_src, state, cw, cb, lg, lb, wco, gc)


def _fold(x, op):
    return op(x.reshape(x.shape[0] // ACC_ROWS, ACC_ROWS, x.shape[1]), axis=0)


def _fin(x8, op):
    return op(x8, axis=0, keepdims=True)


def _any_set(flag):
    return jnp.max(jnp.where(flag, 1, 0))


def _select_threshold(sc_ref, n_units, unit, qb, n_sel, mx, mn, n_adm, lane_ok):
    kf = float(n_sel)
    log_target = float(np.log(n_sel + 0.5))
    full8 = lambda v: jnp.full((ACC_ROWS, qb), v, F32)
    row = lambda v: jnp.full((1, qb), v, F32)

    def rows(u):
        return pl.ds(pl.multiple_of(u * unit, unit), unit)

    def count2(p):
        def blk(u, c):
            x = sc_ref[rows(u), :]
            return (c[0] + _fold(jnp.where(x >= p, 1.0, 0.0), jnp.sum),
                    c[1] + _fold(jnp.where(x > p, 1.0, 0.0), jnp.sum))
        ge8, gt8 = lax.fori_loop(0, n_units, blk, (full8(0.0), full8(0.0)))
        return _fin(ge8, jnp.sum), _fin(gt8, jnp.sum)

    def scan(p, lo, up, snap):
        def blk(u, c):
            x = sc_ref[rows(u), :]
            out = [c[0] + _fold(jnp.where(x >= p, 1.0, 0.0), jnp.sum)]
            if snap:
                out.append(jnp.minimum(c[1], _fold(jnp.where(x >= lo, x, INF), jnp.min)))
                out.append(jnp.maximum(c[2], _fold(jnp.where(x < up, x, -INF), jnp.max)))
            return tuple(out)
        init = (full8(0.0), full8(INF), full8(-INF)) if snap else (full8(0.0),)
        res = lax.fori_loop(0, n_units, blk, init)
        if snap:
            return _fin(res[0], jnp.sum), _fin(res[1], jnp.min), _fin(res[2], jnp.max)
        return _fin(res[0], jnp.sum)

    def pivot(lo, c_lo, up, c_up, wl, wu, bisect):
        xu = jnp.where(up == INF, mx, up)
        gl = (jnp.log(c_lo) - log_target) * wl
        gu = (log_target - jnp.log(jnp.maximum(c_up, 0.5))) * wu
        p = jnp.where(bisect, 0.5 * lo + 0.5 * xu, lo + (xu - lo) * (gl / (gl + gu)))
        p = jnp.minimum(p, xu)
        stuck = jnp.logical_not(p > lo)
        return jnp.where(stuck, xu, p), stuck

    def update(p, c, lo, c_lo, up, c_up, live):
        ge = c >= kf
        to_lo = jnp.logical_and(live, ge)
        to_up = jnp.logical_and(live, jnp.logical_not(ge))
        return (jnp.where(to_lo, p, lo), jnp.where(to_lo, c, c_lo),
                jnp.where(to_up, p, up), jnp.where(to_up, c, c_up), to_lo, to_up)

    def count_pass(st):
        it, _, _, lo, c_lo, up, c_up, wl, wu, side, done = st
        live = done == 0
        p, stuck = pivot(lo, c_lo, up, c_up, wl, wu, it % 8 == 7)
        c = scan(p, lo, up, False)
        lo, c_lo, up, c_up, to_lo, to_up = update(p, c, lo, c_lo, up, c_up, live)
        wl = jnp.where(jnp.logical_and(to_up, side < 0.0), wl * 0.5, jnp.where(to_lo, 1.0, wl))
        wu = jnp.where(jnp.logical_and(to_lo, side > 0.0), wu * 0.5, jnp.where(to_up, 1.0, wu))
        side = jnp.where(to_lo, 1.0, jnp.where(to_up, -1.0, side))
        done = jnp.where(c_lo == kf, 1, done)
        flags = jnp.sum(jnp.where(done == 0, 1, 0) + jnp.where(jnp.logical_and(stuck, live), 1 << 16, 0))
        return (it + 1, flags & 0xFFFF, flags >> 16, lo, c_lo, up, c_up, wl, wu, side, done)

    def snap_pass(st):
        it, _, lo, c_lo, up, c_up, done, tie = st
        live = done == 0
        p, _ = pivot(lo, c_lo, up, c_up, row(1.0), row(1.0), it % 2 == 1)
        c, a, b = scan(p, lo, up, True)
        tied = jnp.logical_and(live, a == b)
        lo2, c_lo2, up2, c_up2, _, _ = update(p, c, lo, c_lo, up, c_up,
                                              jnp.logical_and(live, jnp.logical_not(tied)))
        lo2 = jnp.where(live, jnp.maximum(lo2, a), lo2)
        tie = jnp.where(tied, 1, tie)
        done = jnp.where(jnp.logical_or(tied, c_lo2 == kf), 1, done)
        return (it + 1, _any_set(done == 0), lo2, c_lo2, up2, c_up2, done, tie)

    few = n_adm <= kf
    live0 = jnp.logical_and(jnp.logical_not(few), lane_ok)
    ge0, gt0 = count2(row(0.0))
    tie0 = jnp.logical_and(live0, jnp.logical_and(gt0 < kf, ge0 >= kf))
    above = ge0 >= kf
    lo0 = jnp.where(few, -INF, jnp.where(tie0, 0.0, jnp.where(above, jnp.maximum(mn, 0.0), mn)))
    c_lo0 = jnp.where(jnp.logical_and(above, mn < 0.0), ge0, n_adm)
    up0 = jnp.where(above, INF, 0.0)
    c_up0 = jnp.where(above, 0.0, ge0)
    done0 = jnp.where(jnp.logical_and(live0, jnp.logical_not(tie0)), 0, 1)
    done0 = jnp.where(c_lo0 == kf, 1, done0)
    tie_init = jnp.where(tie0, 1, 0)
    st = (jnp.int32(0), _any_set(done0 == 0), jnp.int32(0), lo0, c_lo0, up0, c_up0,
          row(1.0), row(1.0), row(0.0), done0)
    st = lax.while_loop(lambda s: (s[0] < SEARCH_FAST_ITERS) & (s[1] > 0) & (s[2] == 0), count_pass, st)
    it1, active1, _, lo1, c_lo1, up1, c_up1, _, _, _, done1 = st
    st = (it1, active1, lo1, c_lo1, up1, c_up1, done1, tie_init)
    st = lax.while_loop(lambda s: (s[0] < SEARCH_MAX_ITERS) & (s[1] > 0), snap_pass, st)
    return st[2], st[7]


def _selection_bias(x, thr, thr_valid, need, seen, tri, with_ties):
    if not with_ties:
        return jnp.where(x >= thr_valid, 0.0, MASK_BIAS), seen
    eq = x == thr
    eqf = jnp.where(eq, 1.0, 0.0)
    pref = jnp.dot(tri, eqf.astype(BF16), preferred_element_type=F32) + seen
    take = jnp.where(eq, jnp.where(pref <= need, 1.0, 0.0), jnp.where(x > thr, 1.0, 0.0))
    bias = jnp.where(jnp.where(x > -INF, take, 0.0) > 0.5, 0.0, MASK_BIAS)
    return bias, seen + jnp.sum(eqf, axis=0, keepdims=True)


def _tie_setup(sc_ref, n_units, unit, lb, qb, n_sel, thr):
    def gt_blk(u, c):
        x = sc_ref[pl.ds(pl.multiple_of(u * unit, unit), unit), :]
        return c + _fold(jnp.where(x > thr, 1.0, 0.0), jnp.sum)
    gt = _fin(lax.fori_loop(0, n_units, gt_blk, jnp.zeros((ACC_ROWS, qb), F32)), jnp.sum)
    ri = lax.broadcasted_iota(I32, (lb, lb), 0)
    ci = lax.broadcasted_iota(I32, (lb, lb), 1)
    return float(n_sel) - gt, jnp.where(ci <= ri, 1.0, 0.0).astype(BF16)


def _dsa_body(qT_ref, qiT_ref, iwT_ref, k_ref, vT_ref, ki_ref, oT_ref,
              sc_ref, qpad_ref, bias_ref, sa_ref, sb_ref, acc_ref, m_ref, alpha_ref, bm_ref, seen_ref,
              *, qb, lb, sub, n_sel):
    j = pl.program_id(1)
    n_kb = j + 1
    lane = lax.broadcasted_iota(I32, (1, qb), 1)
    qchunk = (j * qb + lane) >> CHUNK_SHIFT
    fold, fin = _fold, _fin

    def rows(kb):
        return pl.ds(pl.multiple_of(kb * lb, lb), lb)

    def idx_block(kb, carry, diagonal):
        mx8, mn8 = carry
        r0 = pl.multiple_of(kb * lb, lb)
        for s in range(lb // sub):
            kis = ki_ref[pl.ds(r0 + s * sub, sub), :]
            score = jnp.zeros((sub, qb), F32)
            for h in range(N_IDX_HEADS):
                sh = jnp.dot(kis, qiT_ref[h * IDX_DIM:(h + 1) * IDX_DIM, :], preferred_element_type=F32)
                score = score + jnp.maximum(sh, 0.0) * iwT_ref[h:h + 1, :]
            lo_s = hi_s = score
            if diagonal:
                kpos = r0 + s * sub + lax.broadcasted_iota(I32, (sub, qb), 0)
                adm = (kpos >> CHUNK_SHIFT) <= qchunk
                hi_s, lo_s = jnp.where(adm, score, -INF), jnp.where(adm, score, INF)
            sc_ref[pl.ds(r0 + s * sub, sub), :] = hi_s
            mx8 = jnp.maximum(mx8, fold(hi_s, jnp.max))
            mn8 = jnp.minimum(mn8, fold(lo_s, jnp.min))
        return mx8, mn8

    full8 = lambda v: jnp.full((ACC_ROWS, qb), v, F32)
    mx8, mn8 = lax.fori_loop(0, j, lambda kb, c: idx_block(kb, c, False), (full8(-INF), full8(INF)))
    mx8, mn8 = idx_block(j, (mx8, mn8), True)
    mx, mn = fin(mx8, jnp.max), fin(mn8, jnp.min)
    n_adm = ((qchunk + 1) * CHUNK).astype(F32)

    n_units, unit = n_kb, lb
    thr, tie = _select_threshold(sc_ref, n_units, unit, qb, n_sel, mx, mn, n_adm, lane >= 0)
    has_ties = _any_set(tie == 1) > 0
    thr_valid = jnp.maximum(thr, -F32_MAX)

    qpad_ref[...] = jnp.zeros(qpad_ref.shape, BF16)
    for h in range(N_HEADS):
        r = (h % 2) * HEAD_DIM
        qpad_ref[h, r:r + HEAD_DIM, :] = qT_ref[h * HEAD_DIM:(h + 1) * HEAD_DIM, :]

    def attention(with_ties):
        m_ref[...] = jnp.full(m_ref.shape, M_FLOOR, F32)
        acc_ref[...] = jnp.zeros(acc_ref.shape, F32)
        seen_ref[...] = jnp.zeros(seen_ref.shape, F32)
        need, tri = _tie_setup(sc_ref, n_units, unit, lb, qb, n_sel, thr) if with_ties else (None, None)

        def select(kb):
            bias_ref[...], seen_ref[...] = _selection_bias(sc_ref[rows(kb), :], thr, thr_valid, need,
                                                           seen_ref[...], tri, with_ties)

        def scores_head(kb, s_ref, h):
            p2 = (h // 2) * 2 * HEAD_DIM
            s = jnp.dot(k_ref[rows(kb), p2:p2 + 2 * HEAD_DIM], qpad_ref[h],
                        preferred_element_type=F32) + bias_ref[...]
            s_ref[h] = s
            bm_ref[h:h + 1, :] = jnp.max(s, axis=0, keepdims=True)

        def rescale():
            m_old = m_ref[...]
            m_new = jnp.maximum(m_old, bm_ref[...])
            alpha_ref[...] = jnp.exp(m_old - m_new)
            m_ref[...] = m_new

        def values_head(kb, s_ref, h, m_new, alpha):
            p = jnp.exp(s_ref[h] - m_new[h:h + 1, :]).astype(BF16)
            vs = slice(h * V_SLAB, (h + 1) * V_SLAB)
            pv = jnp.dot(vT_ref[vs, rows(kb)], p, preferred_element_type=F32)
            acc_ref[vs, :] = acc_ref[vs, :] * alpha[h:h + 1, :] + pv

        def scores(kb, s_ref):
            select(kb)
            for h in range(N_HEADS):
                scores_head(kb, s_ref, h)
            rescale()

        def values(kb, s_ref):
            m_new, alpha = m_ref[...], alpha_ref[...]
            for h in range(N_HEADS):
                values_head(kb, s_ref, h, m_new, alpha)

        def step(kb, s_prev, s_cur):
            m_new, alpha = m_ref[...], alpha_ref[...]
            select(kb)
            for h in range(N_HEADS):
                scores_head(kb, s_cur, h)
                values_head(kb - 1, s_prev, h, m_new, alpha)
            rescale()

        scores(0, sa_ref)

        def body(kb, carry):
            @pl.when(kb % 2 == 1)
            def _():
                step(kb, sa_ref, sb_ref)

            @pl.when(kb % 2 == 0)
            def _():
                step(kb, sb_ref, sa_ref)
            return carry

        lax.fori_loop(1, n_kb, body, 0)

        @pl.when(n_kb % 2 == 1)
        def _():
            values(n_kb - 1, sa_ref)

        @pl.when(n_kb % 2 == 0)
        def _():
            values(n_kb - 1, sb_ref)

        for h in range(N_HEADS):
            num = acc_ref[h * V_SLAB:h * V_SLAB + HEAD_DIM, :]
            den = acc_ref[h * V_SLAB + HEAD_DIM:h * V_SLAB + HEAD_DIM + 1, :]
            oT_ref[h * HEAD_DIM:(h + 1) * HEAD_DIM, :] = num / den

    @pl.when(has_ties)
    def _():
        attention(True)

    @pl.when(jnp.logical_not(has_ties))
    def _():
        attention(False)


def _dsa(qT, qiT, iwT, k, vT, ki, *, b, qb, lb, n_sel):
    n = qT.shape[1]
    t = n // b
    nqb = t // qb
    assert t % lb == 0 and qb == lb and qb % CHUNK == 0
    body = functools.partial(_dsa_body, qb=qb, lb=lb, sub=64, n_sel=n_sel)
    qcol = lambda height: pl.BlockSpec((height, qb), lambda bi, j: (0, bi * nqb + j))
    return pl.pallas_call(
        body,
        grid=(b, nqb),
        in_specs=[
            qcol(D_ATT), qcol(N_IDX_HEADS * IDX_DIM), qcol(N_IDX_HEADS),
            _resident((t, D_ATT), lambda bi, j: (bi, 0)),
            _resident((N_HEADS * V_SLAB, t), lambda bi, j: (0, bi)),
            _resident((t, IDX_DIM), lambda bi, j: (bi, 0)),
        ],
        out_specs=qcol(D_ATT),
        out_shape=jax.ShapeDtypeStruct((D_ATT, n), F32),
        scratch_shapes=[
            pltpu.VMEM((t, qb), F32),
            pltpu.VMEM((N_HEADS, 2 * HEAD_DIM, qb), BF16),
            pltpu.VMEM((lb, qb), F32),
            pltpu.VMEM((N_HEADS, lb, qb), F32),
            pltpu.VMEM((N_HEADS, lb, qb), F32),
            pltpu.VMEM((N_HEADS * V_SLAB, qb), F32),
            pltpu.VMEM((N_HEADS, qb), F32),
            pltpu.VMEM((N_HEADS, qb), F32),
            pltpu.VMEM((N_HEADS, qb), F32),
            pltpu.VMEM((1, qb), F32),
        ],
        compiler_params=_cparams(("parallel", "arbitrary")),
        name="dsa",
    )(qT, qiT, iwT, k, vT, ki)


def _dsa_sample_body(wq_ref, wqi_ref, iw_ref, kc_ref, vc_ref, kic_ref, kn_ref, vn_ref, kin_ref, o_ref,
                     sc_ref, s_ref, kx_ref, vx_ref, kix_ref, acc_ref,
                     *, lb, n_cache_kb, past, t_new, n_sel):
    qb = LANES
    n_kb = n_cache_kb + 1
    l_valid = past + t_new
    lane = lax.broadcasted_iota(I32, (1, qb), 1)
    qchunk = (past + (lane & (t_new - 1))) >> CHUNK_SHIFT
    full8 = lambda v: jnp.full((ACC_ROWS, qb), v, F32)

    def rows(kb):
        return pl.ds(kb * lb, lb) if isinstance(kb, int) else pl.ds(pl.multiple_of(kb * lb, lb), lb)

    def over_blocks(fn, carry, cache_ref, new_ref):
        def cached(kb, c):
            cols = pl.ds(pl.multiple_of(kb * lb, lb), lb)
            return fn(kb, c, cache_ref[0, :, cols].astype(BF16))
        return fn(n_cache_kb, lax.fori_loop(0, n_cache_kb, cached, carry, unroll=4), new_ref[...])

    for new_ref, stage_ref in ((kn_ref, kx_ref), (vn_ref, vx_ref), (kin_ref, kix_ref)):
        feat = stage_ref.shape[0]
        new = new_ref[0]
        if feat < LANES:
            new = jnp.concatenate([new, jnp.zeros((t_new, LANES - feat), F32)], axis=1)
        new = jnp.concatenate([new, jnp.zeros((LANES - t_new, new.shape[1]), F32)], axis=0)
        stage_ref[...] = jnp.zeros(stage_ref.shape, BF16)
        stage_ref[:, 0:LANES] = new.T[0:feat, :].astype(BF16)

    def idx_block(kb, carry, kit):
        mx8, mn8, n8 = carry
        r = jnp.maximum(jnp.dot(wqi_ref[0], kit, preferred_element_type=F32).T, 0.0) * iw_ref[0]
        for shift in (t_new, 2 * t_new, 4 * t_new):
            r = r + pltpu.roll(r, shift, 1)
        kpos = kb * lb + lax.broadcasted_iota(I32, (lb, qb), 0)
        adm = jnp.where(kpos < l_valid, kpos >> CHUNK_SHIFT, qchunk + 1) <= qchunk
        sc_ref[rows(kb), :] = jnp.where(adm, r, -INF)
        return (jnp.maximum(mx8, _fold(jnp.where(adm, r, -INF), jnp.max)),
                jnp.minimum(mn8, _fold(jnp.where(adm, r, INF), jnp.min)),
                n8 + _fold(jnp.where(adm, 1.0, 0.0), jnp.sum))

    mx8, mn8, n8 = over_blocks(idx_block, (full8(-INF), full8(INF), full8(0.0)), kic_ref, kix_ref)
    mx, mn, n_adm = _fin(mx8, jnp.max), _fin(mn8, jnp.min), _fin(n8, jnp.sum)
    if n_kb % 2 == 1:
        sc_ref[rows(n_kb), :] = jnp.full((lb, qb), -INF, F32)
    n_units, unit = (n_kb + 1) // 2, 2 * lb
    thr, tie = _select_threshold(sc_ref, n_units, unit, qb, n_sel, mx, mn, n_adm, lane >= 0)
    has_ties = _any_set(tie == 1) > 0
    thr_valid = jnp.maximum(thr, -F32_MAX)

    def attention(with_ties):
        need, tri = _tie_setup(sc_ref, n_units, unit, lb, qb, n_sel, thr) if with_ties else (None, None)

        def score_block(kb, carry, kt):
            m8, seen = carry
            bias, seen = _selection_bias(sc_ref[rows(kb), :], thr, thr_valid, need, seen, tri, with_ties)
            s = jnp.dot(wq_ref[0], kt, preferred_element_type=F32).T + bias
            s_ref[rows(kb), :] = s
            return jnp.maximum(m8, _fold(s, jnp.max)), seen

        m8, _ = over_blocks(score_block, (full8(M_FLOOR), jnp.zeros((1, qb), F32)), kc_ref, kx_ref)
        m = _fin(m8, jnp.max)
        acc_ref[...] = jnp.zeros(acc_ref.shape, F32)

        def value_block(kb, den8, vt):
            p = jnp.exp(s_ref[rows(kb), :] - m).astype(BF16)
            acc_ref[...] += jnp.dot(vt, p, preferred_element_type=F32)
            return den8 + _fold(p.astype(F32), jnp.sum)

        den = _fin(over_blocks(value_block, full8(0.0), vc_ref, vx_ref), jnp.sum)
        o_all = (acc_ref[...] / den).T
        for h in range(N_HEADS):
            rs = slice(h * t_new, (h + 1) * t_new)
            cs = slice(h * HEAD_DIM, (h + 1) * HEAD_DIM)
            o_ref[0, :, cs] = o_all[rs, cs]

    @pl.when(has_ties)
    def _():
        attention(True)

    @pl.when(jnp.logical_not(has_ties))
    def _():
        attention(False)


def _dsa_sample(q, qi, iw, k_cache, v_cache, ki_cache, layer, k_new, v_new, ki_new, *, lb, n_sel):
    b, t, _ = q.shape
    past = ki_cache.shape[2]
    assert N_HEADS * t == LANES and past % lb == 0 and LANES <= lb
    n_cache_kb = past // lb
    lp = (n_cache_kb + 1 + (n_cache_kb + 1) % 2) * lb
    eye = jnp.eye(N_HEADS, dtype=q.dtype)
    qh = q.reshape(b, t, N_HEADS, HEAD_DIM)
    wq = jnp.einsum("bqhd,gh->bhqgd", qh, eye).reshape(b, LANES, D_ATT)
    wqi = jnp.transpose(qi.reshape(b, t, N_IDX_HEADS, IDX_DIM), (0, 2, 1, 3)).reshape(b, LANES, IDX_DIM)
    iw_row = jnp.transpose(iw, (0, 2, 1)).reshape(b, 1, LANES)
    per_b = lambda shape: pl.BlockSpec((1,) + shape, lambda bi: (bi, 0, 0))
    cache = lambda shape: pl.BlockSpec((1,) + shape, lambda bi: (layer * b + bi, 0, 0))
    body = functools.partial(_dsa_sample_body, lb=lb, n_cache_kb=n_cache_kb, past=past, t_new=t, n_sel=n_sel)
    return pl.pallas_call(
        body,
        grid=(b,),
        in_specs=[per_b((LANES, D_ATT)), per_b((LANES, IDX_DIM)), per_b((1, LANES)),
                  cache((D_ATT, past)), cache((D_ATT, past)), cache((IDX_DIM, past)),
                  per_b((t, D_ATT)), per_b((t, D_ATT)), per_b((t, IDX_DIM))],
        out_specs=per_b((t, D_ATT)),
        out_shape=jax.ShapeDtypeStruct((b, t, D_ATT), F32),
        scratch_shapes=[
            pltpu.VMEM((lp, LANES), F32),
            pltpu.VMEM((lp, LANES), F32),
            pltpu.VMEM((D_ATT, lb), BF16),
            pltpu.VMEM((D_ATT, lb), BF16),
            pltpu.VMEM((IDX_DIM, lb), BF16),
            pltpu.VMEM((D_ATT, LANES), F32),
        ],
        compiler_params=_cparams(("parallel",)),
        name="dsa_sample",
    )(wq, wqi, iw_row, k_cache, v_cache, ki_cache, k_new, v_new, ki_new)


def _tail_body(x_ref, o_ref, mc_ref, ga_ref, wao_ref, wout_ref, g_ref, w1_ref, w2_ref, y_ref,
               *, ff_chunk, o_transposed):
    o = o_ref[...].T if o_transposed else o_ref[...]
    attn = jnp.dot(o.astype(BF16), wao_ref[...], preferred_element_type=F32)
    merged = mc_ref[...] + ga_ref[...] * attn
    x1 = x_ref[...] + jnp.dot(merged.astype(BF16), wout_ref[...], preferred_element_type=F32)
    ms = jnp.mean(x1 * x1, axis=-1, keepdims=True)
    h = (x1 * lax.rsqrt(ms + EPS) * g_ref[...]).astype(BF16)
    y = x1
    for c in range(D_FF // ff_chunk):
        cs = slice(c * ff_chunk, (c + 1) * ff_chunk)
        u = jnp.maximum(jnp.dot(h, w1_ref[:, cs], preferred_element_type=F32), 0.0)
        y = y + jnp.dot((u * u).astype(BF16), w2_ref[cs, :], preferred_element_type=F32)
    y_ref[...] = y


def _tail(x, o, mc, ga, wao, wout, g, w1, w2, tm, o_transposed):
    n = x.shape[0]
    row = lambda width: pl.BlockSpec((tm, width), lambda i: (i, 0))
    full = lambda a: _resident(a.shape, lambda i: (0, 0))
    o_spec = pl.BlockSpec((D_ATT, tm), lambda i: (0, i)) if o_transposed else row(D_ATT)
    return pl.pallas_call(
        functools.partial(_tail_body, ff_chunk=1024, o_transposed=o_transposed),
        grid=(n // tm,),
        in_specs=[row(D_MODEL), o_spec, row(D_MODEL), row(D_MODEL),
                  full(wao), full(wout), pl.BlockSpec((1, D_MODEL), lambda i: (0, 0)), full(w1), full(w2)],
        out_specs=row(D_MODEL),
        out_shape=jax.ShapeDtypeStruct((n, D_MODEL), F32),
        compiler_params=_cparams(("parallel",)),
        name="tail",
    )(x, o, mc, ga, wao, wout, g, w1, w2)


def _rope_tables(pos):
    inv = ROPE_THETA ** (-jnp.arange(ROT_HALF, dtype=F32) / ROT_HALF)
    ang = pos.astype(F32)[:, None] * inv[None, :]
    cos, sin = jnp.cos(ang), jnp.sin(ang)
    r = np.arange(LANES) % HEAD_DIM
    jj = r % ROT_HALF
    first = jnp.asarray(r < ROT_HALF)[None, :]
    second = jnp.asarray((r >= ROT_HALF) & (r < ROT_DIM))[None, :]
    c = jnp.where(first | second, cos[:, jj], 1.0)
    sa = jnp.where(first, -sin[:, jj], 0.0)
    sb = jnp.where(second, sin[:, jj], 0.0)
    return c, sa, sb


def _pack_w_in(w):
    pad = jnp.zeros((D_MODEL, LANES - (D_IN_HEAD - COL_KIW)), w.dtype)
    return jnp.concatenate([w[:, :D_IN_HEAD], pad, w[:, D_IN_HEAD:]], axis=1).astype(BF16)


def _tile2(v):
    return jnp.concatenate([v, v])[None, :].astype(F32)


def _layer(x, tabs, n_tab_blocks, conv_state, caches, kv_all, layer, depth, w, *, tm, conv_tm, dsa_cfg):
    b, t, _ = x.shape
    n = b * t
    prompt = caches is None
    (norm_mix, w_in_p, conv_w, conv_b, ln_g, ln_b, w_conv_out, q_norm, k_norm,
     w_attn_out, w_out, norm_ffn, w_ff1, w_ff2) = w
    xf = x.reshape(n, D_MODEL)
    glu, q, qi, gc, ga, k_all, v_all, ki_all, *extra = _inproj(
        xf, norm_mix[None, :], w_in_p, *tabs, _tile2(q_norm), _tile2(k_norm), kv_all, layer, depth, b,
        tm, n_tab_blocks, prompt)

    glu3 = glu.reshape(b, t, D_CONV)
    state_p = jnp.pad(conv_state, ((0, 0), (HALO - (CONV_W - 1), 0), (0, 0)))
    halo_src = glu3 if t >= HALO else state_p
    cw_p = jnp.pad(conv_w, ((0, HALO - CONV_W), (0, 0)))
    mc = _conv_branch(glu3, halo_src, state_p, cw_p, conv_b[None, :], ln_g[None, :], ln_b[None, :],
                      w_conv_out, gc.reshape(b, t, D_MODEL), conv_tm)
    new_conv = jnp.concatenate([conv_state, glu3], axis=1)[:, -(CONV_W - 1):]

    if prompt:
        kb, vt_ones, iwT, kib = extra
        o = _dsa(q, qi, iwT, kb, vt_ones, kib, b=b, **dsa_cfg)
    else:
        k, v, kiw = extra
        iw = kiw[:, IDX_DIM:IDX_DIM + N_IDX_HEADS].reshape(b, t, N_IDX_HEADS)
        o = _dsa_sample(q.reshape(b, t, D_ATT), qi.reshape(b, t, D_ATT), iw, *caches, layer,
                        k.reshape(b, t, D_ATT), v.reshape(b, t, D_ATT), kiw[:, :IDX_DIM].reshape(b, t, IDX_DIM),
                        **dsa_cfg).reshape(n, D_ATT)

    y = _tail(xf, o, mc.reshape(n, D_MODEL), ga, w_attn_out, w_out, norm_ffn[None, :], w_ff1, w_ff2, tm, prompt)
    return y.reshape(b, t, D_MODEL), (k_all, v_all, ki_all), new_conv


def kernel(x_prompt, x_sample, cache_k, cache_v, cache_kidx, state_conv, norm_mix, w_in, conv_w, conv_b,
           conv_ln_g, conv_ln_b, w_conv_out, q_norm, k_norm, w_attn_out, w_out, norm_ffn, w_ff1, w_ff2):
    bp, tp, _ = x_prompt.shape
    bs, ts, _ = x_sample.shape
    depth = norm_mix.shape[0]
    past = cache_k.shape[2]
    tm = 256
    tabs_p = _rope_tables(jnp.arange(tp, dtype=I32))
    tabs_s = tuple(jnp.tile(a, (bs, 1)) for a in _rope_tables(past + jnp.arange(ts, dtype=I32)))
    cfg_p = dict(qb=256, lb=256, n_sel=min(TOPK_MAX, tp // 4))
    cfg_s = dict(lb=256, n_sel=min(TOPK_MAX, (past + ts) // 4))
    caches = (jnp.transpose(cache_k, (0, 1, 3, 4, 2)).reshape(depth * bs, D_ATT, past),
              jnp.transpose(cache_v, (0, 1, 3, 4, 2)).reshape(depth * bs, D_ATT, past),
              jnp.transpose(cache_kidx, (0, 1, 3, 2)).reshape(depth * bs, IDX_DIM, past))
    kv_p = tuple(jnp.zeros(s, F32) for s in _kv_all_shapes(depth, bp, tp, True))
    kv_s = tuple(jnp.zeros(s, F32) for s in _kv_all_shapes(depth, bs, ts, False))
    hp, hs = x_prompt, x_sample
    conv_p, conv_s = [], []
    for l in range(depth):
        w = (norm_mix[l], _pack_w_in(w_in[l]), conv_w[l], conv_b[l], conv_ln_g[l], conv_ln_b[l],
             w_conv_out[l].astype(BF16), q_norm[l], k_norm[l], w_attn_out[l].astype(BF16),
             w_out[l].astype(BF16), norm_ffn[l], w_ff1[l].astype(BF16), w_ff2[l].astype(BF16))
        zero_state = jnp.zeros((bp, CONV_W - 1, D_CONV), F32)
        hp, kv_p, conv = _layer(hp, tabs_p, tp // tm, zero_state, None, kv_p, l, depth, w,
                                tm=tm, conv_tm=tm, dsa_cfg=cfg_p)
        conv_p.append(conv)
        hs, kv_s, conv = _layer(hs, tabs_s, (bs * ts) // tm, state_conv[l], caches, kv_s, l, depth, w,
                                tm=tm, conv_tm=ts, dsa_cfg=cfg_s)
        conv_s.append(conv)
    heads_p = lambda a: jnp.transpose(a.reshape(depth, bp, N_HEADS, HEAD_DIM, tp), (0, 1, 4, 2, 3))
    heads_s = lambda a: a.reshape(depth, bs, ts, N_HEADS, HEAD_DIM)
    return (hp, hs,
            heads_p(kv_p[0]), heads_p(kv_p[1]),
            jnp.transpose(kv_p[2].reshape(depth, bp, IDX_DIM, tp), (0, 1, 3, 2)), jnp.stack(conv_p),
            heads_s(kv_s[0]), heads_s(kv_s[1]), kv_s[2].reshape(depth, bs, ts, IDX_DIM), jnp.stack(conv_s))
```

```python
import functools

import jax
import jax.numpy as jnp
import numpy as np
from jax import lax
from jax.experimental import pallas as pl
from jax.experimental.pallas import tpu as pltpu

F32 = jnp.float32
BF16 = jnp.bfloat16
I32 = jnp.int32

D_MODEL = 1024
CHUNK = 64
CHUNK_SHIFT = 6
N_HEADS = 8
HEAD_DIM = 64
ROT_DIM = HEAD_DIM // 4
ROT_HALF = ROT_DIM // 2
ROPE_THETA = 500000.0
N_IDX_HEADS = 8
IDX_DIM = 64
TOPK_MAX = 256
D_CONV = 512
CONV_W = 31
D_FF = 4 * D_MODEL
EPS = 1e-6
LN_EPS = 1e-5
ATTN_SCALE = HEAD_DIM ** -0.5
IDX_SCALE = (N_IDX_HEADS ** -0.5) * (IDX_DIM ** -0.5)
D_ATT = N_HEADS * HEAD_DIM

LANES = 128
SUBLANES = 8
ACC_ROWS = 4 * SUBLANES
VMEM_LIMIT_BYTES = 56 * 1024 * 1024

COL_KIW = 6 * 512
COL_GC = COL_KIW + LANES
COL_GA = COL_GC + D_MODEL
D_IN_PACKED = COL_GA + D_MODEL
D_IN_HEAD = 2 * D_CONV + 3 * D_ATT + N_IDX_HEADS * IDX_DIM + IDX_DIM + N_IDX_HEADS

HALO = 32
INF = float("inf")
F32_MAX = float(np.finfo(np.float32).max)
ONES_ROWS = 16
V_SLAB = HEAD_DIM + ONES_ROWS
M_FLOOR = -1e30
MASK_BIAS = -2e30
SEARCH_FAST_ITERS = 48
SEARCH_MAX_ITERS = 4096


def _cparams(sem):
    return pltpu.CompilerParams(dimension_semantics=sem, vmem_limit_bytes=VMEM_LIMIT_BYTES)


def _resident(block_shape, index_map):
    return pl.BlockSpec(block_shape, index_map, pipeline_mode=pl.Buffered(1))


def _rope(x, c, sa, sb):
    return x * c + pltpu.roll(x, LANES - ROT_HALF, 1) * sa + pltpu.roll(x, ROT_HALF, 1) * sb


def _head_rms(x, gain):
    lane = lax.broadcasted_iota(I32, x.shape, 1)
    lo = lane < HEAD_DIM
    x2 = x * x
    s_lo = jnp.sum(jnp.where(lo, x2, 0.0), axis=-1, keepdims=True)
    s_hi = jnp.sum(jnp.where(lo, 0.0, x2), axis=-1, keepdims=True)
    ms = jnp.where(lo, s_lo, s_hi) * (1.0 / HEAD_DIM)
    return x * lax.rsqrt(ms + EPS) * gain


def _inproj_body(*refs, transposed, n_aliased):
    x_ref, g_ref, w_ref, cos_ref, sa_ref, sb_ref, qg_ref, kg_ref = refs[:8]
    glu_ref, q_ref, qi_ref, gc_ref, ga_ref, kall_ref, vall_ref, kiall_ref, *extra = refs[8 + n_aliased:]
    tm = x_ref.shape[0]
    x = x_ref[...]
    ms = jnp.mean(x * x, axis=-1, keepdims=True)
    h = (x * lax.rsqrt(ms + EPS) * g_ref[...]).astype(BF16)

    def proj(c0, n):
        return jnp.dot(h, w_ref[:, c0:c0 + n], preferred_element_type=F32)

    glu_ref[...] = proj(0, D_CONV) * jax.nn.sigmoid(proj(D_CONV, D_CONV))

    c, sa, sb = cos_ref[...], sa_ref[...], sb_ref[...]
    zq = proj(2 * D_CONV, D_ATT)
    zk = proj(2 * D_CONV + D_ATT, D_ATT)
    zqi = proj(2 * D_CONV + 3 * D_ATT, N_IDX_HEADS * IDX_DIM)
    zv = proj(2 * D_CONV + 2 * D_ATT, D_ATT)
    for g in range(D_ATT // LANES):
        sl = slice(g * LANES, (g + 1) * LANES)
        qh = _rope(_head_rms(zq[:, sl], qg_ref[...]), c, sa, sb) * ATTN_SCALE
        kh = _rope(_head_rms(zk[:, sl], kg_ref[...]), c, sa, sb)
        qih = _rope(zqi[:, sl], c, sa, sb)
        if transposed:
            kb_ref, vt_ref = extra[0], extra[1]
            q_ref[sl, :] = qh.T.astype(BF16)
            qi_ref[sl, :] = qih.T.astype(BF16)
            kb_ref[:, sl] = kh.astype(BF16)
            kall_ref[sl, :] = kh.T
            vt32 = zv[:, sl].T
            vall_ref[sl, :] = vt32
            vt = vt32.astype(BF16)
            ones = jnp.ones((ONES_ROWS, tm), BF16)
            for half in range(2):
                r0 = (2 * g + half) * V_SLAB
                vt_ref[r0:r0 + HEAD_DIM, :] = vt[half * HEAD_DIM:(half + 1) * HEAD_DIM, :]
                vt_ref[r0 + HEAD_DIM:r0 + V_SLAB, :] = ones
        else:
            q_ref[:, sl] = qh.astype(BF16)
            qi_ref[:, sl] = qih.astype(BF16)
            extra[0][:, sl] = kh
            extra[1][:, sl] = zv[:, sl]
            for half in range(2):
                hs = slice(half * HEAD_DIM, (half + 1) * HEAD_DIM)
                head_rows = pl.ds(2 * g + half, tm, stride=N_HEADS)
                kall_ref[head_rows, :] = kh[:, hs]
                vall_ref[head_rows, :] = zv[:, sl][:, hs]

    zkiw = proj(COL_KIW, LANES)
    lane = lax.broadcasted_iota(I32, zkiw.shape, 1)
    is_ki = lane < IDX_DIM
    roped = _rope(zkiw, jnp.where(is_ki, c, 1.0), jnp.where(is_ki, sa, 0.0), jnp.where(is_ki, sb, 0.0))
    kiw = jnp.where(is_ki, roped, zkiw * IDX_SCALE)
    if transposed:
        kiwt = kiw.T
        kiall_ref[...] = kiwt[0:IDX_DIM, :]
        extra[2][...] = kiwt[IDX_DIM:IDX_DIM + N_IDX_HEADS, :]
        extra[3][...] = kiw[:, 0:IDX_DIM].astype(BF16)
    else:
        kiall_ref[...] = kiw[:, 0:IDX_DIM]
        extra[2][...] = kiw

    gc_ref[...] = jax.nn.sigmoid(proj(COL_GC, D_MODEL))
    ga_ref[...] = jax.nn.sigmoid(proj(COL_GA, D_MODEL))


def _kv_all_shapes(depth, b, t, transposed):
    if transposed:
        return ((depth * b * D_ATT, t), (depth * b * D_ATT, t), (depth * b * IDX_DIM, t))
    return ((depth * b * t * N_HEADS, HEAD_DIM), (depth * b * t * N_HEADS, HEAD_DIM), (depth * b * t, IDX_DIM))


def _inproj(x, g, w, cos, sa, sb, qg, kg, kv_all, layer, depth, b, tm, n_tab_blocks, transposed):
    n = x.shape[0]
    t = n // b
    nb = n // tm
    nt = t // tm if transposed else None
    row = lambda width: pl.BlockSpec((tm, width), lambda i: (i, 0))
    col = lambda height: pl.BlockSpec((height, tm), lambda i: (0, i))
    tab = pl.BlockSpec((tm, LANES), lambda i: (i % n_tab_blocks, 0))
    vec = lambda width: pl.BlockSpec((1, width), lambda i: (0, 0))
    rows_of = lambda width, dt: (jax.ShapeDtypeStruct((n, width), dt), row(width))
    cols_of = lambda height, dt: (jax.ShapeDtypeStruct((height, n), dt), col(height))
    shapes = _kv_all_shapes(depth, b, t, transposed)
    if transposed:
        slab = lambda height: pl.BlockSpec((height, tm), lambda i: (layer * b + i // nt, i % nt))
        all_specs = [slab(D_ATT), slab(D_ATT), slab(IDX_DIM)]
    else:
        rows5 = lambda r, width: pl.BlockSpec((r, width), lambda i: (layer * nb + i, 0))
        all_specs = [rows5(tm * N_HEADS, HEAD_DIM), rows5(tm * N_HEADS, HEAD_DIM), rows5(tm, IDX_DIM)]
    outs = [
        rows_of(D_CONV, F32),
        cols_of(D_ATT, BF16) if transposed else rows_of(D_ATT, BF16),
        cols_of(D_ATT, BF16) if transposed else rows_of(D_ATT, BF16),
        rows_of(D_MODEL, F32),
        rows_of(D_MODEL, F32),
    ] + [(jax.ShapeDtypeStruct(s, F32), spec) for s, spec in zip(shapes, all_specs)]
    n_common = len(outs) - len(shapes)
    if transposed:
        outs += [rows_of(D_ATT, BF16), cols_of(N_HEADS * V_SLAB, BF16), cols_of(N_IDX_HEADS, F32),
                 rows_of(IDX_DIM, BF16)]
    else:
        outs += [rows_of(D_ATT, F32), rows_of(D_ATT, F32), rows_of(LANES, F32)]
    aliased = tuple(kv_all)
    n_in = 8
    return pl.pallas_call(
        functools.partial(_inproj_body, transposed=transposed, n_aliased=len(aliased)),
        grid=(nb,),
        in_specs=[row(D_MODEL), vec(D_MODEL), _resident((D_MODEL, D_IN_PACKED), lambda i: (0, 0)),
                  tab, tab, tab, vec(LANES), vec(LANES)] + [pl.BlockSpec(memory_space=pl.ANY)] * len(aliased),
        out_specs=[o[1] for o in outs],
        out_shape=[o[0] for o in outs],
        input_output_aliases={n_in + a: n_common + a for a in range(len(aliased))},
        compiler_params=_cparams(("parallel",)),
        name="inproj",
    )(x, g, w, cos, sa, sb, qg, kg, *aliased)


def _conv_body(cur_ref, halo_ref, st_ref, cw_ref, cb_ref, lg_ref, lb_ref, wco_ref, gc_ref,
               out_ref, win_ref, sh_ref, act_ref, *, tm, rows):
    i = pl.program_id(1)
    win_ref[0:HALO, :] = jnp.where(i == 0, st_ref[0], halo_ref[0])
    win_ref[HALO:HALO + tm, :] = cur_ref[0]
    span = HALO + tm - SUBLANES
    for s in range(1, SUBLANES):
        sh_ref[s, 0:span, :] = win_ref[s:s + span, :]
    off = HALO - (CONV_W - 1)
    for r in range(tm // rows):
        acc = jnp.zeros((rows, D_CONV), F32) + cb_ref[...]
        for j in range(CONV_W):
            s, base = (off + j) % SUBLANES, r * rows + (off + j) // SUBLANES * SUBLANES
            tap = win_ref[base:base + rows, :] if s == 0 else sh_ref[s, base:base + rows, :]
            acc = acc + tap * cw_ref[j:j + 1, :]
        mu = jnp.mean(acc, axis=-1, keepdims=True)
        d = acc - mu
        var = jnp.mean(d * d, axis=-1, keepdims=True)
        y = d * lax.rsqrt(var + LN_EPS) * lg_ref[...] + lb_ref[...]
        act_ref[r * rows:(r + 1) * rows, :] = (y * jax.nn.sigmoid(y)).astype(BF16)
    out_ref[0] = gc_ref[0] * jnp.dot(act_ref[...], wco_ref[...], preferred_element_type=F32)


def _conv_branch(glu, halo_src, state, cw, cb, lg, lb, wco, gc, tm):
    b, t, _ = glu.shape
    rows = min(tm, 32)
    hb = tm // HALO
    vec = pl.BlockSpec((1, D_CONV), lambda bi, i: (0, 0))
    return pl.pallas_call(
        functools.partial(_conv_body, tm=tm, rows=rows),
        grid=(b, t // tm),
        in_specs=[
            pl.BlockSpec((1, tm, D_CONV), lambda bi, i: (bi, i, 0)),
            pl.BlockSpec((1, HALO, D_CONV), lambda bi, i: (bi, jnp.maximum(i * hb - 1, 0), 0)),
            pl.BlockSpec((1, HALO, D_CONV), lambda bi, i: (bi, 0, 0)),
            pl.BlockSpec((HALO, D_CONV), lambda bi, i: (0, 0)),
            vec, vec, vec,
            _resident((D_CONV, D_MODEL), lambda bi, i: (0, 0)),
            pl.BlockSpec((1, tm, D_MODEL), lambda bi, i: (bi, i, 0)),
        ],
        out_specs=pl.BlockSpec((1, tm, D_MODEL), lambda bi, i: (bi, i, 0)),
        out_shape=jax.ShapeDtypeStruct((b, t, D_MODEL), F32),
        scratch_shapes=[pltpu.VMEM((HALO + tm, D_CONV), F32), pltpu.VMEM((SUBLANES, HALO + tm, D_CONV), F32),
                        pltpu.VMEM((tm, D_CONV), BF16)],
        compiler_params=_cparams(("parallel", "arbitrary")),
        name="conv_branch",
    )(glu, halo_src, state, cw, cb, lg, lb, wco, gc)


def _fold(x, op):
    return op(x.reshape(x.shape[0] // ACC_ROWS, ACC_ROWS, x.shape[1]), axis=0)


def _fin(x8, op):
    return op(x8, axis=0, keepdims=True)


def _any_set(flag):
    return jnp.max(jnp.where(flag, 1, 0))


def _select_threshold(sc_ref, n_units, unit, qb, n_sel, stats, lane_ok):
    mx, mn, n_adm, ge0, gt0 = stats
    kf = float(n_sel)
    log_target = float(np.log(n_sel + 0.5))
    full8 = lambda v: jnp.full((ACC_ROWS, qb), v, F32)
    row = lambda v: jnp.full((1, qb), v, F32)

    def rows(u):
        return pl.ds(pl.multiple_of(u * unit, unit), unit)

    def scan(p, lo, up, snap):
        def blk(u, c):
            x = sc_ref[rows(u), :]
            out = [c[0] + _fold(jnp.where(x >= p, 1.0, 0.0), jnp.sum)]
            if snap:
                out.append(jnp.minimum(c[1], _fold(jnp.where(x >= lo, x, INF), jnp.min)))
                out.append(jnp.maximum(c[2], _fold(jnp.where(x < up, x, -INF), jnp.max)))
            return tuple(out)
        init = (full8(0.0), full8(INF), full8(-INF)) if snap else (full8(0.0),)
        res = lax.fori_loop(0, n_units, blk, init)
        if snap:
            return _fin(res[0], jnp.sum), _fin(res[1], jnp.min), _fin(res[2], jnp.max)
        return _fin(res[0], jnp.sum)

    def pivot(lo, c_lo, up, c_up, wl, wu, bisect):
        xu = jnp.where(up == INF, mx, up)
        gl = (jnp.log(c_lo) - log_target) * wl
        gu = (log_target - jnp.log(jnp.maximum(c_up, 0.5))) * wu
        p = jnp.where(bisect, 0.5 * lo + 0.5 * xu, lo + (xu - lo) * (gl / (gl + gu)))
        p = jnp.minimum(p, xu)
        stuck = jnp.logical_not(p > lo)
        return jnp.where(stuck, xu, p), stuck

    def update(p, c, lo, c_lo, up, c_up, live):
        ge = c >= kf
        to_lo = jnp.logical_and(live, ge)
        to_up = jnp.logical_and(live, jnp.logical_not(ge))
        return (jnp.where(to_lo, p, lo), jnp.where(to_lo, c, c_lo),
                jnp.where(to_up, p, up), jnp.where(to_up, c, c_up), to_lo, to_up)

    def count_pass(st):
        it, _, _, lo, c_lo, up, c_up, wl, wu, side, done = st
        live = done == 0
        p, stuck = pivot(lo, c_lo, up, c_up, wl, wu, it % 8 == 7)
        c = scan(p, lo, up, False)
        lo, c_lo, up, c_up, to_lo, to_up = update(p, c, lo, c_lo, up, c_up, live)
        wl = jnp.where(jnp.logical_and(to_up, side < 0.0), wl * 0.5, jnp.where(to_lo, 1.0, wl))
        wu = jnp.where(jnp.logical_and(to_lo, side > 0.0), wu * 0.5, jnp.where(to_up, 1.0, wu))
        side = jnp.where(to_lo, 1.0, jnp.where(to_up, -1.0, side))
        done = jnp.where(c_lo == kf, 1, done)
        flags = jnp.sum(jnp.where(done == 0, 1, 0) + jnp.where(jnp.logical_and(stuck, live), 1 << 16, 0))
        return (it + 1, flags & 0xFFFF, flags >> 16, lo, c_lo, up, c_up, wl, wu, side, done)

    def snap_pass(st):
        it, _, lo, c_lo, up, c_up, done, tie = st
        live = done == 0
        p, _ = pivot(lo, c_lo, up, c_up, row(1.0), row(1.0), it % 2 == 1)
        c, a, b = scan(p, lo, up, True)
        tied = jnp.logical_and(live, a == b)
        lo2, c_lo2, up2, c_up2, _, _ = update(p, c, lo, c_lo, up, c_up,
                                              jnp.logical_and(live, jnp.logical_not(tied)))
        lo2 = jnp.where(live, jnp.maximum(lo2, a), lo2)
        tie = jnp.where(tied, 1, tie)
        done = jnp.where(jnp.logical_or(tied, c_lo2 == kf), 1, done)
        return (it + 1, _any_set(done == 0), lo2, c_lo2, up2, c_up2, done, tie)

    few = n_adm <= kf
    live0 = jnp.logical_and(jnp.logical_not(few), lane_ok)
    tie0 = jnp.logical_and(live0, jnp.logical_and(gt0 < kf, ge0 >= kf))
    above = ge0 >= kf
    lo0 = jnp.where(few, -INF, jnp.where(tie0, 0.0, jnp.where(above, jnp.maximum(mn, 0.0), mn)))
    c_lo0 = jnp.where(jnp.logical_and(above, mn < 0.0), ge0, n_adm)
    up0 = jnp.where(above, INF, 0.0)
    c_up0 = jnp.where(above, 0.0, ge0)
    done0 = jnp.where(jnp.logical_and(live0, jnp.logical_not(tie0)), 0, 1)
    done0 = jnp.where(c_lo0 == kf, 1, done0)
    tie_init = jnp.where(tie0, 1, 0)
    st = (jnp.int32(0), _any_set(done0 == 0), jnp.int32(0), lo0, c_lo0, up0, c_up0,
          row(1.0), row(1.0), row(0.0), done0)
    st = lax.while_loop(lambda s: (s[0] < SEARCH_FAST_ITERS) & (s[1] > 0) & (s[2] == 0), count_pass, st)
    it1, active1, _, lo1, c_lo1, up1, c_up1, _, _, _, done1 = st
    st = (it1, active1, lo1, c_lo1, up1, c_up1, done1, tie_init)
    st = lax.while_loop(lambda s: (s[0] < SEARCH_MAX_ITERS) & (s[1] > 0), snap_pass, st)
    return st[2], st[7]


def _selection_bias(x, thr, thr_valid, need, seen, tri, with_ties):
    if not with_ties:
        return jnp.where(x >= thr_valid, 0.0, MASK_BIAS), seen
    eq = x == thr
    eqf = jnp.where(eq, 1.0, 0.0)
    pref = jnp.dot(tri, eqf.astype(BF16), preferred_element_type=F32) + seen
    take = jnp.where(eq, jnp.where(pref <= need, 1.0, 0.0), jnp.where(x > thr, 1.0, 0.0))
    bias = jnp.where(jnp.where(x > -INF, take, 0.0) > 0.5, 0.0, MASK_BIAS)
    return bias, seen + jnp.sum(eqf, axis=0, keepdims=True)


def _tie_setup(sc_ref, n_units, unit, lb, qb, n_sel, thr):
    def gt_blk(u, c):
        x = sc_ref[pl.ds(pl.multiple_of(u * unit, unit), unit), :]
        return c + _fold(jnp.where(x > thr, 1.0, 0.0), jnp.sum)
    gt = _fin(lax.fori_loop(0, n_units, gt_blk, jnp.zeros((ACC_ROWS, qb), F32)), jnp.sum)
    ri = lax.broadcasted_iota(I32, (lb, lb), 0)
    ci = lax.broadcasted_iota(I32, (lb, lb), 1)
    return float(n_sel) - gt, jnp.where(ci <= ri, 1.0, 0.0).astype(BF16)


def _dsa_body(qT_ref, qiT_ref, iwT_ref, k_ref, vT_ref, ki_ref, oT_ref,
              sc_ref, qpad_ref, bias_ref, sa_ref, sb_ref, acc_ref, m_ref, alpha_ref, bm_ref, seen_ref,
              *, qb, lb, sub, n_sel):
    j = pl.program_id(1)
    n_kb = j + 1
    lane = lax.broadcasted_iota(I32, (1, qb), 1)
    qchunk = (j * qb + lane) >> CHUNK_SHIFT
    fold, fin = _fold, _fin

    def rows(kb):
        return pl.ds(pl.multiple_of(kb * lb, lb), lb)

    def idx_block(kb, carry, diagonal):
        mx8, mn8, ge8, gt8 = carry
        r0 = pl.multiple_of(kb * lb, lb)
        for s in range(lb // sub):
            kis = ki_ref[pl.ds(r0 + s * sub, sub), :]
            score = jnp.zeros((sub, qb), F32)
            for h in range(N_IDX_HEADS):
                sh = jnp.dot(kis, qiT_ref[h * IDX_DIM:(h + 1) * IDX_DIM, :], preferred_element_type=F32)
                score = score + jnp.maximum(sh, 0.0) * iwT_ref[h:h + 1, :]
            lo_s = hi_s = score
            if diagonal:
                kpos = r0 + s * sub + lax.broadcasted_iota(I32, (sub, qb), 0)
                adm = (kpos >> CHUNK_SHIFT) <= qchunk
                hi_s, lo_s = jnp.where(adm, score, -INF), jnp.where(adm, score, INF)
            sc_ref[pl.ds(r0 + s * sub, sub), :] = hi_s
            mx8 = jnp.maximum(mx8, fold(hi_s, jnp.max))
            mn8 = jnp.minimum(mn8, fold(lo_s, jnp.min))
            ge8 = ge8 + fold(jnp.where(hi_s >= 0.0, 1.0, 0.0), jnp.sum)
            gt8 = gt8 + fold(jnp.where(hi_s > 0.0, 1.0, 0.0), jnp.sum)
        return mx8, mn8, ge8, gt8

    full8 = lambda v: jnp.full((ACC_ROWS, qb), v, F32)
    acc8 = lax.fori_loop(0, j, lambda kb, c: idx_block(kb, c, False),
                         (full8(-INF), full8(INF), full8(0.0), full8(0.0)))
    mx8, mn8, ge8, gt8 = idx_block(j, acc8, True)
    n_adm = ((qchunk + 1) * CHUNK).astype(F32)
    stats = (fin(mx8, jnp.max), fin(mn8, jnp.min), n_adm, fin(ge8, jnp.sum), fin(gt8, jnp.sum))

    n_units, unit = n_kb, lb
    thr, tie = _select_threshold(sc_ref, n_units, unit, qb, n_sel, stats, lane >= 0)
    has_ties = _any_set(tie == 1) > 0
    thr_valid = jnp.maximum(thr, -F32_MAX)

    qpad_ref[...] = jnp.zeros(qpad_ref.shape, BF16)
    for h in range(N_HEADS):
        r = (h % 2) * HEAD_DIM
        qpad_ref[h, r:r + HEAD_DIM, :] = qT_ref[h * HEAD_DIM:(h + 1) * HEAD_DIM, :]

    def attention(with_ties):
        m_ref[...] = jnp.full(m_ref.shape, M_FLOOR, F32)
        acc_ref[...] = jnp.zeros(acc_ref.shape, F32)
        seen_ref[...] = jnp.zeros(seen_ref.shape, F32)
        need, tri = _tie_setup(sc_ref, n_units, unit, lb, qb, n_sel, thr) if with_ties else (None, None)

        def select(kb):
            bias_ref[...], seen_ref[...] = _selection_bias(sc_ref[rows(kb), :], thr, thr_valid, need,
                                                           seen_ref[...], tri, with_ties)

        def scores_head(kb, s_ref, h):
            p2 = (h // 2) * 2 * HEAD_DIM
            s = jnp.dot(k_ref[rows(kb), p2:p2 + 2 * HEAD_DIM], qpad_ref[h],
                        preferred_element_type=F32) + bias_ref[...]
            s_ref[h] = s
            bm_ref[h:h + 1, :] = jnp.max(s, axis=0, keepdims=True)

        def rescale():
            m_old = m_ref[...]
            m_new = jnp.maximum(m_old, bm_ref[...])
            alpha_ref[...] = jnp.exp(m_old - m_new)
            m_ref[...] = m_new

        def values_head(kb, s_ref, h, m_new, alpha):
            p = jnp.exp(s_ref[h] - m_new[h:h + 1, :]).astype(BF16)
            vs = slice(h * V_SLAB, (h + 1) * V_SLAB)
            pv = jnp.dot(vT_ref[vs, rows(kb)], p, preferred_element_type=F32)
            acc_ref[vs, :] = acc_ref[vs, :] * alpha[h:h + 1, :] + pv

        def scores(kb, s_ref):
            select(kb)
            for h in range(N_HEADS):
                scores_head(kb, s_ref, h)
            rescale()

        def values(kb, s_ref):
            m_new, alpha = m_ref[...], alpha_ref[...]
            for h in range(N_HEADS):
                values_head(kb, s_ref, h, m_new, alpha)

        def step(kb, s_prev, s_cur):
            m_new, alpha = m_ref[...], alpha_ref[...]
            select(kb)
            for h in range(N_HEADS):
                scores_head(kb, s_cur, h)
                values_head(kb - 1, s_prev, h, m_new, alpha)
            rescale()

        scores(0, sa_ref)

        def body(kb, carry):
            @pl.when(kb % 2 == 1)
            def _():
                step(kb, sa_ref, sb_ref)

            @pl.when(kb % 2 == 0)
            def _():
                step(kb, sb_ref, sa_ref)
            return carry

        lax.fori_loop(1, n_kb, body, 0)

        @pl.when(n_kb % 2 == 1)
        def _():
            values(n_kb - 1, sa_ref)

        @pl.when(n_kb % 2 == 0)
        def _():
            values(n_kb - 1, sb_ref)

        for h in range(N_HEADS):
            num = acc_ref[h * V_SLAB:h * V_SLAB + HEAD_DIM, :]
            den = acc_ref[h * V_SLAB + HEAD_DIM:h * V_SLAB + HEAD_DIM + 1, :]
            oT_ref[h * HEAD_DIM:(h + 1) * HEAD_DIM, :] = num / den

    @pl.when(has_ties)
    def _():
        attention(True)

    @pl.when(jnp.logical_not(has_ties))
    def _():
        attention(False)


def _dsa(qT, qiT, iwT, k, vT, ki, *, b, qb, lb, n_sel):
    n = qT.shape[1]
    t = n // b
    nqb = t // qb
    assert t % lb == 0 and qb == lb and qb % CHUNK == 0
    body = functools.partial(_dsa_body, qb=qb, lb=lb, sub=64, n_sel=n_sel)
    qcol = lambda height: pl.BlockSpec((height, qb), lambda bi, j: (0, bi * nqb + j))
    return pl.pallas_call(
        body,
        grid=(b, nqb),
        in_specs=[
            qcol(D_ATT), qcol(N_IDX_HEADS * IDX_DIM), qcol(N_IDX_HEADS),
            _resident((t, D_ATT), lambda bi, j: (bi, 0)),
            _resident((N_HEADS * V_SLAB, t), lambda bi, j: (0, bi)),
            _resident((t, IDX_DIM), lambda bi, j: (bi, 0)),
        ],
        out_specs=qcol(D_ATT),
        out_shape=jax.ShapeDtypeStruct((D_ATT, n), F32),
        scratch_shapes=[
            pltpu.VMEM((t, qb), F32),
            pltpu.VMEM((N_HEADS, 2 * HEAD_DIM, qb), BF16),
            pltpu.VMEM((lb, qb), F32),
            pltpu.VMEM((N_HEADS, lb, qb), F32),
            pltpu.VMEM((N_HEADS, lb, qb), F32),
            pltpu.VMEM((N_HEADS * V_SLAB, qb), F32),
            pltpu.VMEM((N_HEADS, qb), F32),
            pltpu.VMEM((N_HEADS, qb), F32),
            pltpu.VMEM((N_HEADS, qb), F32),
            pltpu.VMEM((1, qb), F32),
        ],
        compiler_params=_cparams(("parallel", "arbitrary")),
        name="dsa",
    )(qT, qiT, iwT, k, vT, ki)


def _dsa_sample_body(wq_ref, wqi_ref, iw_ref, kc_ref, vc_ref, kic_ref, kn_ref, vn_ref, kin_ref, o_ref,
                     sc_ref, s_ref, kx_ref, vx_ref, kix_ref, acc_ref,
                     *, lb, n_cache_kb, past, t_new, n_sel):
    qb = LANES
    n_kb = n_cache_kb + 1
    l_valid = past + t_new
    lane = lax.broadcasted_iota(I32, (1, qb), 1)
    qchunk = (past + (lane & (t_new - 1))) >> CHUNK_SHIFT
    full8 = lambda v: jnp.full((ACC_ROWS, qb), v, F32)

    def rows(kb):
        return pl.ds(kb * lb, lb) if isinstance(kb, int) else pl.ds(pl.multiple_of(kb * lb, lb), lb)

    def over_blocks(fn, carry, cache_ref, new_ref):
        def cached(kb, c):
            cols = pl.ds(pl.multiple_of(kb * lb, lb), lb)
            return fn(kb, c, cache_ref[0, :, cols].astype(BF16))
        return fn(n_cache_kb, lax.fori_loop(0, n_cache_kb, cached, carry, unroll=4), new_ref[...])

    for new_ref, stage_ref in ((kn_ref, kx_ref), (vn_ref, vx_ref), (kin_ref, kix_ref)):
        feat = stage_ref.shape[0]
        new = new_ref[0]
        if feat < LANES:
            new = jnp.concatenate([new, jnp.zeros((t_new, LANES - feat), F32)], axis=1)
        new = jnp.concatenate([new, jnp.zeros((LANES - t_new, new.shape[1]), F32)], axis=0)
        stage_ref[...] = jnp.zeros(stage_ref.shape, BF16)
        stage_ref[:, 0:LANES] = new.T[0:feat, :].astype(BF16)

    def idx_block(kb, carry, kit):
        mx8, mn8, n8, ge8, gt8 = carry
        r = jnp.maximum(jnp.dot(wqi_ref[0], kit, preferred_element_type=F32).T, 0.0) * iw_ref[0]
        for shift in (t_new, 2 * t_new, 4 * t_new):
            r = r + pltpu.roll(r, shift, 1)
        kpos = kb * lb + lax.broadcasted_iota(I32, (lb, qb), 0)
        adm = jnp.where(kpos < l_valid, kpos >> CHUNK_SHIFT, qchunk + 1) <= qchunk
        hi_s = jnp.where(adm, r, -INF)
        sc_ref[rows(kb), :] = hi_s
        return (jnp.maximum(mx8, _fold(hi_s, jnp.max)),
                jnp.minimum(mn8, _fold(jnp.where(adm, r, INF), jnp.min)),
                n8 + _fold(jnp.where(adm, 1.0, 0.0), jnp.sum),
                ge8 + _fold(jnp.where(hi_s >= 0.0, 1.0, 0.0), jnp.sum),
                gt8 + _fold(jnp.where(hi_s > 0.0, 1.0, 0.0), jnp.sum))

    acc8 = over_blocks(idx_block, (full8(-INF), full8(INF), full8(0.0), full8(0.0), full8(0.0)), kic_ref, kix_ref)
    stats = tuple(_fin(a, op) for a, op in zip(acc8, (jnp.max, jnp.min, jnp.sum, jnp.sum, jnp.sum)))
    if n_kb % 2 == 1:
        sc_ref[rows(n_kb), :] = jnp.full((lb, qb), -INF, F32)
    n_units, unit = (n_kb + 1) // 2, 2 * lb
    thr, tie = _select_threshold(sc_ref, n_units, unit, qb, n_sel, stats, lane >= 0)
    has_ties = _any_set(tie == 1) > 0
    thr_valid = jnp.maximum(thr, -F32_MAX)

    def attention(with_ties):
        need, tri = _tie_setup(sc_ref, n_units, unit, lb, qb, n_sel, thr) if with_ties else (None, None)

        def score_block(kb, carry, kt):
            m8, seen = carry
            bias, seen = _selection_bias(sc_ref[rows(kb), :], thr, thr_valid, need, seen, tri, with_ties)
            s = jnp.dot(wq_ref[0], kt, preferred_element_type=F32).T + bias
            s_ref[rows(kb), :] = s
            return jnp.maximum(m8, _fold(s, jnp.max)), seen

        m8, _ = over_blocks(score_block, (full8(M_FLOOR), jnp.zeros((1, qb), F32)), kc_ref, kx_ref)
        m = _fin(m8, jnp.max)
        acc_ref[...] = jnp.zeros(acc_ref.shape, F32)

        def value_block(kb, den8, vt):
            p = jnp.exp(s_ref[rows(kb), :] - m).astype(BF16)
            acc_ref[...] += jnp.dot(vt, p, preferred_element_type=F32)
            return den8 + _fold(p.astype(F32), jnp.sum)

        den = _fin(over_blocks(value_block, full8(0.0), vc_ref, vx_ref), jnp.sum)
        o_all = (acc_ref[...] / den).T
        for h in range(N_HEADS):
            rs = slice(h * t_new, (h + 1) * t_new)
            cs = slice(h * HEAD_DIM, (h + 1) * HEAD_DIM)
            o_ref[0, :, cs] = o_all[rs, cs]

    @pl.when(has_ties)
    def _():
        attention(True)

    @pl.when(jnp.logical_not(has_ties))
    def _():
        attention(False)


def _dsa_sample(q, qi, iw, k_cache, v_cache, ki_cache, layer, k_new, v_new, ki_new, *, lb, n_sel):
    b, t, _ = q.shape
    past = ki_cache.shape[2]
    assert N_HEADS * t == LANES and past % lb == 0 and LANES <= lb
    n_cache_kb = past // lb
    lp = (n_cache_kb + 1 + (n_cache_kb + 1) % 2) * lb
    eye = jnp.eye(N_HEADS, dtype=q.dtype)
    qh = q.reshape(b, t, N_HEADS, HEAD_DIM)
    wq = jnp.einsum("bqhd,gh->bhqgd", qh, eye).reshape(b, LANES, D_ATT)
    wqi = jnp.transpose(qi.reshape(b, t, N_IDX_HEADS, IDX_DIM), (0, 2, 1, 3)).reshape(b, LANES, IDX_DIM)
    iw_row = jnp.transpose(iw, (0, 2, 1)).reshape(b, 1, LANES)
    per_b = lambda shape: pl.BlockSpec((1,) + shape, lambda bi: (bi, 0, 0))
    cache = lambda shape: pl.BlockSpec((1,) + shape, lambda bi: (layer * b + bi, 0, 0))
    body = functools.partial(_dsa_sample_body, lb=lb, n_cache_kb=n_cache_kb, past=past, t_new=t, n_sel=n_sel)
    return pl.pallas_call(
        body,
        grid=(b,),
        in_specs=[per_b((LANES, D_ATT)), per_b((LANES, IDX_DIM)), per_b((1, LANES)),
                  cache((D_ATT, past)), cache((D_ATT, past)), cache((IDX_DIM, past)),
                  per_b((t, D_ATT)), per_b((t, D_ATT)), per_b((t, IDX_DIM))],
        out_specs=per_b((t, D_ATT)),
        out_shape=jax.ShapeDtypeStruct((b, t, D_ATT), F32),
        scratch_shapes=[
            pltpu.VMEM((lp, LANES), F32),
            pltpu.VMEM((lp, LANES), F32),
            pltpu.VMEM((D_ATT, lb), BF16),
            pltpu.VMEM((D_ATT, lb), BF16),
            pltpu.VMEM((IDX_DIM, lb), BF16),
            pltpu.VMEM((D_ATT, LANES), F32),
        ],
        compiler_params=_cparams(("parallel",)),
        name="dsa_sample",
    )(wq, wqi, iw_row, k_cache, v_cache, ki_cache, k_new, v_new, ki_new)


def _tail_body(x_ref, o_ref, mc_ref, ga_ref, wao_ref, wout_ref, g_ref, w1_ref, w2_ref, y_ref,
               *, ff_chunk, o_transposed):
    o = o_ref[...].T if o_transposed else o_ref[...]
    attn = jnp.dot(o.astype(BF16), wao_ref[...], preferred_element_type=F32)
    merged = mc_ref[...] + ga_ref[...] * attn
    x1 = x_ref[...] + jnp.dot(merged.astype(BF16), wout_ref[...], preferred_element_type=F32)
    ms = jnp.mean(x1 * x1, axis=-1, keepdims=True)
    h = (x1 * lax.rsqrt(ms + EPS) * g_ref[...]).astype(BF16)
    y = x1
    for c in range(D_FF // ff_chunk):
        cs = slice(c * ff_chunk, (c + 1) * ff_chunk)
        u = jnp.maximum(jnp.dot(h, w1_ref[:, cs], preferred_element_type=F32), 0.0)
        y = y + jnp.dot((u * u).astype(BF16), w2_ref[cs, :], preferred_element_type=F32)
    y_ref[...] = y


def _tail(x, o, mc, ga, wao, wout, g, w1, w2, tm, o_transposed):
    n = x.shape[0]
    row = lambda width: pl.BlockSpec((tm, width), lambda i: (i, 0))
    full = lambda a: _resident(a.shape, lambda i: (0, 0))
    o_spec = pl.BlockSpec((D_ATT, tm), lambda i: (0, i)) if o_transposed else row(D_ATT)
    return pl.pallas_call(
        functools.partial(_tail_body, ff_chunk=1024, o_transposed=o_transposed),
        grid=(n // tm,),
        in_specs=[row(D_MODEL), o_spec, row(D_MODEL), row(D_MODEL),
                  full(wao), full(wout), pl.BlockSpec((1, D_MODEL), lambda i: (0, 0)), full(w1), full(w2)],
        out_specs=row(D_MODEL),
        out_shape=jax.ShapeDtypeStruct((n, D_MODEL), F32),
        compiler_params=_cparams(("parallel",)),
        name="tail",
    )(x, o, mc, ga, wao, wout, g, w1, w2)


def _rope_tables(pos):
    inv = ROPE_THETA ** (-jnp.arange(ROT_HALF, dtype=F32) / ROT_HALF)
    ang = pos.astype(F32)[:, None] * inv[None, :]
    cos, sin = jnp.cos(ang), jnp.sin(ang)
    r = np.arange(LANES) % HEAD_DIM
    jj = r % ROT_HALF
    first = jnp.asarray(r < ROT_HALF)[None, :]
    second = jnp.asarray((r >= ROT_HALF) & (r < ROT_DIM))[None, :]
    c = jnp.where(first | second, cos[:, jj], 1.0)
    sa = jnp.where(first, -sin[:, jj], 0.0)
    sb = jnp.where(second, sin[:, jj], 0.0)
    return c, sa, sb


def _pack_w_in(w):
    pad = jnp.zeros((D_MODEL, LANES - (D_IN_HEAD - COL_KIW)), w.dtype)
    return jnp.concatenate([w[:, :D_IN_HEAD], pad, w[:, D_IN_HEAD:]], axis=1).astype(BF16)


def _tile2(v):
    return jnp.concatenate([v, v])[None, :].astype(F32)


def _layer(x, tabs, n_tab_blocks, conv_state, caches, kv_all, layer, depth, w, *, tm, conv_tm, dsa_cfg):
    b, t, _ = x.shape
    n = b * t
    prompt = caches is None
    (norm_mix, w_in_p, conv_w, conv_b, ln_g, ln_b, w_conv_out, q_norm, k_norm,
     w_attn_out, w_out, norm_ffn, w_ff1, w_ff2) = w
    xf = x.reshape(n, D_MODEL)
    glu, q, qi, gc, ga, k_all, v_all, ki_all, *extra = _inproj(
        xf, norm_mix[None, :], w_in_p, *tabs, _tile2(q_norm), _tile2(k_norm), kv_all, layer, depth, b,
        tm, n_tab_blocks, prompt)

    glu3 = glu.reshape(b, t, D_CONV)
    state_p = jnp.pad(conv_state, ((0, 0), (HALO - (CONV_W - 1), 0), (0, 0)))
    halo_src = glu3 if t >= HALO else state_p
    cw_p = jnp.pad(conv_w, ((0, HALO - CONV_W), (0, 0)))
    mc = _conv_branch(glu3, halo_src, state_p, cw_p, conv_b[None, :], ln_g[None, :], ln_b[None, :],
                      w_conv_out, gc.reshape(b, t, D_MODEL), conv_tm)
    new_conv = jnp.concatenate([conv_state, glu3], axis=1)[:, -(CONV_W - 1):]

    if prompt:
        kb, vt_ones, iwT, kib = extra
        o = _dsa(q, qi, iwT, kb, vt_ones, kib, b=b, **dsa_cfg)
    else:
        k, v, kiw = extra
        iw = kiw[:, IDX_DIM:IDX_DIM + N_IDX_HEADS].reshape(b, t, N_IDX_HEADS)
        o = _dsa_sample(q.reshape(b, t, D_ATT), qi.reshape(b, t, D_ATT), iw, *caches, layer,
                        k.reshape(b, t, D_ATT), v.reshape(b, t, D_ATT), kiw[:, :IDX_DIM].reshape(b, t, IDX_DIM),
                        **dsa_cfg).reshape(n, D_ATT)

    y = _tail(xf, o, mc.reshape(n, D_MODEL), ga, w_attn_out, w_out, norm_ffn[None, :], w_ff1, w_ff2, tm, prompt)
    return y.reshape(b, t, D_MODEL), (k_all, v_all, ki_all), new_conv


def kernel(x_prompt, x_sample, cache_k, cache_v, cache_kidx, state_conv, norm_mix, w_in, conv_w, conv_b,
           conv_ln_g, conv_ln_b, w_conv_out, q_norm, k_norm, w_attn_out, w_out, norm_ffn, w_ff1, w_ff2):
    bp, tp, _ = x_prompt.shape
    bs, ts, _ = x_sample.shape
    depth = norm_mix.shape[0]
    past = cache_k.shape[2]
    tm = 256
    tabs_p = _rope_tables(jnp.arange(tp, dtype=I32))
    tabs_s = tuple(jnp.tile(a, (bs, 1)) for a in _rope_tables(past + jnp.arange(ts, dtype=I32)))
    cfg_p = dict(qb=256, lb=256, n_sel=min(TOPK_MAX, tp // 4))
    cfg_s = dict(lb=256, n_sel=min(TOPK_MAX, (past + ts) // 4))
    caches = (jnp.transpose(cache_k, (0, 1, 3, 4, 2)).reshape(depth * bs, D_ATT, past),
              jnp.transpose(cache_v, (0, 1, 3, 4, 2)).reshape(depth * bs, D_ATT, past),
              jnp.transpose(cache_kidx, (0, 1, 3, 2)).reshape(depth * bs, IDX_DIM, past))
    kv_p = tuple(jnp.zeros(s, F32) for s in _kv_all_shapes(depth, bp, tp, True))
    kv_s = tuple(jnp.zeros(s, F32) for s in _kv_all_shapes(depth, bs, ts, False))
    hp, hs = x_prompt, x_sample
    conv_p, conv_s = [], []
    for l in range(depth):
        w = (norm_mix[l], _pack_w_in(w_in[l]), conv_w[l], conv_b[l], conv_ln_g[l], conv_ln_b[l],
             w_conv_out[l].astype(BF16), q_norm[l], k_norm[l], w_attn_out[l].astype(BF16),
             w_out[l].astype(BF16), norm_ffn[l], w_ff1[l].astype(BF16), w_ff2[l].astype(BF16))
        zero_state = jnp.zeros((bp, CONV_W - 1, D_CONV), F32)
        hp, kv_p, conv = _layer(hp, tabs_p, tp // tm, zero_state, None, kv_p, l, depth, w,
                                tm=tm, conv_tm=tm, dsa_cfg=cfg_p)
        conv_p.append(conv)
        hs, kv_s, conv = _layer(hs, tabs_s, (bs * ts) // tm, state_conv[l], caches, kv_s, l, depth, w,
                                tm=tm, conv_tm=ts, dsa_cfg=cfg_s)
        conv_s.append(conv)
    heads_p = lambda a: jnp.transpose(a.reshape(depth, bp, N_HEADS, HEAD_DIM, tp), (0, 1, 4, 2, 3))
    heads_s = lambda a: a.reshape(depth, bs, ts, N_HEADS, HEAD_DIM)
    return (hp, hs,
            heads_p(kv_p[0]), heads_p(kv_p[1]),
            jnp.transpose(kv_p[2].reshape(depth, bp, IDX_DIM, tp), (0, 1, 3, 2)), jnp.stack(conv_p),
            heads_s(kv_s[0]), heads_s(kv_s[1]), kv_s[2].reshape(depth, bs, ts, IDX_DIM), jnp.stack(conv_s))
```

```python
import functools

import jax
import jax.numpy as jnp
import numpy as np
from jax import lax
from jax.experimental import pallas as pl
from jax.experimental.pallas import tpu as pltpu

F32 = jnp.float32
BF16 = jnp.bfloat16
I32 = jnp.int32

D_MODEL = 1024
CHUNK = 64
CHUNK_SHIFT = 6
N_HEADS = 8
HEAD_DIM = 64
ROT_DIM = HEAD_DIM // 4
ROT_HALF = ROT_DIM // 2
ROPE_THETA = 500000.0
N_IDX_HEADS = 8
IDX_DIM = 64
TOPK_MAX = 256
D_CONV = 512
CONV_W = 31
D_FF = 4 * D_MODEL
EPS = 1e-6
LN_EPS = 1e-5
ATTN_SCALE = HEAD_DIM ** -0.5
IDX_SCALE = (N_IDX_HEADS ** -0.5) * (IDX_DIM ** -0.5)
D_ATT = N_HEADS * HEAD_DIM

LANES = 128
SUBLANES = 8
ACC_ROWS = 4 * SUBLANES
VMEM_LIMIT_BYTES = 56 * 1024 * 1024

COL_KIW = 6 * 512
COL_GC = COL_KIW + LANES
COL_GA = COL_GC + D_MODEL
D_IN_PACKED = COL_GA + D_MODEL
D_IN_HEAD = 2 * D_CONV + 3 * D_ATT + N_IDX_HEADS * IDX_DIM + IDX_DIM + N_IDX_HEADS

HALO = 32
INF = float("inf")
F32_MAX = float(np.finfo(np.float32).max)
ONES_ROWS = 16
V_SLAB = HEAD_DIM + ONES_ROWS
M_FLOOR = -1e30
MASK_BIAS = -2e30
SEARCH_FAST_ITERS = 48
SEARCH_MAX_ITERS = 4096
BISECT_EVERY = 16
STALE_END_DECAY = 0.5
STUCK_FLAG_SHIFT = 16


def _cparams(sem):
    return pltpu.CompilerParams(dimension_semantics=sem, vmem_limit_bytes=VMEM_LIMIT_BYTES)


def _resident(block_shape, index_map):
    return pl.BlockSpec(block_shape, index_map, pipeline_mode=pl.Buffered(1))


def _rope(x, c, sa, sb):
    return x * c + pltpu.roll(x, LANES - ROT_HALF, 1) * sa + pltpu.roll(x, ROT_HALF, 1) * sb


def _head_rms(x, gain):
    lane = lax.broadcasted_iota(I32, x.shape, 1)
    lo = lane < HEAD_DIM
    x2 = x * x
    s_lo = jnp.sum(jnp.where(lo, x2, 0.0), axis=-1, keepdims=True)
    s_hi = jnp.sum(jnp.where(lo, 0.0, x2), axis=-1, keepdims=True)
    ms = jnp.where(lo, s_lo, s_hi) * (1.0 / HEAD_DIM)
    return x * lax.rsqrt(ms + EPS) * gain


def _inproj_body(*refs, transposed, n_aliased):
    x_ref, g_ref, w_ref, cos_ref, sa_ref, sb_ref, qg_ref, kg_ref = refs[:8]
    glu_ref, q_ref, qi_ref, gc_ref, ga_ref, kall_ref, vall_ref, kiall_ref, *extra = refs[8 + n_aliased:]
    tm = x_ref.shape[0]
    x = x_ref[...]
    ms = jnp.mean(x * x, axis=-1, keepdims=True)
    h = (x * lax.rsqrt(ms + EPS) * g_ref[...]).astype(BF16)

    def proj(c0, n):
        return jnp.dot(h, w_ref[:, c0:c0 + n], preferred_element_type=F32)

    glu_ref[...] = proj(0, D_CONV) * jax.nn.sigmoid(proj(D_CONV, D_CONV))

    c, sa, sb = cos_ref[...], sa_ref[...], sb_ref[...]
    zq = proj(2 * D_CONV, D_ATT)
    zk = proj(2 * D_CONV + D_ATT, D_ATT)
    zqi = proj(2 * D_CONV + 3 * D_ATT, N_IDX_HEADS * IDX_DIM)
    zv = proj(2 * D_CONV + 2 * D_ATT, D_ATT)
    for g in range(D_ATT // LANES):
        sl = slice(g * LANES, (g + 1) * LANES)
        qh = _rope(_head_rms(zq[:, sl], qg_ref[...]), c, sa, sb) * ATTN_SCALE
        kh = _rope(_head_rms(zk[:, sl], kg_ref[...]), c, sa, sb)
        qih = _rope(zqi[:, sl], c, sa, sb)
        if transposed:
            kb_ref, vt_ref = extra[0], extra[1]
            q_ref[sl, :] = qh.T.astype(BF16)
            qi_ref[sl, :] = qih.T.astype(BF16)
            kb_ref[:, sl] = kh.astype(BF16)
            kall_ref[sl, :] = kh.T
            vt32 = zv[:, sl].T
            vall_ref[sl, :] = vt32
            vt = vt32.astype(BF16)
            ones = jnp.ones((ONES_ROWS, tm), BF16)
            for half in range(2):
                r0 = (2 * g + half) * V_SLAB
                vt_ref[r0:r0 + HEAD_DIM, :] = vt[half * HEAD_DIM:(half + 1) * HEAD_DIM, :]
                vt_ref[r0 + HEAD_DIM:r0 + V_SLAB, :] = ones
        else:
            q_ref[:, sl] = qh.astype(BF16)
            qi_ref[:, sl] = qih.astype(BF16)
            extra[0][:, sl] = kh
            extra[1][:, sl] = zv[:, sl]
            for half in range(2):
                hs = slice(half * HEAD_DIM, (half + 1) * HEAD_DIM)
                head_rows = pl.ds(2 * g + half, tm, stride=N_HEADS)
                kall_ref[head_rows, :] = kh[:, hs]
                vall_ref[head_rows, :] = zv[:, sl][:, hs]

    zkiw = proj(COL_KIW, LANES)
    lane = lax.broadcasted_iota(I32, zkiw.shape, 1)
    is_ki = lane < IDX_DIM
    roped = _rope(zkiw, jnp.where(is_ki, c, 1.0), jnp.where(is_ki, sa, 0.0), jnp.where(is_ki, sb, 0.0))
    kiw = jnp.where(is_ki, roped, zkiw * IDX_SCALE)
    if transposed:
        kiwt = kiw.T
        kiall_ref[...] = kiwt[0:IDX_DIM, :]
        extra[2][...] = kiwt[IDX_DIM:IDX_DIM + N_IDX_HEADS, :]
        extra[3][...] = kiw[:, 0:IDX_DIM].astype(BF16)
    else:
        kiall_ref[...] = kiw[:, 0:IDX_DIM]
        extra[2][...] = kiw

    gc_ref[...] = jax.nn.sigmoid(proj(COL_GC, D_MODEL))
    ga_ref[...] = jax.nn.sigmoid(proj(COL_GA, D_MODEL))


def _kv_all_shapes(depth, b, t, transposed):
    if transposed:
        return ((depth * b * D_ATT, t), (depth * b * D_ATT, t), (depth * b * IDX_DIM, t))
    return ((depth * b * t * N_HEADS, HEAD_DIM), (depth * b * t * N_HEADS, HEAD_DIM), (depth * b * t, IDX_DIM))


def _inproj(x, g, w, cos, sa, sb, qg, kg, kv_all, layer, depth, b, tm, n_tab_blocks, transposed):
    n = x.shape[0]
    t = n // b
    nb = n // tm
    nt = t // tm if transposed else None
    row = lambda width: pl.BlockSpec((tm, width), lambda i: (i, 0))
    col = lambda height: pl.BlockSpec((height, tm), lambda i: (0, i))
    tab = pl.BlockSpec((tm, LANES), lambda i: (i % n_tab_blocks, 0))
    vec = lambda width: pl.BlockSpec((1, width), lambda i: (0, 0))
    rows_of = lambda width, dt: (jax.ShapeDtypeStruct((n, width), dt), row(width))
    cols_of = lambda height, dt: (jax.ShapeDtypeStruct((height, n), dt), col(height))
    shapes = _kv_all_shapes(depth, b, t, transposed)
    if transposed:
        slab = lambda height: pl.BlockSpec((height, tm), lambda i: (layer * b + i // nt, i % nt))
        all_specs = [slab(D_ATT), slab(D_ATT), slab(IDX_DIM)]
    else:
        rows5 = lambda r, width: pl.BlockSpec((r, width), lambda i: (layer * nb + i, 0))
        all_specs = [rows5(tm * N_HEADS, HEAD_DIM), rows5(tm * N_HEADS, HEAD_DIM), rows5(tm, IDX_DIM)]
    outs = [
        rows_of(D_CONV, F32),
        cols_of(D_ATT, BF16) if transposed else rows_of(D_ATT, BF16),
        cols_of(D_ATT, BF16) if transposed else rows_of(D_ATT, BF16),
        rows_of(D_MODEL, F32),
        rows_of(D_MODEL, F32),
    ] + [(jax.ShapeDtypeStruct(s, F32), spec) for s, spec in zip(shapes, all_specs)]
    n_common = len(outs) - len(shapes)
    if transposed:
        outs += [rows_of(D_ATT, BF16), cols_of(N_HEADS * V_SLAB, BF16), cols_of(N_IDX_HEADS, F32),
                 rows_of(IDX_DIM, BF16)]
    else:
        outs += [rows_of(D_ATT, F32), rows_of(D_ATT, F32), rows_of(LANES, F32)]
    aliased = tuple(kv_all)
    n_in = 8
    return pl.pallas_call(
        functools.partial(_inproj_body, transposed=transposed, n_aliased=len(aliased)),
        grid=(nb,),
        in_specs=[row(D_MODEL), vec(D_MODEL), _resident((D_MODEL, D_IN_PACKED), lambda i: (0, 0)),
                  tab, tab, tab, vec(LANES), vec(LANES)] + [pl.BlockSpec(memory_space=pl.ANY)] * len(aliased),
        out_specs=[o[1] for o in outs],
        out_shape=[o[0] for o in outs],
        input_output_aliases={n_in + a: n_common + a for a in range(len(aliased))},
        compiler_params=_cparams(("parallel",)),
        name="inproj",
    )(x, g, w, cos, sa, sb, qg, kg, *aliased)


def _conv_body(cur_ref, halo_ref, st_ref, cw_ref, cb_ref, lg_ref, lb_ref, wco_ref, gc_ref,
               out_ref, win_ref, sh_ref, act_ref, *, tm, rows):
    i = pl.program_id(1)
    win_ref[0:HALO, :] = jnp.where(i == 0, st_ref[0], halo_ref[0])
    win_ref[HALO:HALO + tm, :] = cur_ref[0]
    span = HALO + tm - SUBLANES
    for s in range(1, SUBLANES):
        sh_ref[s, 0:span, :] = win_ref[s:s + span, :]
    off = HALO - (CONV_W - 1)
    for r in range(tm // rows):
        acc = jnp.zeros((rows, D_CONV), F32) + cb_ref[...]
        for j in range(CONV_W):
            s, base = (off + j) % SUBLANES, r * rows + (off + j) // SUBLANES * SUBLANES
            tap = win_ref[base:base + rows, :] if s == 0 else sh_ref[s, base:base + rows, :]
            acc = acc + tap * cw_ref[j:j + 1, :]
        mu = jnp.mean(acc, axis=-1, keepdims=True)
        d = acc - mu
        var = jnp.mean(d * d, axis=-1, keepdims=True)
        y = d * lax.rsqrt(var + LN_EPS) * lg_ref[...] + lb_ref[...]
        act_ref[r * rows:(r + 1) * rows, :] = (y * jax.nn.sigmoid(y)).astype(BF16)
    out_ref[0] = gc_ref[0] * jnp.dot(act_ref[...], wco_ref[...], preferred_element_type=F32)


def _conv_branch(glu, halo_src, state, cw, cb, lg, lb, wco, gc, tm):
    b, t, _ = glu.shape
    rows = min(tm, 32)
    hb = tm // HALO
    vec = pl.BlockSpec((1, D_CONV), lambda bi, i: (0, 0))
    return pl.pallas_call(
        functools.partial(_conv_body, tm=tm, rows=rows),
        grid=(b, t // tm),
        in_specs=[
            pl.BlockSpec((1, tm, D_CONV), lambda bi, i: (bi, i, 0)),
            pl.BlockSpec((1, HALO, D_CONV), lambda bi, i: (bi, jnp.maximum(i * hb - 1, 0), 0)),
            pl.BlockSpec((1, HALO, D_CONV), lambda bi, i: (bi, 0, 0)),
            pl.BlockSpec((HALO, D_CONV), lambda bi, i: (0, 0)),
            vec, vec, vec,
            _resident((D_CONV, D_MODEL), lambda bi, i: (0, 0)),
            pl.BlockSpec((1, tm, D_MODEL), lambda bi, i: (bi, i, 0)),
        ],
        out_specs=pl.BlockSpec((1, tm, D_MODEL), lambda bi, i: (bi, i, 0)),
        out_shape=jax.ShapeDtypeStruct((b, t, D_MODEL), F32),
        scratch_shapes=[pltpu.VMEM((HALO + tm, D_CONV), F32), pltpu.VMEM((SUBLANES, HALO + tm, D_CONV), F32),
                        pltpu.VMEM((tm, D_CONV), BF16)],
        compiler_params=_cparams(("parallel", "arbitrary")),
        name="conv_branch",
    )(glu, halo_src, state, cw, cb, lg, lb, wco, gc)


def _fold(x, op):
    return op(x.reshape(x.shape[0] // ACC_ROWS, ACC_ROWS, x.shape[1]), axis=0)


def _fin(x8, op):
    return op(x8, axis=0, keepdims=True)


def _any_set(flag):
    return jnp.max(jnp.where(flag, 1, 0))


def _select_threshold(sc_ref, n_units, unit, qb, n_sel, stats, lane_ok):
    mx, mn, n_adm, ge0, gt0 = stats
    kf = float(n_sel)
    log_target = float(np.log(n_sel + 0.5))
    full8 = lambda v: jnp.full((ACC_ROWS, qb), v, F32)
    row = lambda v: jnp.full((1, qb), v, F32)

    def rows(u):
        return pl.ds(pl.multiple_of(u * unit, unit), unit)

    def scan(p, lo, up, snap):
        def blk(u, c):
            x = sc_ref[rows(u), :]
            out = [c[0] + _fold(jnp.where(x >= p, 1.0, 0.0), jnp.sum)]
            if snap:
                out.append(jnp.minimum(c[1], _fold(jnp.where(x >= lo, x, INF), jnp.min)))
                out.append(jnp.maximum(c[2], _fold(jnp.where(x < up, x, -INF), jnp.max)))
            return tuple(out)
        init = (full8(0.0), full8(INF), full8(-INF)) if snap else (full8(0.0),)
        res = lax.fori_loop(0, n_units, blk, init)
        if snap:
            return _fin(res[0], jnp.sum), _fin(res[1], jnp.min), _fin(res[2], jnp.max)
        return _fin(res[0], jnp.sum)

    def pivot(lo, c_lo, up, c_up, wl, wu, bisect):
        xu = jnp.where(up == INF, mx, up)
        gl = (jnp.log(c_lo) - log_target) * wl
        gu = (log_target - jnp.log(jnp.maximum(c_up, 0.5))) * wu
        p = jnp.where(bisect, 0.5 * lo + 0.5 * xu, lo + (xu - lo) * (gl / (gl + gu)))
        p = jnp.minimum(p, xu)
        stuck = jnp.logical_not(p > lo)
        return jnp.where(stuck, xu, p), stuck

    def update(p, c, lo, c_lo, up, c_up, live):
        ge = c >= kf
        to_lo = jnp.logical_and(live, ge)
        to_up = jnp.logical_and(live, jnp.logical_not(ge))
        return (jnp.where(to_lo, p, lo), jnp.where(to_lo, c, c_lo),
                jnp.where(to_up, p, up), jnp.where(to_up, c, c_up), to_lo, to_up)

    def count_pass(st):
        it, _, _, lo, c_lo, up, c_up, wl, wu, side, done = st
        live = done == 0
        p, stuck = pivot(lo, c_lo, up, c_up, wl, wu, it % BISECT_EVERY == BISECT_EVERY - 1)
        c = scan(p, lo, up, False)
        lo, c_lo, up, c_up, to_lo, to_up = update(p, c, lo, c_lo, up, c_up, live)
        wl = jnp.where(jnp.logical_and(to_up, side < 0.0), wl * STALE_END_DECAY, jnp.where(to_lo, 1.0, wl))
        wu = jnp.where(jnp.logical_and(to_lo, side > 0.0), wu * STALE_END_DECAY, jnp.where(to_up, 1.0, wu))
        side = jnp.where(to_lo, 1.0, jnp.where(to_up, -1.0, side))
        done = jnp.where(c_lo == kf, 1, done)
        flags = jnp.sum(jnp.where(done == 0, 1, 0)
                        + jnp.where(jnp.logical_and(stuck, live), 1 << STUCK_FLAG_SHIFT, 0))
        return (it + 1, flags & ((1 << STUCK_FLAG_SHIFT) - 1), flags >> STUCK_FLAG_SHIFT,
                lo, c_lo, up, c_up, wl, wu, side, done)

    def snap_pass(st):
        it, _, lo, c_lo, up, c_up, done, tie = st
        live = done == 0
        p, _ = pivot(lo, c_lo, up, c_up, row(1.0), row(1.0), it % 2 == 1)
        c, a, b = scan(p, lo, up, True)
        tied = jnp.logical_and(live, a == b)
        lo2, c_lo2, up2, c_up2, _, _ = update(p, c, lo, c_lo, up, c_up,
                                              jnp.logical_and(live, jnp.logical_not(tied)))
        lo2 = jnp.where(live, jnp.maximum(lo2, a), lo2)
        tie = jnp.where(tied, 1, tie)
        done = jnp.where(jnp.logical_or(tied, c_lo2 == kf), 1, done)
        return (it + 1, _any_set(done == 0), lo2, c_lo2, up2, c_up2, done, tie)

    few = n_adm <= kf
    live0 = jnp.logical_and(jnp.logical_not(few), lane_ok)
    tie0 = jnp.logical_and(live0, jnp.logical_and(gt0 < kf, ge0 >= kf))
    above = ge0 >= kf
    lo0 = jnp.where(few, -INF, jnp.where(tie0, 0.0, jnp.where(above, jnp.maximum(mn, 0.0), mn)))
    c_lo0 = jnp.where(jnp.logical_and(above, mn < 0.0), ge0, n_adm)
    up0 = jnp.where(above, INF, 0.0)
    c_up0 = jnp.where(above, 0.0, ge0)
    done0 = jnp.where(jnp.logical_and(live0, jnp.logical_not(tie0)), 0, 1)
    done0 = jnp.where(c_lo0 == kf, 1, done0)
    tie_init = jnp.where(tie0, 1, 0)
    st = (jnp.int32(0), _any_set(done0 == 0), jnp.int32(0), lo0, c_lo0, up0, c_up0,
          row(1.0), row(1.0), row(0.0), done0)
    st = lax.while_loop(lambda s: (s[0] < SEARCH_FAST_ITERS) & (s[1] > 0) & (s[2] == 0), count_pass, st)
    it1, active1, _, lo1, c_lo1, up1, c_up1, _, _, _, done1 = st
    st = (it1, active1, lo1, c_lo1, up1, c_up1, done1, tie_init)
    st = lax.while_loop(lambda s: (s[0] < SEARCH_MAX_ITERS) & (s[1] > 0), snap_pass, st)
    return st[2], st[7]


def _selection_bias(x, thr, thr_valid, need, seen, tri, with_ties):
    if not with_ties:
        return jnp.where(x >= thr_valid, 0.0, MASK_BIAS), seen
    eq = x == thr
    eqf = jnp.where(eq, 1.0, 0.0)
    pref = jnp.dot(tri, eqf.astype(BF16), preferred_element_type=F32) + seen
    take = jnp.where(eq, jnp.where(pref <= need, 1.0, 0.0), jnp.where(x > thr, 1.0, 0.0))
    bias = jnp.where(jnp.where(x > -INF, take, 0.0) > 0.5, 0.0, MASK_BIAS)
    return bias, seen + jnp.sum(eqf, axis=0, keepdims=True)


def _tie_setup(sc_ref, n_units, unit, lb, qb, n_sel, thr):
    def gt_blk(u, c):
        x = sc_ref[pl.ds(pl.multiple_of(u * unit, unit), unit), :]
        return c + _fold(jnp.where(x > thr, 1.0, 0.0), jnp.sum)
    gt = _fin(lax.fori_loop(0, n_units, gt_blk, jnp.zeros((ACC_ROWS, qb), F32)), jnp.sum)
    ri = lax.broadcasted_iota(I32, (lb, lb), 0)
    ci = lax.broadcasted_iota(I32, (lb, lb), 1)
    return float(n_sel) - gt, jnp.where(ci <= ri, 1.0, 0.0).astype(BF16)


def _dsa_body(qT_ref, qiT_ref, iwT_ref, k_ref, vT_ref, ki_ref, oT_ref,
              sc_ref, qpad_ref, bias_ref, sa_ref, sb_ref, acc_ref, m_ref, alpha_ref, bm_ref, seen_ref,
              *, qb, lb, sub, n_sel):
    j = pl.program_id(1)
    n_kb = j + 1
    lane = lax.broadcasted_iota(I32, (1, qb), 1)
    qchunk = (j * qb + lane) >> CHUNK_SHIFT
    fold, fin = _fold, _fin

    def rows(kb):
        return pl.ds(pl.multiple_of(kb * lb, lb), lb)

    def idx_block(kb, carry, diagonal):
        mx8, mn8, ge8, gt8 = carry
        r0 = pl.multiple_of(kb * lb, lb)
        for s in range(lb // sub):
            kis = ki_ref[pl.ds(r0 + s * sub, sub), :]
            score = jnp.zeros((sub, qb), F32)
            for h in range(N_IDX_HEADS):
                sh = jnp.dot(kis, qiT_ref[h * IDX_DIM:(h + 1) * IDX_DIM, :], preferred_element_type=F32)
                score = score + jnp.maximum(sh, 0.0) * iwT_ref[h:h + 1, :]
            lo_s = hi_s = score
            if diagonal:
                kpos = r0 + s * sub + lax.broadcasted_iota(I32, (sub, qb), 0)
                adm = (kpos >> CHUNK_SHIFT) <= qchunk
                hi_s, lo_s = jnp.where(adm, score, -INF), jnp.where(adm, score, INF)
            sc_ref[pl.ds(r0 + s * sub, sub), :] = hi_s
            mx8 = jnp.maximum(mx8, fold(hi_s, jnp.max))
            mn8 = jnp.minimum(mn8, fold(lo_s, jnp.min))
            ge8 = ge8 + fold(jnp.where(hi_s >= 0.0, 1.0, 0.0), jnp.sum)
            gt8 = gt8 + fold(jnp.where(hi_s > 0.0, 1.0, 0.0), jnp.sum)
        return mx8, mn8, ge8, gt8

    full8 = lambda v: jnp.full((ACC_ROWS, qb), v, F32)
    acc8 = lax.fori_loop(0, j, lambda kb, c: idx_block(kb, c, False),
                         (full8(-INF), full8(INF), full8(0.0), full8(0.0)))
    mx8, mn8, ge8, gt8 = idx_block(j, acc8, True)
    n_adm = ((qchunk + 1) * CHUNK).astype(F32)
    stats = (fin(mx8, jnp.max), fin(mn8, jnp.min), n_adm, fin(ge8, jnp.sum), fin(gt8, jnp.sum))

    n_units, unit = n_kb, lb
    thr, tie = _select_threshold(sc_ref, n_units, unit, qb, n_sel, stats, lane >= 0)
    has_ties = _any_set(tie == 1) > 0
    thr_valid = jnp.maximum(thr, -F32_MAX)

    qpad_ref[...] = jnp.zeros(qpad_ref.shape, BF16)
    for h in range(N_HEADS):
        r = (h % 2) * HEAD_DIM
        qpad_ref[h, r:r + HEAD_DIM, :] = qT_ref[h * HEAD_DIM:(h + 1) * HEAD_DIM, :]

    def attention(with_ties):
        m_ref[...] = jnp.full(m_ref.shape, M_FLOOR, F32)
        acc_ref[...] = jnp.zeros(acc_ref.shape, F32)
        seen_ref[...] = jnp.zeros(seen_ref.shape, F32)
        need, tri = _tie_setup(sc_ref, n_units, unit, lb, qb, n_sel, thr) if with_ties else (None, None)

        def select(kb):
            bias_ref[...], seen_ref[...] = _selection_bias(sc_ref[rows(kb), :], thr, thr_valid, need,
                                                           seen_ref[...], tri, with_ties)

        def scores_head(kb, s_ref, h):
            p2 = (h // 2) * 2 * HEAD_DIM
            s = jnp.dot(k_ref[rows(kb), p2:p2 + 2 * HEAD_DIM], qpad_ref[h],
                        preferred_element_type=F32) + bias_ref[...]
            s_ref[h] = s
            bm_ref[h:h + 1, :] = jnp.max(s, axis=0, keepdims=True)

        def rescale():
            m_old = m_ref[...]
            m_new = jnp.maximum(m_old, bm_ref[...])
            alpha_ref[...] = jnp.exp(m_old - m_new)
            m_ref[...] = m_new

        def values_head(kb, s_ref, h, m_new, alpha):
            p = jnp.exp(s_ref[h] - m_new[h:h + 1, :]).astype(BF16)
            vs = slice(h * V_SLAB, (h + 1) * V_SLAB)
            pv = jnp.dot(vT_ref[vs, rows(kb)], p, preferred_element_type=F32)
            acc_ref[vs, :] = acc_ref[vs, :] * alpha[h:h + 1, :] + pv

        def scores(kb, s_ref):
            select(kb)
            for h in range(N_HEADS):
                scores_head(kb, s_ref, h)
            rescale()

        def values(kb, s_ref):
            m_new, alpha = m_ref[...], alpha_ref[...]
            for h in range(N_HEADS):
                values_head(kb, s_ref, h, m_new, alpha)

        def step(kb, s_prev, s_cur):
            m_new, alpha = m_ref[...], alpha_ref[...]
            select(kb)
            for h in range(N_HEADS):
                scores_head(kb, s_cur, h)
                values_head(kb - 1, s_prev, h, m_new, alpha)
            rescale()

        scores(0, sa_ref)

        def body(kb, carry):
            @pl.when(kb % 2 == 1)
            def _():
                step(kb, sa_ref, sb_ref)

            @pl.when(kb % 2 == 0)
            def _():
                step(kb, sb_ref, sa_ref)
            return carry

        lax.fori_loop(1, n_kb, body, 0)

        @pl.when(n_kb % 2 == 1)
        def _():
            values(n_kb - 1, sa_ref)

        @pl.when(n_kb % 2 == 0)
        def _():
            values(n_kb - 1, sb_ref)

        for h in range(N_HEADS):
            num = acc_ref[h * V_SLAB:h * V_SLAB + HEAD_DIM, :]
            den = acc_ref[h * V_SLAB + HEAD_DIM:h * V_SLAB + HEAD_DIM + 1, :]
            oT_ref[h * HEAD_DIM:(h + 1) * HEAD_DIM, :] = num / den

    @pl.when(has_ties)
    def _():
        attention(True)

    @pl.when(jnp.logical_not(has_ties))
    def _():
        attention(False)


def _dsa(qT, qiT, iwT, k, vT, ki, *, b, qb, lb, n_sel):
    n = qT.shape[1]
    t = n // b
    nqb = t // qb
    assert t % lb == 0 and qb == lb and qb % CHUNK == 0
    body = functools.partial(_dsa_body, qb=qb, lb=lb, sub=64, n_sel=n_sel)
    qcol = lambda height: pl.BlockSpec((height, qb), lambda bi, j: (0, bi * nqb + j))
    return pl.pallas_call(
        body,
        grid=(b, nqb),
        in_specs=[
            qcol(D_ATT), qcol(N_IDX_HEADS * IDX_DIM), qcol(N_IDX_HEADS),
            _resident((t, D_ATT), lambda bi, j: (bi, 0)),
            _resident((N_HEADS * V_SLAB, t), lambda bi, j: (0, bi)),
            _resident((t, IDX_DIM), lambda bi, j: (bi, 0)),
        ],
        out_specs=qcol(D_ATT),
        out_shape=jax.ShapeDtypeStruct((D_ATT, n), F32),
        scratch_shapes=[
            pltpu.VMEM((t, qb), F32),
            pltpu.VMEM((N_HEADS, 2 * HEAD_DIM, qb), BF16),
            pltpu.VMEM((lb, qb), F32),
            pltpu.VMEM((N_HEADS, lb, qb), F32),
            pltpu.VMEM((N_HEADS, lb, qb), F32),
            pltpu.VMEM((N_HEADS * V_SLAB, qb), F32),
            pltpu.VMEM((N_HEADS, qb), F32),
            pltpu.VMEM((N_HEADS, qb), F32),
            pltpu.VMEM((N_HEADS, qb), F32),
            pltpu.VMEM((1, qb), F32),
        ],
        compiler_params=_cparams(("parallel", "arbitrary")),
        name="dsa",
    )(qT, qiT, iwT, k, vT, ki)


def _dsa_sample_body(wq_ref, wqi_ref, iw_ref, kc_ref, vc_ref, kic_ref, kn_ref, vn_ref, kin_ref, o_ref,
                     sc_ref, s_ref, kx_ref, vx_ref, kix_ref, acc_ref,
                     *, lb, n_cache_kb, past, t_new, n_sel):
    qb = LANES
    n_kb = n_cache_kb + 1
    l_valid = past + t_new
    lane = lax.broadcasted_iota(I32, (1, qb), 1)
    qchunk = (past + (lane & (t_new - 1))) >> CHUNK_SHIFT
    full8 = lambda v: jnp.full((ACC_ROWS, qb), v, F32)

    def rows(kb):
        return pl.ds(kb * lb, lb) if isinstance(kb, int) else pl.ds(pl.multiple_of(kb * lb, lb), lb)

    def over_blocks(fn, carry, cache_ref, new_ref):
        def cached(kb, c):
            cols = pl.ds(pl.multiple_of(kb * lb, lb), lb)
            return fn(kb, c, cache_ref[0, :, cols].astype(BF16))
        return fn(n_cache_kb, lax.fori_loop(0, n_cache_kb, cached, carry, unroll=4), new_ref[...])

    for new_ref, stage_ref in ((kn_ref, kx_ref), (vn_ref, vx_ref), (kin_ref, kix_ref)):
        feat = stage_ref.shape[0]
        new = new_ref[0]
        if feat < LANES:
            new = jnp.concatenate([new, jnp.zeros((t_new, LANES - feat), F32)], axis=1)
        new = jnp.concatenate([new, jnp.zeros((LANES - t_new, new.shape[1]), F32)], axis=0)
        stage_ref[...] = jnp.zeros(stage_ref.shape, BF16)
        stage_ref[:, 0:LANES] = new.T[0:feat, :].astype(BF16)

    def idx_block(kb, carry, kit):
        mx8, mn8, n8, ge8, gt8 = carry
        r = jnp.maximum(jnp.dot(wqi_ref[0], kit, preferred_element_type=F32).T, 0.0) * iw_ref[0]
        for shift in (t_new, 2 * t_new, 4 * t_new):
            r = r + pltpu.roll(r, shift, 1)
        kpos = kb * lb + lax.broadcasted_iota(I32, (lb, qb), 0)
        adm = jnp.where(kpos < l_valid, kpos >> CHUNK_SHIFT, qchunk + 1) <= qchunk
        hi_s = jnp.where(adm, r, -INF)
        sc_ref[rows(kb), :] = hi_s
        return (jnp.maximum(mx8, _fold(hi_s, jnp.max)),
                jnp.minimum(mn8, _fold(jnp.where(adm, r, INF), jnp.min)),
                n8 + _fold(jnp.where(adm, 1.0, 0.0), jnp.sum),
                ge8 + _fold(jnp.where(hi_s >= 0.0, 1.0, 0.0), jnp.sum),
                gt8 + _fold(jnp.where(hi_s > 0.0, 1.0, 0.0), jnp.sum))

    acc8 = over_blocks(idx_block, (full8(-INF), full8(INF), full8(0.0), full8(0.0), full8(0.0)), kic_ref, kix_ref)
    stats = tuple(_fin(a, op) for a, op in zip(acc8, (jnp.max, jnp.min, jnp.sum, jnp.sum, jnp.sum)))
    if n_kb % 2 == 1:
        sc_ref[rows(n_kb), :] = jnp.full((lb, qb), -INF, F32)
    n_units, unit = (n_kb + 1) // 2, 2 * lb
    thr, tie = _select_threshold(sc_ref, n_units, unit, qb, n_sel, stats, lane >= 0)
    has_ties = _any_set(tie == 1) > 0
    thr_valid = jnp.maximum(thr, -F32_MAX)

    def attention(with_ties):
        need, tri = _tie_setup(sc_ref, n_units, unit, lb, qb, n_sel, thr) if with_ties else (None, None)

        def score_block(kb, carry, kt):
            m8, seen = carry
            bias, seen = _selection_bias(sc_ref[rows(kb), :], thr, thr_valid, need, seen, tri, with_ties)
            s = jnp.dot(wq_ref[0], kt, preferred_element_type=F32).T + bias
            s_ref[rows(kb), :] = s
            return jnp.maximum(m8, _fold(s, jnp.max)), seen

        m8, _ = over_blocks(score_block, (full8(M_FLOOR), jnp.zeros((1, qb), F32)), kc_ref, kx_ref)
        m = _fin(m8, jnp.max)
        acc_ref[...] = jnp.zeros(acc_ref.shape, F32)

        def value_block(kb, den8, vt):
            p = jnp.exp(s_ref[rows(kb), :] - m).astype(BF16)
            acc_ref[...] += jnp.dot(vt, p, preferred_element_type=F32)
            return den8 + _fold(p.astype(F32), jnp.sum)

        den = _fin(over_blocks(value_block, full8(0.0), vc_ref, vx_ref), jnp.sum)
        o_all = (acc_ref[...] / den).T
        for h in range(N_HEADS):
            rs = slice(h * t_new, (h + 1) * t_new)
            cs = slice(h * HEAD_DIM, (h + 1) * HEAD_DIM)
            o_ref[0, :, cs] = o_all[rs, cs]

    @pl.when(has_ties)
    def _():
        attention(True)

    @pl.when(jnp.logical_not(has_ties))
    def _():
        attention(False)


def _dsa_sample(q, qi, iw, k_cache, v_cache, ki_cache, layer, k_new, v_new, ki_new, *, lb, n_sel):
    b, t, _ = q.shape
    past = ki_cache.shape[2]
    assert N_HEADS * t == LANES and past % lb == 0 and LANES <= lb
    n_cache_kb = past // lb
    lp = (n_cache_kb + 1 + (n_cache_kb + 1) % 2) * lb
    eye = jnp.eye(N_HEADS, dtype=q.dtype)
    qh = q.reshape(b, t, N_HEADS, HEAD_DIM)
    wq = jnp.einsum("bqhd,gh->bhqgd", qh, eye).reshape(b, LANES, D_ATT)
    wqi = jnp.transpose(qi.reshape(b, t, N_IDX_HEADS, IDX_DIM), (0, 2, 1, 3)).reshape(b, LANES, IDX_DIM)
    iw_row = jnp.transpose(iw, (0, 2, 1)).reshape(b, 1, LANES)
    per_b = lambda shape: pl.BlockSpec((1,) + shape, lambda bi: (bi, 0, 0))
    cache = lambda shape: pl.BlockSpec((1,) + shape, lambda bi: (layer * b + bi, 0, 0))
    body = functools.partial(_dsa_sample_body, lb=lb, n_cache_kb=n_cache_kb, past=past, t_new=t, n_sel=n_sel)
    return pl.pallas_call(
        body,
        grid=(b,),
        in_specs=[per_b((LANES, D_ATT)), per_b((LANES, IDX_DIM)), per_b((1, LANES)),
                  cache((D_ATT, past)), cache((D_ATT, past)), cache((IDX_DIM, past)),
                  per_b((t, D_ATT)), per_b((t, D_ATT)), per_b((t, IDX_DIM))],
        out_specs=per_b((t, D_ATT)),
        out_shape=jax.ShapeDtypeStruct((b, t, D_ATT), F32),
        scratch_shapes=[
            pltpu.VMEM((lp, LANES), F32),
            pltpu.VMEM((lp, LANES), F32),
            pltpu.VMEM((D_ATT, lb), BF16),
            pltpu.VMEM((D_ATT, lb), BF16),
            pltpu.VMEM((IDX_DIM, lb), BF16),
            pltpu.VMEM((D_ATT, LANES), F32),
        ],
        compiler_params=_cparams(("parallel",)),
        name="dsa_sample",
    )(wq, wqi, iw_row, k_cache, v_cache, ki_cache, k_new, v_new, ki_new)


def _tail_body(x_ref, o_ref, mc_ref, ga_ref, wao_ref, wout_ref, g_ref, w1_ref, w2_ref, y_ref,
               *, ff_chunk, o_transposed):
    o = o_ref[...].T if o_transposed else o_ref[...]
    attn = jnp.dot(o.astype(BF16), wao_ref[...], preferred_element_type=F32)
    merged = mc_ref[...] + ga_ref[...] * attn
    x1 = x_ref[...] + jnp.dot(merged.astype(BF16), wout_ref[...], preferred_element_type=F32)
    ms = jnp.mean(x1 * x1, axis=-1, keepdims=True)
    h = (x1 * lax.rsqrt(ms + EPS) * g_ref[...]).astype(BF16)
    y = x1
    for c in range(D_FF // ff_chunk):
        cs = slice(c * ff_chunk, (c + 1) * ff_chunk)
        u = jnp.maximum(jnp.dot(h, w1_ref[:, cs], preferred_element_type=F32), 0.0)
        y = y + jnp.dot((u * u).astype(BF16), w2_ref[cs, :], preferred_element_type=F32)
    y_ref[...] = y


def _tail(x, o, mc, ga, wao, wout, g, w1, w2, tm, o_transposed):
    n = x.shape[0]
    row = lambda width: pl.BlockSpec((tm, width), lambda i: (i, 0))
    full = lambda a: _resident(a.shape, lambda i: (0, 0))
    o_spec = pl.BlockSpec((D_ATT, tm), lambda i: (0, i)) if o_transposed else row(D_ATT)
    return pl.pallas_call(
        functools.partial(_tail_body, ff_chunk=1024, o_transposed=o_transposed),
        grid=(n // tm,),
        in_specs=[row(D_MODEL), o_spec, row(D_MODEL), row(D_MODEL),
                  full(wao), full(wout), pl.BlockSpec((1, D_MODEL), lambda i: (0, 0)), full(w1), full(w2)],
        out_specs=row(D_MODEL),
        out_shape=jax.ShapeDtypeStruct((n, D_MODEL), F32),
        compiler_params=_cparams(("parallel",)),
        name="tail",
    )(x, o, mc, ga, wao, wout, g, w1, w2)


def _rope_tables(pos):
    inv = ROPE_THETA ** (-jnp.arange(ROT_HALF, dtype=F32) / ROT_HALF)
    ang = pos.astype(F32)[:, None] * inv[None, :]
    cos, sin = jnp.cos(ang), jnp.sin(ang)
    r = np.arange(LANES) % HEAD_DIM
    jj = r % ROT_HALF
    first = jnp.asarray(r < ROT_HALF)[None, :]
    second = jnp.asarray((r >= ROT_HALF) & (r < ROT_DIM))[None, :]
    c = jnp.where(first | second, cos[:, jj], 1.0)
    sa = jnp.where(first, -sin[:, jj], 0.0)
    sb = jnp.where(second, sin[:, jj], 0.0)
    return c, sa, sb


def _pack_w_in(w):
    pad = jnp.zeros((D_MODEL, LANES - (D_IN_HEAD - COL_KIW)), w.dtype)
    return jnp.concatenate([w[:, :D_IN_HEAD], pad, w[:, D_IN_HEAD:]], axis=1).astype(BF16)


def _tile2(v):
    return jnp.concatenate([v, v])[None, :].astype(F32)


def _layer(x, tabs, n_tab_blocks, conv_state, caches, kv_all, layer, depth, w, *, tm, conv_tm, dsa_cfg):
    b, t, _ = x.shape
    n = b * t
    prompt = caches is None
    (norm_mix, w_in_p, conv_w, conv_b, ln_g, ln_b, w_conv_out, q_norm, k_norm,
     w_attn_out, w_out, norm_ffn, w_ff1, w_ff2) = w
    xf = x.reshape(n, D_MODEL)
    glu, q, qi, gc, ga, k_all, v_all, ki_all, *extra = _inproj(
        xf, norm_mix[None, :], w_in_p, *tabs, _tile2(q_norm), _tile2(k_norm), kv_all, layer, depth, b,
        tm, n_tab_blocks, prompt)

    glu3 = glu.reshape(b, t, D_CONV)
    state_p = jnp.pad(conv_state, ((0, 0), (HALO - (CONV_W - 1), 0), (0, 0)))
    halo_src = glu3 if t >= HALO else state_p
    cw_p = jnp.pad(conv_w, ((0, HALO - CONV_W), (0, 0)))
    mc = _conv_branch(glu3, halo_src, state_p, cw_p, conv_b[None, :], ln_g[None, :], ln_b[None, :],
                      w_conv_out, gc.reshape(b, t, D_MODEL), conv_tm)
    new_conv = jnp.concatenate([conv_state, glu3], axis=1)[:, -(CONV_W - 1):]

    if prompt:
        kb, vt_ones, iwT, kib = extra
        o = _dsa(q, qi, iwT, kb, vt_ones, kib, b=b, **dsa_cfg)
    else:
        k, v, kiw = extra
        iw = kiw[:, IDX_DIM:IDX_DIM + N_IDX_HEADS].reshape(b, t, N_IDX_HEADS)
        o = _dsa_sample(q.reshape(b, t, D_ATT), qi.reshape(b, t, D_ATT), iw, *caches, layer,
                        k.reshape(b, t, D_ATT), v.reshape(b, t, D_ATT), kiw[:, :IDX_DIM].reshape(b, t, IDX_DIM),
                        **dsa_cfg).reshape(n, D_ATT)

    y = _tail(xf, o, mc.reshape(n, D_MODEL), ga, w_attn_out, w_out, norm_ffn[None, :], w_ff1, w_ff2, tm, prompt)
    return y.reshape(b, t, D_MODEL), (k_all, v_all, ki_all), new_conv


def kernel(x_prompt, x_sample, cache_k, cache_v, cache_kidx, state_conv, norm_mix, w_in, conv_w, conv_b,
           conv_ln_g, conv_ln_b, w_conv_out, q_norm, k_norm, w_attn_out, w_out, norm_ffn, w_ff1, w_ff2):
    bp, tp, _ = x_prompt.shape
    bs, ts, _ = x_sample.shape
    depth = norm_mix.shape[0]
    past = cache_k.shape[2]
    tm = 256
    tabs_p = _rope_tables(jnp.arange(tp, dtype=I32))
    tabs_s = tuple(jnp.tile(a, (bs, 1)) for a in _rope_tables(past + jnp.arange(ts, dtype=I32)))
    cfg_p = dict(qb=256, lb=256, n_sel=min(TOPK_MAX, tp // 4))
    cfg_s = dict(lb=256, n_sel=min(TOPK_MAX, (past + ts) // 4))
    caches = (jnp.transpose(cache_k, (0, 1, 3, 4, 2)).reshape(depth * bs, D_ATT, past),
              jnp.transpose(cache_v, (0, 1, 3, 4, 2)).reshape(depth * bs, D_ATT, past),
              jnp.transpose(cache_kidx, (0, 1, 3, 2)).reshape(depth * bs, IDX_DIM, past))
    kv_p = tuple(jnp.zeros(s, F32) for s in _kv_all_shapes(depth, bp, tp, True))
    kv_s = tuple(jnp.zeros(s, F32) for s in _kv_all_shapes(depth, bs, ts, False))
    hp, hs = x_prompt, x_sample
    conv_p, conv_s = [], []
    for l in range(depth):
        w = (norm_mix[l], _pack_w_in(w_in[l]), conv_w[l], conv_b[l], conv_ln_g[l], conv_ln_b[l],
             w_conv_out[l].astype(BF16), q_norm[l], k_norm[l], w_attn_out[l].astype(BF16),
             w_out[l].astype(BF16), norm_ffn[l], w_ff1[l].astype(BF16), w_ff2[l].astype(BF16))
        zero_state = jnp.zeros((bp, CONV_W - 1, D_CONV), F32)
        hp, kv_p, conv = _layer(hp, tabs_p, tp // tm, zero_state, None, kv_p, l, depth, w,
                                tm=tm, conv_tm=tm, dsa_cfg=cfg_p)
        conv_p.append(conv)
        hs, kv_s, conv = _layer(hs, tabs_s, (bs * ts) // tm, state_conv[l], caches, kv_s, l, depth, w,
                                tm=tm, conv_tm=ts, dsa_cfg=cfg_s)
        conv_s.append(conv)
    heads_p = lambda a: jnp.transpose(a.reshape(depth, bp, N_HEADS, HEAD_DIM, tp), (0, 1, 4, 2, 3))
    heads_s = lambda a: a.reshape(depth, bs, ts, N_HEADS, HEAD_DIM)
    return (hp, hs,
            heads_p(kv_p[0]), heads_p(kv_p[1]),
            jnp.transpose(kv_p[2].reshape(depth, bp, IDX_DIM, tp), (0, 1, 3, 2)), jnp.stack(conv_p),
            heads_s(kv_s[0]), heads_s(kv_s[1]), kv_s[2].reshape(depth, bs, ts, IDX_DIM), jnp.stack(conv_s))
```

```python
import functools

import jax
import jax.numpy as jnp
import numpy as np
from jax import lax
from jax.experimental import pallas as pl
from jax.experimental.pallas import tpu as pltpu

F32 = jnp.float32
BF16 = jnp.bfloat16
I32 = jnp.int32

D_MODEL = 1024
CHUNK = 64
CHUNK_SHIFT = 6
N_HEADS = 8
HEAD_DIM = 64
ROT_DIM = HEAD_DIM // 4
ROT_HALF = ROT_DIM // 2
ROPE_THETA = 500000.0
N_IDX_HEADS = 8
IDX_DIM = 64
TOPK_MAX = 256
D_CONV = 512
CONV_W = 31
D_FF = 4 * D_MODEL
EPS = 1e-6
LN_EPS = 1e-5
ATTN_SCALE = HEAD_DIM ** -0.5
IDX_SCALE = (N_IDX_HEADS ** -0.5) * (IDX_DIM ** -0.5)
D_ATT = N_HEADS * HEAD_DIM

LANES = 128
SUBLANES = 8
ACC_ROWS = 4 * SUBLANES
VMEM_LIMIT_BYTES = 56 * 1024 * 1024

COL_KIW = 6 * 512
COL_GC = COL_KIW + LANES
COL_GA = COL_GC + D_MODEL
D_IN_PACKED = COL_GA + D_MODEL
D_IN_HEAD = 2 * D_CONV + 3 * D_ATT + N_IDX_HEADS * IDX_DIM + IDX_DIM + N_IDX_HEADS

HALO = 32
INF = float("inf")
F32_MAX = float(np.finfo(np.float32).max)
ONES_ROWS = 16
V_SLAB = HEAD_DIM + ONES_ROWS
M_FLOOR = -1e30
MASK_BIAS = -2e30
SEARCH_FAST_ITERS = 48
SEARCH_MAX_ITERS = 4096
BISECT_EVERY = 8
STALE_END_DECAY = 0.5
STUCK_FLAG_SHIFT = 16


def _cparams(sem):
    return pltpu.CompilerParams(dimension_semantics=sem, vmem_limit_bytes=VMEM_LIMIT_BYTES)


def _resident(block_shape, index_map):
    return pl.BlockSpec(block_shape, index_map, pipeline_mode=pl.Buffered(1))


def _rope(x, c, sa, sb):
    return x * c + pltpu.roll(x, LANES - ROT_HALF, 1) * sa + pltpu.roll(x, ROT_HALF, 1) * sb


def _head_rms(x, gain):
    lane = lax.broadcasted_iota(I32, x.shape, 1)
    lo = lane < HEAD_DIM
    x2 = x * x
    s_lo = jnp.sum(jnp.where(lo, x2, 0.0), axis=-1, keepdims=True)
    s_hi = jnp.sum(jnp.where(lo, 0.0, x2), axis=-1, keepdims=True)
    ms = jnp.where(lo, s_lo, s_hi) * (1.0 / HEAD_DIM)
    return x * lax.rsqrt(ms + EPS) * gain


def _inproj_body(*refs, transposed, n_aliased):
    x_ref, g_ref, w_ref, cos_ref, sa_ref, sb_ref, qg_ref, kg_ref = refs[:8]
    glu_ref, q_ref, qi_ref, gc_ref, ga_ref, kall_ref, vall_ref, kiall_ref, *extra = refs[8 + n_aliased:]
    tm = x_ref.shape[0]
    x = x_ref[...]
    ms = jnp.mean(x * x, axis=-1, keepdims=True)
    h = (x * lax.rsqrt(ms + EPS) * g_ref[...]).astype(BF16)

    def proj(c0, n):
        return jnp.dot(h, w_ref[:, c0:c0 + n], preferred_element_type=F32)

    glu_ref[...] = proj(0, D_CONV) * jax.nn.sigmoid(proj(D_CONV, D_CONV))

    c, sa, sb = cos_ref[...], sa_ref[...], sb_ref[...]
    zq = proj(2 * D_CONV, D_ATT)
    zk = proj(2 * D_CONV + D_ATT, D_ATT)
    zqi = proj(2 * D_CONV + 3 * D_ATT, N_IDX_HEADS * IDX_DIM)
    zv = proj(2 * D_CONV + 2 * D_ATT, D_ATT)
    for g in range(D_ATT // LANES):
        sl = slice(g * LANES, (g + 1) * LANES)
        qh = _rope(_head_rms(zq[:, sl], qg_ref[...]), c, sa, sb) * ATTN_SCALE
        kh = _rope(_head_rms(zk[:, sl], kg_ref[...]), c, sa, sb)
        qih = _rope(zqi[:, sl], c, sa, sb)
        if transposed:
            kb_ref, vt_ref = extra[0], extra[1]
            q_ref[sl, :] = qh.T.astype(BF16)
            qi_ref[sl, :] = qih.T.astype(BF16)
            kb_ref[:, sl] = kh.astype(BF16)
            kall_ref[sl, :] = kh.T
            vt32 = zv[:, sl].T
            vall_ref[sl, :] = vt32
            vt = vt32.astype(BF16)
            ones = jnp.ones((ONES_ROWS, tm), BF16)
            for half in range(2):
                r0 = (2 * g + half) * V_SLAB
                vt_ref[r0:r0 + HEAD_DIM, :] = vt[half * HEAD_DIM:(half + 1) * HEAD_DIM, :]
                vt_ref[r0 + HEAD_DIM:r0 + V_SLAB, :] = ones
        else:
            q_ref[:, sl] = qh.astype(BF16)
            qi_ref[:, sl] = qih.astype(BF16)
            extra[0][:, sl] = kh
            extra[1][:, sl] = zv[:, sl]
            for half in range(2):
                hs = slice(half * HEAD_DIM, (half + 1) * HEAD_DIM)
                head_rows = pl.ds(2 * g + half, tm, stride=N_HEADS)
                kall_ref[head_rows, :] = kh[:, hs]
                vall_ref[head_rows, :] = zv[:, sl][:, hs]

    zkiw = proj(COL_KIW, LANES)
    lane = lax.broadcasted_iota(I32, zkiw.shape, 1)
    is_ki = lane < IDX_DIM
    roped = _rope(zkiw, jnp.where(is_ki, c, 1.0), jnp.where(is_ki, sa, 0.0), jnp.where(is_ki, sb, 0.0))
    kiw = jnp.where(is_ki, roped, zkiw * IDX_SCALE)
    if transposed:
        kiwt = kiw.T
        kiall_ref[...] = kiwt[0:IDX_DIM, :]
        extra[2][...] = kiwt[IDX_DIM:IDX_DIM + N_IDX_HEADS, :]
        extra[3][...] = kiw[:, 0:IDX_DIM].astype(BF16)
    else:
        kiall_ref[...] = kiw[:, 0:IDX_DIM]
        extra[2][...] = kiw

    gc_ref[...] = jax.nn.sigmoid(proj(COL_GC, D_MODEL))
    ga_ref[...] = jax.nn.sigmoid(proj(COL_GA, D_MODEL))


def _kv_all_shapes(depth, b, t, transposed):
    if transposed:
        return ((depth * b * D_ATT, t), (depth * b * D_ATT, t), (depth * b * IDX_DIM, t))
    return ((depth * b * t * N_HEADS, HEAD_DIM), (depth * b * t * N_HEADS, HEAD_DIM), (depth * b * t, IDX_DIM))


def _inproj(x, g, w, cos, sa, sb, qg, kg, kv_all, layer, depth, b, tm, n_tab_blocks, transposed):
    n = x.shape[0]
    t = n // b
    nb = n // tm
    nt = t // tm if transposed else None
    row = lambda width: pl.BlockSpec((tm, width), lambda i: (i, 0))
    col = lambda height: pl.BlockSpec((height, tm), lambda i: (0, i))
    tab = pl.BlockSpec((tm, LANES), lambda i: (i % n_tab_blocks, 0))
    vec = lambda width: pl.BlockSpec((1, width), lambda i: (0, 0))
    rows_of = lambda width, dt: (jax.ShapeDtypeStruct((n, width), dt), row(width))
    cols_of = lambda height, dt: (jax.ShapeDtypeStruct((height, n), dt), col(height))
    shapes = _kv_all_shapes(depth, b, t, transposed)
    if transposed:
        slab = lambda height: pl.BlockSpec((height, tm), lambda i: (layer * b + i // nt, i % nt))
        all_specs = [slab(D_ATT), slab(D_ATT), slab(IDX_DIM)]
    else:
        rows5 = lambda r, width: pl.BlockSpec((r, width), lambda i: (layer * nb + i, 0))
        all_specs = [rows5(tm * N_HEADS, HEAD_DIM), rows5(tm * N_HEADS, HEAD_DIM), rows5(tm, IDX_DIM)]
    outs = [
        rows_of(D_CONV, F32),
        cols_of(D_ATT, BF16) if transposed else rows_of(D_ATT, BF16),
        cols_of(D_ATT, BF16) if transposed else rows_of(D_ATT, BF16),
        rows_of(D_MODEL, F32),
        rows_of(D_MODEL, F32),
    ] + [(jax.ShapeDtypeStruct(s, F32), spec) for s, spec in zip(shapes, all_specs)]
    n_common = len(outs) - len(shapes)
    if transposed:
        outs += [rows_of(D_ATT, BF16), cols_of(N_HEADS * V_SLAB, BF16), cols_of(N_IDX_HEADS, F32),
                 rows_of(IDX_DIM, BF16)]
    else:
        outs += [rows_of(D_ATT, F32), rows_of(D_ATT, F32), rows_of(LANES, F32)]
    aliased = tuple(kv_all)
    n_in = 8
    return pl.pallas_call(
        functools.partial(_inproj_body, transposed=transposed, n_aliased=len(aliased)),
        grid=(nb,),
        in_specs=[row(D_MODEL), vec(D_MODEL), _resident((D_MODEL, D_IN_PACKED), lambda i: (0, 0)),
                  tab, tab, tab, vec(LANES), vec(LANES)] + [pl.BlockSpec(memory_space=pl.ANY)] * len(aliased),
        out_specs=[o[1] for o in outs],
        out_shape=[o[0] for o in outs],
        input_output_aliases={n_in + a: n_common + a for a in range(len(aliased))},
        compiler_params=_cparams(("parallel",)),
        name="inproj",
    )(x, g, w, cos, sa, sb, qg, kg, *aliased)


def _conv_body(cur_ref, halo_ref, st_ref, cw_ref, cb_ref, lg_ref, lb_ref, wco_ref, gc_ref,
               out_ref, win_ref, sh_ref, act_ref, *, tm, rows):
    i = pl.program_id(1)
    win_ref[0:HALO, :] = jnp.where(i == 0, st_ref[0], halo_ref[0])
    win_ref[HALO:HALO + tm, :] = cur_ref[0]
    span = HALO + tm - SUBLANES
    for s in range(1, SUBLANES):
        sh_ref[s, 0:span, :] = win_ref[s:s + span, :]
    off = HALO - (CONV_W - 1)
    for r in range(tm // rows):
        acc = jnp.zeros((rows, D_CONV), F32) + cb_ref[...]
        for j in range(CONV_W):
            s, base = (off + j) % SUBLANES, r * rows + (off + j) // SUBLANES * SUBLANES
            tap = win_ref[base:base + rows, :] if s == 0 else sh_ref[s, base:base + rows, :]
            acc = acc + tap * cw_ref[j:j + 1, :]
        mu = jnp.mean(acc, axis=-1, keepdims=True)
        d = acc - mu
        var = jnp.mean(d * d, axis=-1, keepdims=True)
        y = d * lax.rsqrt(var + LN_EPS) * lg_ref[...] + lb_ref[...]
        act_ref[r * rows:(r + 1) * rows, :] = (y * jax.nn.sigmoid(y)).astype(BF16)
    out_ref[0] = gc_ref[0] * jnp.dot(act_ref[...], wco_ref[...], preferred_element_type=F32)


def _conv_branch(glu, halo_src, state, cw, cb, lg, lb, wco, gc, tm):
    b, t, _ = glu.shape
    rows = min(tm, 32)
    hb = tm // HALO
    vec = pl.BlockSpec((1, D_CONV), lambda bi, i: (0, 0))
    return pl.pallas_call(
        functools.partial(_conv_body, tm=tm, rows=rows),
        grid=(b, t // tm),
        in_specs=[
            pl.BlockSpec((1, tm, D_CONV), lambda bi, i: (bi, i, 0)),
            pl.BlockSpec((1, HALO, D_CONV), lambda bi, i: (bi, jnp.maximum(i * hb - 1, 0), 0)),
            pl.BlockSpec((1, HALO, D_CONV), lambda bi, i: (bi, 0, 0)),
            pl.BlockSpec((HALO, D_CONV), lambda bi, i: (0, 0)),
            vec, vec, vec,
            _resident((D_CONV, D_MODEL), lambda bi, i: (0, 0)),
            pl.BlockSpec((1, tm, D_MODEL), lambda bi, i: (bi, i, 0)),
        ],
        out_specs=pl.BlockSpec((1, tm, D_MODEL), lambda bi, i: (bi, i, 0)),
        out_shape=jax.ShapeDtypeStruct((b, t, D_MODEL), F32),
        scratch_shapes=[pltpu.VMEM((HALO + tm, D_CONV), F32), pltpu.VMEM((SUBLANES, HALO + tm, D_CONV), F32),
                        pltpu.VMEM((tm, D_CONV), BF16)],
        compiler_params=_cparams(("parallel", "arbitrary")),
        name="conv_branch",
    )(glu, halo_src, state, cw, cb, lg, lb, wco, gc)


def _fold(x, op):
    return op(x.reshape(x.shape[0] // ACC_ROWS, ACC_ROWS, x.shape[1]), axis=0)


def _fin(x8, op):
    return op(x8, axis=0, keepdims=True)


def _any_set(flag):
    return jnp.max(jnp.where(flag, 1, 0))


def _select_threshold(sc_ref, n_units, unit, qb, n_sel, stats, lane_ok):
    mx, mn, n_adm, ge0, gt0 = stats
    kf = float(n_sel)
    log_target = float(np.log(n_sel + 0.5))
    full8 = lambda v: jnp.full((ACC_ROWS, qb), v, F32)
    row = lambda v: jnp.full((1, qb), v, F32)

    def rows(u):
        return pl.ds(pl.multiple_of(u * unit, unit), unit)

    def scan(p, lo, up, snap):
        def blk(u, c):
            x = sc_ref[rows(u), :]
            out = [c[0] + _fold(jnp.where(x >= p, 1.0, 0.0), jnp.sum)]
            if snap:
                out.append(jnp.minimum(c[1], _fold(jnp.where(x >= lo, x, INF), jnp.min)))
                out.append(jnp.maximum(c[2], _fold(jnp.where(x < up, x, -INF), jnp.max)))
            return tuple(out)
        init = (full8(0.0), full8(INF), full8(-INF)) if snap else (full8(0.0),)
        res = lax.fori_loop(0, n_units, blk, init)
        if snap:
            return _fin(res[0], jnp.sum), _fin(res[1], jnp.min), _fin(res[2], jnp.max)
        return _fin(res[0], jnp.sum)

    def pivot(lo, c_lo, up, c_up, wl, wu, bisect):
        xu = jnp.where(up == INF, mx, up)
        gl = (jnp.log(c_lo) - log_target) * wl
        gu = (log_target - jnp.log(jnp.maximum(c_up, 0.5))) * wu
        p = jnp.where(bisect, 0.5 * lo + 0.5 * xu, lo + (xu - lo) * (gl / (gl + gu)))
        p = jnp.minimum(p, xu)
        stuck = jnp.logical_not(p > lo)
        return jnp.where(stuck, xu, p), stuck

    def update(p, c, lo, c_lo, up, c_up, live):
        ge = c >= kf
        to_lo = jnp.logical_and(live, ge)
        to_up = jnp.logical_and(live, jnp.logical_not(ge))
        return (jnp.where(to_lo, p, lo), jnp.where(to_lo, c, c_lo),
                jnp.where(to_up, p, up), jnp.where(to_up, c, c_up), to_lo, to_up)

    def count_pass(st):
        it, _, _, lo, c_lo, up, c_up, wl, wu, side, done = st
        live = done == 0
        p, stuck = pivot(lo, c_lo, up, c_up, wl, wu, it % BISECT_EVERY == BISECT_EVERY - 1)
        c = scan(p, lo, up, False)
        lo, c_lo, up, c_up, to_lo, to_up = update(p, c, lo, c_lo, up, c_up, live)
        wl = jnp.where(jnp.logical_and(to_up, side < 0.0), wl * STALE_END_DECAY, jnp.where(to_lo, 1.0, wl))
        wu = jnp.where(jnp.logical_and(to_lo, side > 0.0), wu * STALE_END_DECAY, jnp.where(to_up, 1.0, wu))
        side = jnp.where(to_lo, 1.0, jnp.where(to_up, -1.0, side))
        done = jnp.where(c_lo == kf, 1, done)
        flags = jnp.sum(jnp.where(done == 0, 1, 0)
                        + jnp.where(jnp.logical_and(stuck, live), 1 << STUCK_FLAG_SHIFT, 0))
        return (it + 1, flags & ((1 << STUCK_FLAG_SHIFT) - 1), flags >> STUCK_FLAG_SHIFT,
                lo, c_lo, up, c_up, wl, wu, side, done)

    def snap_pass(st):
        it, _, lo, c_lo, up, c_up, done, tie = st
        live = done == 0
        p, _ = pivot(lo, c_lo, up, c_up, row(1.0), row(1.0), it % 2 == 1)
        c, a, b = scan(p, lo, up, True)
        tied = jnp.logical_and(live, a == b)
        lo2, c_lo2, up2, c_up2, _, _ = update(p, c, lo, c_lo, up, c_up,
                                              jnp.logical_and(live, jnp.logical_not(tied)))
        lo2 = jnp.where(live, jnp.maximum(lo2, a), lo2)
        tie = jnp.where(tied, 1, tie)
        done = jnp.where(jnp.logical_or(tied, c_lo2 == kf), 1, done)
        return (it + 1, _any_set(done == 0), lo2, c_lo2, up2, c_up2, done, tie)

    few = n_adm <= kf
    live0 = jnp.logical_and(jnp.logical_not(few), lane_ok)
    tie0 = jnp.logical_and(live0, jnp.logical_and(gt0 < kf, ge0 >= kf))
    above = ge0 >= kf
    lo0 = jnp.where(few, -INF, jnp.where(tie0, 0.0, jnp.where(above, jnp.maximum(mn, 0.0), mn)))
    c_lo0 = jnp.where(jnp.logical_and(above, mn < 0.0), ge0, n_adm)
    up0 = jnp.where(above, INF, 0.0)
    c_up0 = jnp.where(above, 0.0, ge0)
    done0 = jnp.where(jnp.logical_and(live0, jnp.logical_not(tie0)), 0, 1)
    done0 = jnp.where(c_lo0 == kf, 1, done0)
    tie_init = jnp.where(tie0, 1, 0)
    st = (jnp.int32(0), _any_set(done0 == 0), jnp.int32(0), lo0, c_lo0, up0, c_up0,
          row(1.0), row(1.0), row(0.0), done0)
    st = lax.while_loop(lambda s: (s[0] < SEARCH_FAST_ITERS) & (s[1] > 0) & (s[2] == 0), count_pass, st)
    it1, active1, _, lo1, c_lo1, up1, c_up1, _, _, _, done1 = st
    st = (it1, active1, lo1, c_lo1, up1, c_up1, done1, tie_init)
    st = lax.while_loop(lambda s: (s[0] < SEARCH_MAX_ITERS) & (s[1] > 0), snap_pass, st)
    return st[2], st[7]


def _selection_bias(x, thr, thr_valid, need, seen, tri, with_ties):
    if not with_ties:
        return jnp.where(x >= thr_valid, 0.0, MASK_BIAS), seen
    eq = x == thr
    eqf = jnp.where(eq, 1.0, 0.0)
    pref = jnp.dot(tri, eqf.astype(BF16), preferred_element_type=F32) + seen
    take = jnp.where(eq, jnp.where(pref <= need, 1.0, 0.0), jnp.where(x > thr, 1.0, 0.0))
    bias = jnp.where(jnp.where(x > -INF, take, 0.0) > 0.5, 0.0, MASK_BIAS)
    return bias, seen + jnp.sum(eqf, axis=0, keepdims=True)


def _tie_setup(sc_ref, n_units, unit, lb, qb, n_sel, thr):
    def gt_blk(u, c):
        x = sc_ref[pl.ds(pl.multiple_of(u * unit, unit), unit), :]
        return c + _fold(jnp.where(x > thr, 1.0, 0.0), jnp.sum)
    gt = _fin(lax.fori_loop(0, n_units, gt_blk, jnp.zeros((ACC_ROWS, qb), F32)), jnp.sum)
    ri = lax.broadcasted_iota(I32, (lb, lb), 0)
    ci = lax.broadcasted_iota(I32, (lb, lb), 1)
    return float(n_sel) - gt, jnp.where(ci <= ri, 1.0, 0.0).astype(BF16)


def _dsa_body(qT_ref, qiT_ref, iwT_ref, k_ref, vT_ref, ki_ref, oT_ref,
              sc_ref, qpad_ref, bias_ref, sa_ref, sb_ref, acc_ref, m_ref, alpha_ref, bm_ref, seen_ref,
              *, qb, lb, sub, n_sel):
    j = pl.program_id(1)
    n_kb = j + 1
    lane = lax.broadcasted_iota(I32, (1, qb), 1)
    qchunk = (j * qb + lane) >> CHUNK_SHIFT
    fold, fin = _fold, _fin

    def rows(kb):
        return pl.ds(pl.multiple_of(kb * lb, lb), lb)

    def idx_block(kb, carry, diagonal):
        mx8, mn8, ge8, gt8 = carry
        r0 = pl.multiple_of(kb * lb, lb)
        for s in range(lb // sub):
            kis = ki_ref[pl.ds(r0 + s * sub, sub), :]
            score = jnp.zeros((sub, qb), F32)
            for h in range(N_IDX_HEADS):
                sh = jnp.dot(kis, qiT_ref[h * IDX_DIM:(h + 1) * IDX_DIM, :], preferred_element_type=F32)
                score = score + jnp.maximum(sh, 0.0) * iwT_ref[h:h + 1, :]
            lo_s = hi_s = score
            if diagonal:
                kpos = r0 + s * sub + lax.broadcasted_iota(I32, (sub, qb), 0)
                adm = (kpos >> CHUNK_SHIFT) <= qchunk
                hi_s, lo_s = jnp.where(adm, score, -INF), jnp.where(adm, score, INF)
            sc_ref[pl.ds(r0 + s * sub, sub), :] = hi_s
            mx8 = jnp.maximum(mx8, fold(hi_s, jnp.max))
            mn8 = jnp.minimum(mn8, fold(lo_s, jnp.min))
            ge8 = ge8 + fold(jnp.where(hi_s >= 0.0, 1.0, 0.0), jnp.sum)
            gt8 = gt8 + fold(jnp.where(hi_s > 0.0, 1.0, 0.0), jnp.sum)
        return mx8, mn8, ge8, gt8

    full8 = lambda v: jnp.full((ACC_ROWS, qb), v, F32)
    acc8 = lax.fori_loop(0, j, lambda kb, c: idx_block(kb, c, False),
                         (full8(-INF), full8(INF), full8(0.0), full8(0.0)))
    mx8, mn8, ge8, gt8 = idx_block(j, acc8, True)
    n_adm = ((qchunk + 1) * CHUNK).astype(F32)
    stats = (fin(mx8, jnp.max), fin(mn8, jnp.min), n_adm, fin(ge8, jnp.sum), fin(gt8, jnp.sum))

    n_units, unit = n_kb, lb
    thr, tie = _select_threshold(sc_ref, n_units, unit, qb, n_sel, stats, lane >= 0)
    has_ties = _any_set(tie == 1) > 0
    thr_valid = jnp.maximum(thr, -F32_MAX)

    qpad_ref[...] = jnp.zeros(qpad_ref.shape, BF16)
    for h in range(N_HEADS):
        r = (h % 2) * HEAD_DIM
        qpad_ref[h, r:r + HEAD_DIM, :] = qT_ref[h * HEAD_DIM:(h + 1) * HEAD_DIM, :]

    def attention(with_ties):
        m_ref[...] = jnp.full(m_ref.shape, M_FLOOR, F32)
        acc_ref[...] = jnp.zeros(acc_ref.shape, F32)
        seen_ref[...] = jnp.zeros(seen_ref.shape, F32)
        need, tri = _tie_setup(sc_ref, n_units, unit, lb, qb, n_sel, thr) if with_ties else (None, None)

        def select(kb):
            bias_ref[...], seen_ref[...] = _selection_bias(sc_ref[rows(kb), :], thr, thr_valid, need,
                                                           seen_ref[...], tri, with_ties)

        def scores_head(kb, s_ref, h):
            p2 = (h // 2) * 2 * HEAD_DIM
            s = jnp.dot(k_ref[rows(kb), p2:p2 + 2 * HEAD_DIM], qpad_ref[h],
                        preferred_element_type=F32) + bias_ref[...]
            s_ref[h] = s
            bm_ref[h:h + 1, :] = jnp.max(s, axis=0, keepdims=True)

        def rescale():
            m_old = m_ref[...]
            m_new = jnp.maximum(m_old, bm_ref[...])
            alpha_ref[...] = jnp.exp(m_old - m_new)
            m_ref[...] = m_new

        def values_head(kb, s_ref, h, m_new, alpha):
            p = jnp.exp(s_ref[h] - m_new[h:h + 1, :]).astype(BF16)
            vs = slice(h * V_SLAB, (h + 1) * V_SLAB)
            pv = jnp.dot(vT_ref[vs, rows(kb)], p, preferred_element_type=F32)
            acc_ref[vs, :] = acc_ref[vs, :] * alpha[h:h + 1, :] + pv

        def scores(kb, s_ref):
            select(kb)
            for h in range(N_HEADS):
                scores_head(kb, s_ref, h)
            rescale()

        def values(kb, s_ref):
            m_new, alpha = m_ref[...], alpha_ref[...]
            for h in range(N_HEADS):
                values_head(kb, s_ref, h, m_new, alpha)

        def step(kb, s_prev, s_cur):
            m_new, alpha = m_ref[...], alpha_ref[...]
            select(kb)
            for h in range(N_HEADS):
                scores_head(kb, s_cur, h)
                values_head(kb - 1, s_prev, h, m_new, alpha)
            rescale()

        scores(0, sa_ref)

        def body(kb, carry):
            @pl.when(kb % 2 == 1)
            def _():
                step(kb, sa_ref, sb_ref)

            @pl.when(kb % 2 == 0)
            def _():
                step(kb, sb_ref, sa_ref)
            return carry

        lax.fori_loop(1, n_kb, body, 0)

        @pl.when(n_kb % 2 == 1)
        def _():
            values(n_kb - 1, sa_ref)

        @pl.when(n_kb % 2 == 0)
        def _():
            values(n_kb - 1, sb_ref)

        for h in range(N_HEADS):
            num = acc_ref[h * V_SLAB:h * V_SLAB + HEAD_DIM, :]
            den = acc_ref[h * V_SLAB + HEAD_DIM:h * V_SLAB + HEAD_DIM + 1, :]
            oT_ref[h * HEAD_DIM:(h + 1) * HEAD_DIM, :] = num / den

    @pl.when(has_ties)
    def _():
        attention(True)

    @pl.when(jnp.logical_not(has_ties))
    def _():
        attention(False)


def _dsa(qT, qiT, iwT, k, vT, ki, *, b, qb, lb, n_sel):
    n = qT.shape[1]
    t = n // b
    nqb = t // qb
    assert t % lb == 0 and qb == lb and qb % CHUNK == 0
    body = functools.partial(_dsa_body, qb=qb, lb=lb, sub=64, n_sel=n_sel)
    qcol = lambda height: pl.BlockSpec((height, qb), lambda bi, j: (0, bi * nqb + j))
    return pl.pallas_call(
        body,
        grid=(b, nqb),
        in_specs=[
            qcol(D_ATT), qcol(N_IDX_HEADS * IDX_DIM), qcol(N_IDX_HEADS),
            _resident((t, D_ATT), lambda bi, j: (bi, 0)),
            _resident((N_HEADS * V_SLAB, t), lambda bi, j: (0, bi)),
            _resident((t, IDX_DIM), lambda bi, j: (bi, 0)),
        ],
        out_specs=qcol(D_ATT),
        out_shape=jax.ShapeDtypeStruct((D_ATT, n), F32),
        scratch_shapes=[
            pltpu.VMEM((t, qb), F32),
            pltpu.VMEM((N_HEADS, 2 * HEAD_DIM, qb), BF16),
            pltpu.VMEM((lb, qb), F32),
            pltpu.VMEM((N_HEADS, lb, qb), F32),
            pltpu.VMEM((N_HEADS, lb, qb), F32),
            pltpu.VMEM((N_HEADS * V_SLAB, qb), F32),
            pltpu.VMEM((N_HEADS, qb), F32),
            pltpu.VMEM((N_HEADS, qb), F32),
            pltpu.VMEM((N_HEADS, qb), F32),
            pltpu.VMEM((1, qb), F32),
        ],
        compiler_params=_cparams(("parallel", "arbitrary")),
        name="dsa",
    )(qT, qiT, iwT, k, vT, ki)


def _dsa_sample_body(wq_ref, wqi_ref, iw_ref, kc_ref, vc_ref, kic_ref, kn_ref, vn_ref, kin_ref, o_ref,
                     sc_ref, s_ref, kx_ref, vx_ref, kix_ref, acc_ref,
                     *, lb, n_cache_kb, past, t_new, n_sel):
    qb = LANES
    n_kb = n_cache_kb + 1
    l_valid = past + t_new
    lane = lax.broadcasted_iota(I32, (1, qb), 1)
    qchunk = (past + (lane & (t_new - 1))) >> CHUNK_SHIFT
    full8 = lambda v: jnp.full((ACC_ROWS, qb), v, F32)

    def rows(kb):
        return pl.ds(kb * lb, lb) if isinstance(kb, int) else pl.ds(pl.multiple_of(kb * lb, lb), lb)

    def over_blocks(fn, carry, cache_ref, new_ref):
        def cached(kb, c):
            cols = pl.ds(pl.multiple_of(kb * lb, lb), lb)
            return fn(kb, c, cache_ref[0, :, cols].astype(BF16))
        return fn(n_cache_kb, lax.fori_loop(0, n_cache_kb, cached, carry, unroll=4), new_ref[...])

    for new_ref, stage_ref in ((kn_ref, kx_ref), (vn_ref, vx_ref), (kin_ref, kix_ref)):
        feat = stage_ref.shape[0]
        new = new_ref[0]
        if feat < LANES:
            new = jnp.concatenate([new, jnp.zeros((t_new, LANES - feat), F32)], axis=1)
        new = jnp.concatenate([new, jnp.zeros((LANES - t_new, new.shape[1]), F32)], axis=0)
        stage_ref[...] = jnp.zeros(stage_ref.shape, BF16)
        stage_ref[:, 0:LANES] = new.T[0:feat, :].astype(BF16)

    def idx_block(kb, carry, kit):
        mx8, mn8, n8, ge8, gt8 = carry
        r = jnp.maximum(jnp.dot(wqi_ref[0], kit, preferred_element_type=F32).T, 0.0) * iw_ref[0]
        for shift in (t_new, 2 * t_new, 4 * t_new):
            r = r + pltpu.roll(r, shift, 1)
        kpos = kb * lb + lax.broadcasted_iota(I32, (lb, qb), 0)
        adm = jnp.where(kpos < l_valid, kpos >> CHUNK_SHIFT, qchunk + 1) <= qchunk
        hi_s = jnp.where(adm, r, -INF)
        sc_ref[rows(kb), :] = hi_s
        return (jnp.maximum(mx8, _fold(hi_s, jnp.max)),
                jnp.minimum(mn8, _fold(jnp.where(adm, r, INF), jnp.min)),
                n8 + _fold(jnp.where(adm, 1.0, 0.0), jnp.sum),
                ge8 + _fold(jnp.where(hi_s >= 0.0, 1.0, 0.0), jnp.sum),
                gt8 + _fold(jnp.where(hi_s > 0.0, 1.0, 0.0), jnp.sum))

    acc8 = over_blocks(idx_block, (full8(-INF), full8(INF), full8(0.0), full8(0.0), full8(0.0)), kic_ref, kix_ref)
    stats = tuple(_fin(a, op) for a, op in zip(acc8, (jnp.max, jnp.min, jnp.sum, jnp.sum, jnp.sum)))
    if n_kb % 2 == 1:
        sc_ref[rows(n_kb), :] = jnp.full((lb, qb), -INF, F32)
    n_units, unit = (n_kb + 1) // 2, 2 * lb
    thr, tie = _select_threshold(sc_ref, n_units, unit, qb, n_sel, stats, lane >= 0)
    has_ties = _any_set(tie == 1) > 0
    thr_valid = jnp.maximum(thr, -F32_MAX)

    def attention(with_ties):
        need, tri = _tie_setup(sc_ref, n_units, unit, lb, qb, n_sel, thr) if with_ties else (None, None)

        def score_block(kb, carry, kt):
            m8, seen = carry
            bias, seen = _selection_bias(sc_ref[rows(kb), :], thr, thr_valid, need, seen, tri, with_ties)
            s = jnp.dot(wq_ref[0], kt, preferred_element_type=F32).T + bias
            s_ref[rows(kb), :] = s
            return jnp.maximum(m8, _fold(s, jnp.max)), seen

        m8, _ = over_blocks(score_block, (full8(M_FLOOR), jnp.zeros((1, qb), F32)), kc_ref, kx_ref)
        m = _fin(m8, jnp.max)
        acc_ref[...] = jnp.zeros(acc_ref.shape, F32)

        def value_block(kb, den8, vt):
            p = jnp.exp(s_ref[rows(kb), :] - m).astype(BF16)
            acc_ref[...] += jnp.dot(vt, p, preferred_element_type=F32)
            return den8 + _fold(p.astype(F32), jnp.sum)

        den = _fin(over_blocks(value_block, full8(0.0), vc_ref, vx_ref), jnp.sum)
        o_all = (acc_ref[...] / den).T
        for h in range(N_HEADS):
            rs = slice(h * t_new, (h + 1) * t_new)
            cs = slice(h * HEAD_DIM, (h + 1) * HEAD_DIM)
            o_ref[0, :, cs] = o_all[rs, cs]

    @pl.when(has_ties)
    def _():
        attention(True)

    @pl.when(jnp.logical_not(has_ties))
    def _():
        attention(False)


def _dsa_sample(q, qi, iw, k_cache, v_cache, ki_cache, layer, k_new, v_new, ki_new, *, lb, n_sel):
    b, t, _ = q.shape
    past = ki_cache.shape[2]
    assert N_HEADS * t == LANES and past % lb == 0 and LANES <= lb
    n_cache_kb = past // lb
    lp = (n_cache_kb + 1 + (n_cache_kb + 1) % 2) * lb
    eye = jnp.eye(N_HEADS, dtype=q.dtype)
    qh = q.reshape(b, t, N_HEADS, HEAD_DIM)
    wq = jnp.einsum("bqhd,gh->bhqgd", qh, eye).reshape(b, LANES, D_ATT)
    wqi = jnp.transpose(qi.reshape(b, t, N_IDX_HEADS, IDX_DIM), (0, 2, 1, 3)).reshape(b, LANES, IDX_DIM)
    iw_row = jnp.transpose(iw, (0, 2, 1)).reshape(b, 1, LANES)
    per_b = lambda shape: pl.BlockSpec((1,) + shape, lambda bi: (bi, 0, 0))
    cache = lambda shape: pl.BlockSpec((1,) + shape, lambda bi: (layer * b + bi, 0, 0))
    body = functools.partial(_dsa_sample_body, lb=lb, n_cache_kb=n_cache_kb, past=past, t_new=t, n_sel=n_sel)
    return pl.pallas_call(
        body,
        grid=(b,),
        in_specs=[per_b((LANES, D_ATT)), per_b((LANES, IDX_DIM)), per_b((1, LANES)),
                  cache((D_ATT, past)), cache((D_ATT, past)), cache((IDX_DIM, past)),
                  per_b((t, D_ATT)), per_b((t, D_ATT)), per_b((t, IDX_DIM))],
        out_specs=per_b((t, D_ATT)),
        out_shape=jax.ShapeDtypeStruct((b, t, D_ATT), F32),
        scratch_shapes=[
            pltpu.VMEM((lp, LANES), F32),
            pltpu.VMEM((lp, LANES), F32),
            pltpu.VMEM((D_ATT, lb), BF16),
            pltpu.VMEM((D_ATT, lb), BF16),
            pltpu.VMEM((IDX_DIM, lb), BF16),
            pltpu.VMEM((D_ATT, LANES), F32),
        ],
        compiler_params=_cparams(("parallel",)),
        name="dsa_sample",
    )(wq, wqi, iw_row, k_cache, v_cache, ki_cache, k_new, v_new, ki_new)


def _tail_body(x_ref, o_ref, mc_ref, ga_ref, wao_ref, wout_ref, g_ref, w1_ref, w2_ref, y_ref,
               *, ff_chunk, o_transposed):
    o = o_ref[...].T if o_transposed else o_ref[...]
    attn = jnp.dot(o.astype(BF16), wao_ref[...], preferred_element_type=F32)
    merged = mc_ref[...] + ga_ref[...] * attn
    x1 = x_ref[...] + jnp.dot(merged.astype(BF16), wout_ref[...], preferred_element_type=F32)
    ms = jnp.mean(x1 * x1, axis=-1, keepdims=True)
    h = (x1 * lax.rsqrt(ms + EPS) * g_ref[...]).astype(BF16)
    y = x1
    for c in range(D_FF // ff_chunk):
        cs = slice(c * ff_chunk, (c + 1) * ff_chunk)
        u = jnp.maximum(jnp.dot(h, w1_ref[:, cs], preferred_element_type=F32), 0.0)
        y = y + jnp.dot((u * u).astype(BF16), w2_ref[cs, :], preferred_element_type=F32)
    y_ref[...] = y


def _tail(x, o, mc, ga, wao, wout, g, w1, w2, tm, o_transposed):
    n = x.shape[0]
    row = lambda width: pl.BlockSpec((tm, width), lambda i: (i, 0))
    full = lambda a: _resident(a.shape, lambda i: (0, 0))
    o_spec = pl.BlockSpec((D_ATT, tm), lambda i: (0, i)) if o_transposed else row(D_ATT)
    return pl.pallas_call(
        functools.partial(_tail_body, ff_chunk=1024, o_transposed=o_transposed),
        grid=(n // tm,),
        in_specs=[row(D_MODEL), o_spec, row(D_MODEL), row(D_MODEL),
                  full(wao), full(wout), pl.BlockSpec((1, D_MODEL), lambda i: (0, 0)), full(w1), full(w2)],
        out_specs=row(D_MODEL),
        out_shape=jax.ShapeDtypeStruct((n, D_MODEL), F32),
        compiler_params=_cparams(("parallel",)),
        name="tail",
    )(x, o, mc, ga, wao, wout, g, w1, w2)


def _rope_tables(pos):
    inv = ROPE_THETA ** (-jnp.arange(ROT_HALF, dtype=F32) / ROT_HALF)
    ang = pos.astype(F32)[:, None] * inv[None, :]
    cos, sin = jnp.cos(ang), jnp.sin(ang)
    r = np.arange(LANES) % HEAD_DIM
    jj = r % ROT_HALF
    first = jnp.asarray(r < ROT_HALF)[None, :]
    second = jnp.asarray((r >= ROT_HALF) & (r < ROT_DIM))[None, :]
    c = jnp.where(first | second, cos[:, jj], 1.0)
    sa = jnp.where(first, -sin[:, jj], 0.0)
    sb = jnp.where(second, sin[:, jj], 0.0)
    return c, sa, sb


def _pack_w_in(w):
    pad = jnp.zeros((D_MODEL, LANES - (D_IN_HEAD - COL_KIW)), w.dtype)
    return jnp.concatenate([w[:, :D_IN_HEAD], pad, w[:, D_IN_HEAD:]], axis=1).astype(BF16)


def _tile2(v):
    return jnp.concatenate([v, v])[None, :].astype(F32)


def _layer(x, tabs, n_tab_blocks, conv_state, caches, kv_all, layer, depth, w, *, tm, conv_tm, dsa_cfg):
    b, t, _ = x.shape
    n = b * t
    prompt = caches is None
    (norm_mix, w_in_p, conv_w, conv_b, ln_g, ln_b, w_conv_out, q_norm, k_norm,
     w_attn_out, w_out, norm_ffn, w_ff1, w_ff2) = w
    xf = x.reshape(n, D_MODEL)
    glu, q, qi, gc, ga, k_all, v_all, ki_all, *extra = _inproj(
        xf, norm_mix[None, :], w_in_p, *tabs, _tile2(q_norm), _tile2(k_norm), kv_all, layer, depth, b,
        tm, n_tab_blocks, prompt)

    glu3 = glu.reshape(b, t, D_CONV)
    state_p = jnp.pad(conv_state, ((0, 0), (HALO - (CONV_W - 1), 0), (0, 0)))
    halo_src = glu3 if t >= HALO else state_p
    cw_p = jnp.pad(conv_w, ((0, HALO - CONV_W), (0, 0)))
    mc = _conv_branch(glu3, halo_src, state_p, cw_p, conv_b[None, :], ln_g[None, :], ln_b[None, :],
                      w_conv_out, gc.reshape(b, t, D_MODEL), conv_tm)
    new_conv = jnp.concatenate([conv_state, glu3], axis=1)[:, -(CONV_W - 1):]

    if prompt:
        kb, vt_ones, iwT, kib = extra
        o = _dsa(q, qi, iwT, kb, vt_ones, kib, b=b, **dsa_cfg)
    else:
        k, v, kiw = extra
        iw = kiw[:, IDX_DIM:IDX_DIM + N_IDX_HEADS].reshape(b, t, N_IDX_HEADS)
        o = _dsa_sample(q.reshape(b, t, D_ATT), qi.reshape(b, t, D_ATT), iw, *caches, layer,
                        k.reshape(b, t, D_ATT), v.reshape(b, t, D_ATT), kiw[:, :IDX_DIM].reshape(b, t, IDX_DIM),
                        **dsa_cfg).reshape(n, D_ATT)

    y = _tail(xf, o, mc.reshape(n, D_MODEL), ga, w_attn_out, w_out, norm_ffn[None, :], w_ff1, w_ff2, tm, prompt)
    return y.reshape(b, t, D_MODEL), (k_all, v_all, ki_all), new_conv


def kernel(x_prompt, x_sample, cache_k, cache_v, cache_kidx, state_conv, norm_mix, w_in, conv_w, conv_b,
           conv_ln_g, conv_ln_b, w_conv_out, q_norm, k_norm, w_attn_out, w_out, norm_ffn, w_ff1, w_ff2):
    bp, tp, _ = x_prompt.shape
    bs, ts, _ = x_sample.shape
    depth = norm_mix.shape[0]
    past = cache_k.shape[2]
    tm = 256
    tabs_p = _rope_tables(jnp.arange(tp, dtype=I32))
    tabs_s = tuple(jnp.tile(a, (bs, 1)) for a in _rope_tables(past + jnp.arange(ts, dtype=I32)))
    cfg_p = dict(qb=256, lb=256, n_sel=min(TOPK_MAX, tp // 4))
    cfg_s = dict(lb=256, n_sel=min(TOPK_MAX, (past + ts) // 4))
    caches = (jnp.transpose(cache_k, (0, 1, 3, 4, 2)).reshape(depth * bs, D_ATT, past),
              jnp.transpose(cache_v, (0, 1, 3, 4, 2)).reshape(depth * bs, D_ATT, past),
              jnp.transpose(cache_kidx, (0, 1, 3, 2)).reshape(depth * bs, IDX_DIM, past))
    kv_p = tuple(jnp.zeros(s, F32) for s in _kv_all_shapes(depth, bp, tp, True))
    kv_s = tuple(jnp.zeros(s, F32) for s in _kv_all_shapes(depth, bs, ts, False))
    hp, hs = x_prompt, x_sample
    conv_p, conv_s = [], []
    for l in range(depth):
        w = (norm_mix[l], _pack_w_in(w_in[l]), conv_w[l], conv_b[l], conv_ln_g[l], conv_ln_b[l],
             w_conv_out[l].astype(BF16), q_norm[l], k_norm[l], w_attn_out[l].astype(BF16),
             w_out[l].astype(BF16), norm_ffn[l], w_ff1[l].astype(BF16), w_ff2[l].astype(BF16))
        zero_state = jnp.zeros((bp, CONV_W - 1, D_CONV), F32)
        hp, kv_p, conv = _layer(hp, tabs_p, tp // tm, zero_state, None, kv_p, l, depth, w,
                                tm=tm, conv_tm=tm, dsa_cfg=cfg_p)
        conv_p.append(conv)
        hs, kv_s, conv = _layer(hs, tabs_s, (bs * ts) // tm, state_conv[l], caches, kv_s, l, depth, w,
                                tm=tm, conv_tm=ts, dsa_cfg=cfg_s)
        conv_s.append(conv)
    heads_p = lambda a: jnp.transpose(a.reshape(depth, bp, N_HEADS, HEAD_DIM, tp), (0, 1, 4, 2, 3))
    heads_s = lambda a: a.reshape(depth, bs, ts, N_HEADS, HEAD_DIM)
    return (hp, hs,
            heads_p(kv_p[0]), heads_p(kv_p[1]),
            jnp.transpose(kv_p[2].reshape(depth, bp, IDX_DIM, tp), (0, 1, 3, 2)), jnp.stack(conv_p),
            heads_s(kv_s[0]), heads_s(kv_s[1]), kv_s[2].reshape(depth, bs, ts, IDX_DIM), jnp.stack(conv_s))
```
